```python
import math
import jax, jax.numpy as jnp
from jax import lax
import numpy as np

D_MODEL = 2048
BATCH = 4
SEQ = 2048
DEPTH = 1
DEC_BATCH = 128
DEC_SEQ = 8
PAST_LEN = 16384
PAGE_SIZE = 128

MIX_WIDTH = D_MODEL
S5_WIDTH = MIX_WIDTH // 2
S5_GROUP = 16
S5_GROUPS = S5_WIDTH // S5_GROUP
S5_STATE = 64
RET_WIDTH = MIX_WIDTH - S5_WIDTH
RET_HEADS = 8
RET_HEAD_DIM = RET_WIDTH // RET_HEADS
RET_CHUNK = 128
ROPE_BASE = 10000.0
D_FF = ((8 * D_MODEL + 2) // 3 + 255) // 256 * 256
IN_WIDTH = S5_WIDTH + 4 * RET_WIDTH
NORM_EPS = 1e-6

kernel_name = "hybrid_s5_retention_decode_step"


def rms_norm(x, g):
    xf = x.astype(jnp.float32)
    y = xf * lax.rsqrt(jnp.mean(xf * xf, axis=-1, keepdims=True) + NORM_EPS)
    return (y * g.astype(jnp.float32)).astype(x.dtype)


def rotary(x, pos):
    half = x.shape[-1] // 2
    inv_freq = ROPE_BASE ** (-jnp.arange(half, dtype=jnp.float32) / half)
    ang = pos[:, None] * inv_freq[None, :]
    cos, sin = jnp.cos(ang), jnp.sin(ang)
    x1, x2 = x[..., :half], x[..., half:]
    return jnp.concatenate([x1 * cos - x2 * sin, x1 * sin + x2 * cos], axis=-1)


def _lin_combine(e_i, e_j):
    a_i, b_i = e_i
    a_j, b_j = e_j
    return a_j * a_i, a_j * b_i + b_j


def s5_mixer(u, x0_re, x0_im, lam_re, lam_im, log_step, b_re, b_im, c_re, c_im, d, w_glu, b_glu):
    n, l, _ = u.shape
    f32 = jnp.float32
    lam = lax.complex(lam_re.astype(f32), lam_im.astype(f32))
    dt = jnp.exp(log_step.astype(f32))
    lam_dt = lam * dt[:, None]
    lam_bar = jnp.exp(lam_dt)
    b = lax.complex(b_re.astype(f32), b_im.astype(f32))
    b_bar = ((lam_bar - 1.0) / lam)[..., None] * b
    c = lax.complex(c_re.astype(f32), c_im.astype(f32))
    uf = u.astype(f32)
    ug = uf.reshape(n, l, S5_GROUPS, S5_GROUP)
    bu = jnp.einsum('nlgh,gph->lngp', ug.astype(jnp.complex64), b_bar)
    x0 = lax.complex(x0_re.astype(f32), x0_im.astype(f32))
    bu = bu.at[0].add(lam_bar[None] * x0)
    a = jnp.broadcast_to(lam_bar, (l, 1, S5_GROUPS, S5_STATE))
    _, xs = lax.associative_scan(_lin_combine, (a, bu), axis=0)
    y = jnp.einsum('lngp,ghp->nlgh', xs, c).real.reshape(n, l, S5_WIDTH)
    y = y + d.astype(f32) * uf
    y = jax.nn.gelu(y)
    y = y * jax.nn.sigmoid(y @ w_glu.astype(f32) + b_glu.astype(f32))
    x_last = xs[-1]
    return y.astype(u.dtype), jnp.real(x_last), jnp.imag(x_last)


def retention_chunked(q, k, v, r0):
    n, h, l, dk = q.shape
    dv = v.shape[-1]
    chunk = math.gcd(l, RET_CHUNK)
    nc = l // chunk
    log_gamma = jnp.log(1.0 - 2.0 ** (-5.0 - jnp.arange(h, dtype=jnp.float32)))
    idx = jnp.arange(chunk, dtype=jnp.float32)
    diff = idx[:, None] - idx[None, :]
    mask = jnp.where(diff >= 0, jnp.exp(log_gamma[:, None, None] * jnp.maximum(diff, 0.0)), 0.0)
    q_decay = jnp.exp(log_gamma[:, None] * (idx + 1.0))[..., None]
    k_decay = jnp.exp(log_gamma[:, None] * (chunk - 1.0 - idx))[..., None]
    chunk_decay = jnp.exp(log_gamma * chunk)[:, None, None]

    def to_chunks(t):
        return t.reshape(n, h, nc, chunk, t.shape[-1]).transpose(2, 0, 1, 3, 4)

    def step(r, inp):
        qc, kc, vc = inp
        scores = jnp.einsum('bhnd,bhmd->bhnm', qc, kc) * mask
        o = jnp.einsum('bhnm,bhme->bhne', scores, vc) + jnp.einsum('bhnd,bhde->bhne', qc, r) * q_decay
        r_new = r * chunk_decay + jnp.einsum('bhmd,bhme->bhde', kc * k_decay, vc)
        return r_new, o

    r_last, o = lax.scan(step, r0, (to_chunks(q), to_chunks(k), to_chunks(v)))
    o = o.transpose(1, 2, 0, 3, 4).reshape(n, h, l, dv)
    return o, r_last


def hybrid_layer(x, s5_re0, s5_im0, ret0, pos0,
                 norm_mix, w_in, lam_re, lam_im, log_step, b_re, b_im, c_re, c_im, d, w_glu, b_glu,
                 ret_gn_w, w_out, norm_ffn, w_ffn_in, w_ffn_out):
    n, l, _ = x.shape
    f32 = jnp.float32
    h = rms_norm(x, norm_mix)
    proj = h @ w_in
    u, q, k, v, g = jnp.split(proj, [S5_WIDTH, S5_WIDTH + RET_WIDTH, S5_WIDTH + 2 * RET_WIDTH,
                                     S5_WIDTH + 3 * RET_WIDTH], axis=-1)
    s5_out, s5_re, s5_im = s5_mixer(u, s5_re0, s5_im0, lam_re, lam_im, log_step,
                                    b_re, b_im, c_re, c_im, d, w_glu, b_glu)
    def heads(t):
        return t.astype(f32).reshape(n, l, RET_HEADS, RET_HEAD_DIM).transpose(0, 2, 1, 3)
    pos = pos0 + jnp.arange(l, dtype=f32)
    qh = rotary(heads(q), pos)
    kh = rotary(heads(k), pos) * (RET_HEAD_DIM ** -0.5)
    vh = heads(v)
    o, ret_new = retention_chunked(qh, kh, vh, ret0.astype(f32))
    mu = jnp.mean(o, axis=-1, keepdims=True)
    var = jnp.mean(jnp.square(o - mu), axis=-1, keepdims=True)
    o = (o - mu) * lax.rsqrt(var + NORM_EPS)
    o = o.transpose(0, 2, 1, 3).reshape(n, l, RET_WIDTH) * ret_gn_w.astype(f32)
    ret_out = (jax.nn.silu(g.astype(f32)) * o).astype(x.dtype)
    x = x + jnp.concatenate([s5_out, ret_out], axis=-1) @ w_out
    h2 = rms_norm(x, norm_ffn)
    gate, up = jnp.split(h2 @ w_ffn_in, 2, axis=-1)
    x = x + (jax.nn.silu(gate) * up) @ w_ffn_out
    return x, s5_re, s5_im, ret_new


def setup_inputs(seed: int = 0) -> dict:
    key = jax.random.key(seed)
    ks = jax.random.split(key, 24)
    nrm = jax.random.normal
    n_idx = jnp.arange(S5_STATE, dtype=jnp.float32)
    return {
        "x_prompt": nrm(ks[0], (BATCH, SEQ, D_MODEL), jnp.float32),
        "x_sample": nrm(ks[1], (DEC_BATCH, DEC_SEQ, D_MODEL), jnp.float32),
        "state_s5_re": 0.5 * nrm(ks[2], (DEPTH, DEC_BATCH, S5_GROUPS, S5_STATE), jnp.float32),
        "state_s5_im": 0.5 * nrm(ks[3], (DEPTH, DEC_BATCH, S5_GROUPS, S5_STATE), jnp.float32),
        "state_ret": 0.5 * nrm(ks[4], (DEPTH, DEC_BATCH, RET_HEADS, RET_HEAD_DIM, RET_HEAD_DIM), jnp.float32),
        "norm_mix": 1.0 + 0.02 * nrm(ks[5], (DEPTH, D_MODEL), jnp.float32),
        "w_in": nrm(ks[6], (DEPTH, D_MODEL, IN_WIDTH), jnp.float32) * D_MODEL ** -0.5,
        "s5_lambda_re": -0.5 + 0.01 * nrm(ks[7], (DEPTH, S5_GROUPS, S5_STATE), jnp.float32),
        "s5_lambda_im": math.pi * n_idx + 0.01 * nrm(ks[8], (DEPTH, S5_GROUPS, S5_STATE), jnp.float32),
        "s5_log_step": jax.random.uniform(ks[9], (DEPTH, S5_GROUPS), jnp.float32,
                                          minval=math.log(1e-3), maxval=math.log(1e-1)),
        "s5_b_re": nrm(ks[10], (DEPTH, S5_GROUPS, S5_STATE, S5_GROUP), jnp.float32) * (2 * S5_GROUP) ** -0.5,
        "s5_b_im": nrm(ks[11], (DEPTH, S5_GROUPS, S5_STATE, S5_GROUP), jnp.float32) * (2 * S5_GROUP) ** -0.5,
        "s5_c_re": 0.5 * nrm(ks[12], (DEPTH, S5_GROUPS, S5_GROUP, S5_STATE), jnp.float32),
        "s5_c_im": 0.5 * nrm(ks[13], (DEPTH, S5_GROUPS, S5_GROUP, S5_STATE), jnp.float32),
        "s5_d": nrm(ks[14], (DEPTH, S5_WIDTH), jnp.float32),
        "s5_w_glu": nrm(ks[15], (DEPTH, S5_WIDTH, S5_WIDTH), jnp.float32) * S5_WIDTH ** -0.5,
        "s5_b_glu": 0.02 * nrm(ks[16], (DEPTH, S5_WIDTH), jnp.float32),
        "ret_gn_w": 1.0 + 0.02 * nrm(ks[17], (DEPTH, RET_WIDTH), jnp.float32),
        "w_out": nrm(ks[18], (DEPTH, MIX_WIDTH, D_MODEL), jnp.float32) * MIX_WIDTH ** -0.5,
        "norm_ffn": 1.0 + 0.02 * nrm(ks[19], (DEPTH, D_MODEL), jnp.float32),
        "w_ffn_in": nrm(ks[20], (DEPTH, D_MODEL, 2 * D_FF), jnp.float32) * D_MODEL ** -0.5,
        "w_ffn_out": nrm(ks[21], (DEPTH, D_FF, D_MODEL), jnp.float32) * D_FF ** -0.5,
        "norm_final": 1.0 + 0.02 * nrm(ks[22], (D_MODEL,), jnp.float32),
    }


def reference(x_prompt, x_sample, state_s5_re, state_s5_im, state_ret,
              norm_mix, w_in, s5_lambda_re, s5_lambda_im, s5_log_step, s5_b_re, s5_b_im,
              s5_c_re, s5_c_im, s5_d, s5_w_glu, s5_b_glu, ret_gn_w, w_out, norm_ffn,
              w_ffn_in, w_ffn_out, norm_final):
    f32 = jnp.float32
    xp, xs = x_prompt, x_sample
    p_re, p_im, p_ret, s_re, s_im, s_ret = [], [], [], [], [], []
    zero_s5 = jnp.zeros((x_prompt.shape[0], S5_GROUPS, S5_STATE), f32)
    zero_ret = jnp.zeros((x_prompt.shape[0], RET_HEADS, RET_HEAD_DIM, RET_HEAD_DIM), f32)
    for li in range(DEPTH):
        weights = (norm_mix[li], w_in[li], s5_lambda_re[li], s5_lambda_im[li], s5_log_step[li],
                   s5_b_re[li], s5_b_im[li], s5_c_re[li], s5_c_im[li], s5_d[li], s5_w_glu[li],
                   s5_b_glu[li], ret_gn_w[li], w_out[li], norm_ffn[li], w_ffn_in[li], w_ffn_out[li])
        xp, pre, pim, pr = hybrid_layer(xp, zero_s5, zero_s5, zero_ret, jnp.float32(0.0), *weights)
        xs, sre, sim, sr = hybrid_layer(xs, state_s5_re[li], state_s5_im[li], state_ret[li],
                                        jnp.float32(PAST_LEN), *weights)
        p_re.append(pre); p_im.append(pim); p_ret.append(pr)
        s_re.append(sre); s_im.append(sim); s_ret.append(sr)
    y_prompt = rms_norm(xp, norm_final)
    y_sample = rms_norm(xs, norm_final)
    return (y_prompt, y_sample,
            jnp.stack(p_re), jnp.stack(p_im), jnp.stack(p_ret),
            jnp.stack(s_re), jnp.stack(s_im), jnp.stack(s_ret))
```

```python
import functools
import math

import jax
import jax.numpy as jnp
from jax import lax
from jax.experimental import pallas as pl
from jax.experimental.pallas import tpu as pltpu

F32 = jnp.float32
BF16 = jnp.bfloat16

D_MODEL = 2048
S5_WIDTH = 1024
S5_GROUP = 16
S5_GROUPS = 64
S5_STATE = 64
N_STATE = S5_GROUPS * S5_STATE
RET_WIDTH = 1024
RET_HEADS = 8
HEAD_DIM = 128
ROPE_BASE = 10000.0
D_FF = 5632
IN_WIDTH = S5_WIDTH + 4 * RET_WIDTH
NORM_EPS = 1e-6
PAST_LEN = 16384

SUBLANES = 8
SLAB_GROUPS = 8
N_SLABS = S5_GROUPS // SLAB_GROUPS
SLAB_IN = SLAB_GROUPS * S5_GROUP
SLAB_STATE = SLAB_GROUPS * S5_STATE
RET_TILE = 128

VMEM_LIMIT_BYTES = 56 * 1024 * 1024


def _params(*sem):
    return pltpu.CompilerParams(dimension_semantics=sem, vmem_limit_bytes=VMEM_LIMIT_BYTES)


def _rms(xf, g):
    ms = jnp.mean(xf * xf, axis=-1, keepdims=True)
    return xf * lax.rsqrt(ms + NORM_EPS) * g


def _inproj_kernel(x_ref, g_ref, w_ref, o_ref, h_scr):
    @pl.when(pl.program_id(1) == 0)
    def _():
        h_scr[...] = _rms(x_ref[...], g_ref[...]).astype(BF16)

    o_ref[...] = jnp.dot(h_scr[...], w_ref[...], preferred_element_type=F32)


def _inproj(x2d, g, w, tm=512, tn=1024):
    t = x2d.shape[0]
    return pl.pallas_call(
        _inproj_kernel,
        grid=(t // tm, IN_WIDTH // tn),
        in_specs=[
            pl.BlockSpec((tm, D_MODEL), lambda i, j: (i, 0)),
            pl.BlockSpec((1, D_MODEL), lambda i, j: (0, 0)),
            pl.BlockSpec((D_MODEL, tn), lambda i, j: (0, j)),
        ],
        out_specs=pl.BlockSpec((tm, tn), lambda i, j: (i, j)),
        out_shape=jax.ShapeDtypeStruct((t, IN_WIDTH), F32),
        scratch_shapes=[pltpu.VMEM((tm, D_MODEL), BF16)],
        compiler_params=_params("arbitrary", "arbitrary"),
        name="in_proj",
    )(x2d, g, w)


def _s5_param_kernel(lr_ref, li_ref, ls_ref, br_ref, bi_ref, tabr_ref, tabi_ref, bbr_ref, bbi_ref):
    lr = lr_ref[...]
    li = li_ref[...]
    dt = jnp.exp(ls_ref[...])
    mag = jnp.exp(lr * dt)
    ar = mag * jnp.cos(li * dt)
    ai = mag * jnp.sin(li * dt)
    den = lr * lr + li * li
    nr = ar - 1.0
    cr = (nr * lr + ai * li) / den
    ci = (ai * lr - nr * li) / den
    br = br_ref[...]
    bi = bi_ref[...]
    bbr_ref[...] = cr * br - ci * bi
    bbi_ref[...] = cr * bi + ci * br

    rows = lax.broadcasted_iota(jnp.int32, (SUBLANES, N_STATE), 0)
    pr, pi = ar, ai
    for r in range(SUBLANES):
        tabr_ref[3, r:r + 1, :] = pr
        tabi_ref[3, r:r + 1, :] = pi
        step = r + 1
        if step in (1, 2, 4):
            k = (1, 2, 4).index(step)
            keep = rows >= step
            tabr_ref[k] = jnp.where(keep, jnp.broadcast_to(pr, (SUBLANES, N_STATE)), 0.0)
            tabi_ref[k] = jnp.where(keep, jnp.broadcast_to(pi, (SUBLANES, N_STATE)), 0.0)
        pr, pi = pr * ar - pi * ai, pr * ai + pi * ar


def _s5_params(lam_re, lam_im, log_step, b_re, b_im):
    lr = lam_re.reshape(1, N_STATE)
    li = lam_im.reshape(1, N_STATE)
    ls = jnp.repeat(log_step, S5_STATE).reshape(1, N_STATE)
    br = jnp.transpose(b_re, (2, 0, 1)).reshape(S5_GROUP, N_STATE)
    bi = jnp.transpose(b_im, (2, 0, 1)).reshape(S5_GROUP, N_STATE)
    tab = jax.ShapeDtypeStruct((4, SUBLANES, N_STATE), F32)
    bb = jax.ShapeDtypeStruct((S5_GROUP, N_STATE), F32)
    return pl.pallas_call(
        _s5_param_kernel,
        out_shape=(tab, tab, bb, bb),
        name="s5_params",
    )(lr, li, ls, br, bi)


def _block_diag_in(bb):
    blk = bb.reshape(S5_GROUP, N_SLABS, SLAB_GROUPS, S5_STATE)
    eye = jnp.eye(SLAB_GROUPS, dtype=F32)
    w = jnp.einsum("hsgp,gk->sghkp", blk, eye)
    return w.reshape(N_SLABS, SLAB_IN, SLAB_STATE).astype(BF16)


def _block_diag_out(c):
    blk = c.reshape(N_SLABS, SLAB_GROUPS, S5_GROUP, S5_STATE)
    eye = jnp.eye(SLAB_GROUPS, dtype=F32)
    w = jnp.einsum("sghp,gk->sgpkh", blk, eye)
    return w.reshape(N_SLABS, SLAB_STATE, SLAB_IN).astype(BF16)


def _s5_kernel(*refs, tc, chain):
    if chain:
        (u_ref, tabr_ref, tabi_ref, wbr_ref, wbi_ref, cdr_ref, cdi_ref, d_ref, wg_ref, bg_ref,
         o_ref, xlr_ref, xli_ref, xr_s, xi_s, y_s, cbr_s, cbi_s) = refs
    else:
        (u_ref, x0r_ref, x0i_ref, tabr_ref, tabi_ref, wbr_ref, wbi_ref, cdr_ref, cdi_ref, d_ref,
         wg_ref, bg_ref, o_ref, xlr_ref, xli_ref, xr_s, xi_s, y_s) = refs

    if chain:
        @pl.when(pl.program_id(1) == 0)
        def _():
            cbr_s[...] = jnp.zeros_like(cbr_s)
            cbi_s[...] = jnp.zeros_like(cbi_s)

    for s in range(N_SLABS):
        ub = u_ref[:, s * SLAB_IN:(s + 1) * SLAB_IN].astype(BF16)
        cs = slice(s * SLAB_STATE, (s + 1) * SLAB_STATE)
        xr_s[:, cs] = jnp.dot(ub, wbr_ref[s], preferred_element_type=F32)
        xi_s[:, cs] = jnp.dot(ub, wbi_ref[s], preferred_element_type=F32)

    def block(b, carry):
        r0 = pl.multiple_of(b * SUBLANES, SUBLANES)
        for s in range(N_SLABS):
            cs = slice(s * SLAB_STATE, (s + 1) * SLAB_STATE)
            xr = xr_s[pl.ds(r0, SUBLANES), cs]
            xi = xi_s[pl.ds(r0, SUBLANES), cs]
            for k, sh in enumerate((1, 2, 4)):
                pr = tabr_ref[k, :, cs]
                pi = tabi_ref[k, :, cs]
                sr = pltpu.roll(xr, sh, 0)
                si = pltpu.roll(xi, sh, 0)
                xr, xi = xr + (pr * sr - pi * si), xi + (pr * si + pi * sr)
            if chain:
                cr = cbr_s[:, cs]
                ci = cbi_s[:, cs]
            else:
                cr = jnp.broadcast_to(x0r_ref[pl.ds(b, 1), cs], (SUBLANES, SLAB_STATE))
                ci = jnp.broadcast_to(x0i_ref[pl.ds(b, 1), cs], (SUBLANES, SLAB_STATE))
            pr = tabr_ref[3, :, cs]
            pi = tabi_ref[3, :, cs]
            xr = xr + (pr * cr - pi * ci)
            xi = xi + (pr * ci + pi * cr)
            xr_s[pl.ds(r0, SUBLANES), cs] = xr
            xi_s[pl.ds(r0, SUBLANES), cs] = xi
            last_r = xr[SUBLANES - 1:SUBLANES, :]
            last_i = xi[SUBLANES - 1:SUBLANES, :]
            if chain:
                cbr_s[:, cs] = jnp.broadcast_to(last_r, (SUBLANES, SLAB_STATE))
                cbi_s[:, cs] = jnp.broadcast_to(last_i, (SUBLANES, SLAB_STATE))
            else:
                xlr_ref[pl.ds(b, 1), cs] = last_r
                xli_ref[pl.ds(b, 1), cs] = last_i
        return carry

    lax.fori_loop(0, tc // SUBLANES, block, 0)

    if chain:
        xlr_ref[0] = cbr_s[...]
        xli_ref[0] = cbi_s[...]

    for s in range(N_SLABS):
        cs = slice(s * SLAB_STATE, (s + 1) * SLAB_STATE)
        us = slice(s * SLAB_IN, (s + 1) * SLAB_IN)
        y = (jnp.dot(xr_s[:, cs].astype(BF16), cdr_ref[s], preferred_element_type=F32)
             - jnp.dot(xi_s[:, cs].astype(BF16), cdi_ref[s], preferred_element_type=F32))
        y = y + d_ref[:, us] * u_ref[:, us]
        y_s[:, us] = jax.nn.gelu(y)

    y = y_s[...]
    z = jnp.dot(y.astype(BF16), wg_ref[...], preferred_element_type=F32) + bg_ref[...]
    o_ref[...] = (y * jax.nn.sigmoid(z)).astype(o_ref.dtype)


def _s5_mixer(proj, x0, tabs, wb, cd, d, w_glu, b_glu, n_seq, seq_len, tc=256):
    t = proj.shape[0]
    chain = x0 is None
    tabr, tabi = tabs
    const3 = lambda *_: (0, 0, 0)
    const2 = lambda *_: (0, 0)
    weight_specs = [
        pl.BlockSpec((4, SUBLANES, N_STATE), const3),
        pl.BlockSpec((4, SUBLANES, N_STATE), const3),
        pl.BlockSpec((N_SLABS, SLAB_IN, SLAB_STATE), const3),
        pl.BlockSpec((N_SLABS, SLAB_IN, SLAB_STATE), const3),
        pl.BlockSpec((N_SLABS, SLAB_STATE, SLAB_IN), const3),
        pl.BlockSpec((N_SLABS, SLAB_STATE, SLAB_IN), const3),
        pl.BlockSpec((1, S5_WIDTH), const2),
        pl.BlockSpec((S5_WIDTH, S5_WIDTH), const2),
        pl.BlockSpec((1, S5_WIDTH), const2),
    ]
    weights = (tabr, tabi, wb[0], wb[1], cd[0], cd[1], d, w_glu, b_glu)
    scratch = [pltpu.VMEM((tc, N_STATE), F32), pltpu.VMEM((tc, N_STATE), F32),
               pltpu.VMEM((tc, S5_WIDTH), F32)]
    out_y = jax.ShapeDtypeStruct((t, S5_WIDTH), BF16)
    if chain:
        tiles = seq_len // tc
        grid = (n_seq, tiles)
        in_specs = [pl.BlockSpec((tc, S5_WIDTH), lambda n, i: (n * tiles + i, 0))] + weight_specs
        out_specs = (pl.BlockSpec((tc, S5_WIDTH), lambda n, i: (n * tiles + i, 0)),
                     pl.BlockSpec((1, SUBLANES, N_STATE), lambda n, i: (n, 0, 0)),
                     pl.BlockSpec((1, SUBLANES, N_STATE), lambda n, i: (n, 0, 0)))
        st = jax.ShapeDtypeStruct((n_seq, SUBLANES, N_STATE), F32)
        scratch += [pltpu.VMEM((SUBLANES, N_STATE), F32), pltpu.VMEM((SUBLANES, N_STATE), F32)]
        args = (proj,) + weights
        sem = ("arbitrary", "arbitrary")
    else:
        assert seq_len == SUBLANES
        seqs = tc // SUBLANES
        grid = (t // tc,)
        in_specs = [pl.BlockSpec((tc, S5_WIDTH), lambda i: (i, 0)),
                    pl.BlockSpec((seqs, N_STATE), lambda i: (i, 0)),
                    pl.BlockSpec((seqs, N_STATE), lambda i: (i, 0))] + weight_specs
        out_specs = (pl.BlockSpec((tc, S5_WIDTH), lambda i: (i, 0)),
                     pl.BlockSpec((seqs, N_STATE), lambda i: (i, 0)),
                     pl.BlockSpec((seqs, N_STATE), lambda i: (i, 0)))
        st = jax.ShapeDtypeStruct((n_seq, N_STATE), F32)
        args = (proj, x0[0], x0[1]) + weights
        sem = ("arbitrary",)
    y, xlr, xli = pl.pallas_call(
        functools.partial(_s5_kernel, tc=tc, chain=chain),
        grid=grid,
        in_specs=in_specs,
        out_specs=out_specs,
        out_shape=(out_y, st, st),
        scratch_shapes=scratch,
        compiler_params=_params(*sem),
        name="s5_chain" if chain else "s5_step",
    )(*args)
    if chain:
        xlr, xli = xlr[:, 0], xli[:, 0]
    return y, xlr, xli


def _ret_kernel(*refs, n_seq, carry):
    if carry:
        (q_ref, k_ref, v_ref, g_ref, cos_ref, sin_ref, mask_ref, qd_ref, kd_ref, cd_ref, gnw_ref,
         o_ref, rn_ref, qr_s, kdt_s, ob_s, r_s) = refs
    else:
        (q_ref, k_ref, v_ref, g_ref, cos_ref, sin_ref, mask_ref, qd_ref, kd_ref, cd_ref, gnw_ref,
         r0_ref, o_ref, rn_ref, qr_s, kdt_s, ob_s) = refs
    rows_per_seq = RET_TILE // n_seq

    if carry:
        @pl.when(pl.program_id(1) == 0)
        def _():
            r_s[...] = jnp.zeros_like(r_s)

    cos = cos_ref[...]
    sin = sin_ref[...]
    scale = HEAD_DIM ** -0.5
    nt = (((1,), (1,)), ((), ()))
    for h in range(RET_HEADS):
        hs = slice(h * HEAD_DIM, (h + 1) * HEAD_DIM)
        qh = q_ref[:, hs]
        kh = k_ref[:, hs]
        qr = qh * cos + pltpu.roll(qh, HEAD_DIM // 2, 1) * sin
        kr = (kh * cos + pltpu.roll(kh, HEAD_DIM // 2, 1) * sin) * scale
        qb = qr.astype(BF16)
        sc = lax.dot_general(qb, kr.astype(BF16), nt, preferred_element_type=F32) * mask_ref[h]
        ob_s[:, hs] = jnp.dot(sc.astype(BF16), v_ref[:, hs].astype(BF16), preferred_element_type=F32)
        qr_s[:, hs] = qr
        kdt_s[h] = (kr * kd_ref[h]).T

    lane = lax.broadcasted_iota(jnp.int32, (HEAD_DIM, RET_TILE), 1)

    def per_seq(s, c):
        r0 = pl.multiple_of(s * rows_per_seq, rows_per_seq)
        rows = pl.ds(r0, rows_per_seq)
        in_seq = (lane >= r0) & (lane < r0 + rows_per_seq)
        for h in range(RET_HEADS):
            hs = slice(h * HEAD_DIM, (h + 1) * HEAD_DIM)
            r_old = r_s[h] if carry else r0_ref[s, h]
            cross = jnp.dot(qr_s[rows, hs].astype(BF16), r_old.astype(BF16), preferred_element_type=F32)
            ob_s[rows, hs] = ob_s[rows, hs] + cross * qd_ref[h, rows, :]
            kdt = jnp.where(in_seq, kdt_s[h], 0.0).astype(BF16)
            r_new = r_old * cd_ref[h] + jnp.dot(kdt, v_ref[:, hs].astype(BF16), preferred_element_type=F32)
            if carry:
                r_s[h] = r_new
                rn_ref[0, h] = r_new
            else:
                rn_ref[s, h] = r_new
        return c

    lax.fori_loop(0, n_seq, per_seq, 0)

    for h in range(RET_HEADS):
        hs = slice(h * HEAD_DIM, (h + 1) * HEAD_DIM)
        o = ob_s[:, hs]
        mu = jnp.mean(o, axis=-1, keepdims=True)
        oc = o - mu
        var = jnp.mean(oc * oc, axis=-1, keepdims=True)
        on = oc * lax.rsqrt(var + NORM_EPS) * gnw_ref[:, hs]
        o_ref[:, hs] = (jax.nn.silu(g_ref[:, hs]) * on).astype(o_ref.dtype)


def _rotary_tables(pos0, rows, reps):
    half = HEAD_DIM // 2
    inv_freq = ROPE_BASE ** (-jnp.arange(half, dtype=F32) / half)
    pos = pos0 + jnp.arange(rows, dtype=F32)
    ang = pos[:, None] * inv_freq[None, :]
    cos, sin = jnp.cos(ang), jnp.sin(ang)
    cos2 = jnp.concatenate([cos, cos], axis=-1)
    sin2 = jnp.concatenate([-sin, sin], axis=-1)
    return jnp.tile(cos2, (reps, 1)), jnp.tile(sin2, (reps, 1))


def _decay_tables(chunk, n_seq):
    log_gamma = jnp.log(1.0 - 2.0 ** (-5.0 - jnp.arange(RET_HEADS, dtype=F32)))
    idx = jnp.arange(chunk, dtype=F32)
    diff = idx[:, None] - idx[None, :]
    mask = jnp.where(diff >= 0, jnp.exp(log_gamma[:, None, None] * jnp.maximum(diff, 0.0)), 0.0)
    q_decay = jnp.exp(log_gamma[:, None] * (idx + 1.0))
    k_decay = jnp.exp(log_gamma[:, None] * (chunk - 1.0 - idx))
    chunk_decay = jnp.exp(log_gamma * chunk)
    eye = jnp.eye(n_seq, dtype=F32)
    mask_t = jnp.einsum("hab,st->hsatb", mask, eye).reshape(RET_HEADS, RET_TILE, RET_TILE)
    qd_t = jnp.broadcast_to(jnp.tile(q_decay, (1, n_seq))[:, :, None], (RET_HEADS, RET_TILE, HEAD_DIM))
    kd_t = jnp.broadcast_to(jnp.tile(k_decay, (1, n_seq))[:, :, None], (RET_HEADS, RET_TILE, HEAD_DIM))
    return mask_t, qd_t, kd_t, chunk_decay


def _retention(proj, r0, gn_w, n_seq, seq_len, pos0):
    t = proj.shape[0]
    carry = r0 is None
    if carry:
        chunks = seq_len // RET_TILE
        tile_seqs = 1
        cos2, sin2 = _rotary_tables(pos0, seq_len, 1)
        grid = (n_seq, chunks)
        row = lambda n, c: n * chunks + c
        tab_map = lambda n, c: (c, 0)
        state_map = lambda n, c: (n, 0, 0, 0)
        state_block = (1, RET_HEADS, HEAD_DIM, HEAD_DIM)
        sem = ("arbitrary", "arbitrary")
    else:
        tile_seqs = RET_TILE // seq_len
        cos2, sin2 = _rotary_tables(pos0, seq_len, tile_seqs)
        grid = (t // RET_TILE,)
        row = lambda i: i
        tab_map = lambda i: (0, 0)
        state_map = lambda i: (i, 0, 0, 0)
        state_block = (tile_seqs, RET_HEADS, HEAD_DIM, HEAD_DIM)
        sem = ("arbitrary",)
    mask_t, qd_t, kd_t, cd = _decay_tables(RET_TILE // tile_seqs, tile_seqs)

    def col(cb):
        return pl.BlockSpec((RET_TILE, RET_WIDTH), lambda *a: (row(*a), cb))

    const3 = lambda *_: (0, 0, 0)
    in_specs = [
        col(1), col(2), col(3), col(4),
        pl.BlockSpec((RET_TILE, HEAD_DIM), tab_map),
        pl.BlockSpec((RET_TILE, HEAD_DIM), tab_map),
        pl.BlockSpec((RET_HEADS, RET_TILE, RET_TILE), const3),
        pl.BlockSpec((RET_HEADS, RET_TILE, HEAD_DIM), const3),
        pl.BlockSpec((RET_HEADS, RET_TILE, HEAD_DIM), const3),
        pl.BlockSpec(memory_space=pltpu.SMEM),
        pl.BlockSpec((1, RET_WIDTH), lambda *_: (0, 0)),
    ]
    args = [proj, proj, proj, proj, cos2, sin2, mask_t, qd_t, kd_t, cd, gn_w]
    scratch = [pltpu.VMEM((RET_TILE, RET_WIDTH), F32),
               pltpu.VMEM((RET_HEADS, HEAD_DIM, RET_TILE), F32),
               pltpu.VMEM((RET_TILE, RET_WIDTH), F32)]
    if carry:
        scratch.append(pltpu.VMEM((RET_HEADS, HEAD_DIM, HEAD_DIM), F32))
    else:
        in_specs.append(pl.BlockSpec(state_block, state_map))
        args.append(r0)
    return pl.pallas_call(
        functools.partial(_ret_kernel, n_seq=tile_seqs, carry=carry),
        grid=grid,
        in_specs=in_specs,
        out_specs=(pl.BlockSpec((RET_TILE, RET_WIDTH), lambda *a: (row(*a), 0)),
                   pl.BlockSpec(state_block, state_map)),
        out_shape=(jax.ShapeDtypeStruct((t, RET_WIDTH), BF16),
                   jax.ShapeDtypeStruct((n_seq, RET_HEADS, HEAD_DIM, HEAD_DIM), F32)),
        scratch_shapes=scratch,
        compiler_params=_params(*sem),
        name="ret_chain" if carry else "ret_step",
    )(*args)


def _outproj_kernel(x_ref, a_ref, b_ref, wa_ref, wb_ref, o_ref):
    o_ref[...] = (x_ref[...]
                  + jnp.dot(a_ref[...], wa_ref[...], preferred_element_type=F32)
                  + jnp.dot(b_ref[...], wb_ref[...], preferred_element_type=F32))


def _outproj(x2d, a, b, w, tm=512, tn=1024):
    t = x2d.shape[0]
    return pl.pallas_call(
        _outproj_kernel,
        grid=(t // tm, D_MODEL // tn),
        in_specs=[
            pl.BlockSpec((tm, tn), lambda i, j: (i, j)),
            pl.BlockSpec((tm, S5_WIDTH), lambda i, j: (i, 0)),
            pl.BlockSpec((tm, RET_WIDTH), lambda i, j: (i, 0)),
            pl.BlockSpec((S5_WIDTH, tn), lambda i, j: (0, j)),
            pl.BlockSpec((RET_WIDTH, tn), lambda i, j: (1, j)),
        ],
        out_specs=pl.BlockSpec((tm, tn), lambda i, j: (i, j)),
        out_shape=jax.ShapeDtypeStruct((t, D_MODEL), F32),
        compiler_params=_params("arbitrary", "arbitrary"),
        name="out_proj",
    )(x2d, a, b, w, w)


def _ffn_kernel(x_ref, gn_ref, wg_ref, wu_ref, wo_ref, gf_ref, o_ref, h_scr, acc_scr):
    j = pl.program_id(1)

    @pl.when(j == 0)
    def _():
        h_scr[...] = _rms(x_ref[...], gn_ref[...]).astype(BF16)
        acc_scr[...] = jnp.zeros_like(acc_scr)

    h = h_scr[...]
    gate = jnp.dot(h, wg_ref[...], preferred_element_type=F32)
    up = jnp.dot(h, wu_ref[...], preferred_element_type=F32)
    act = (jax.nn.silu(gate) * up).astype(BF16)
    acc_scr[...] += jnp.dot(act, wo_ref[...], preferred_element_type=F32)

    @pl.when(j == pl.num_programs(1) - 1)
    def _():
        o_ref[...] = _rms(x_ref[...] + acc_scr[...], gf_ref[...])


def _ffn(x2d, g_ffn, w_in, w_out, g_final, tm=512, tf=512):
    t = x2d.shape[0]
    nf = D_FF // tf
    return pl.pallas_call(
        _ffn_kernel,
        grid=(t // tm, nf),
        in_specs=[
            pl.BlockSpec((tm, D_MODEL), lambda i, j: (i, 0)),
            pl.BlockSpec((1, D_MODEL), lambda i, j: (0, 0)),
            pl.BlockSpec((D_MODEL, tf), lambda i, j: (0, j)),
            pl.BlockSpec((D_MODEL, tf), lambda i, j: (0, j + nf)),
            pl.BlockSpec((tf, D_MODEL), lambda i, j: (j, 0)),
            pl.BlockSpec((1, D_MODEL), lambda i, j: (0, 0)),
        ],
        out_specs=pl.BlockSpec((tm, D_MODEL), lambda i, j: (i, 0)),
        out_shape=jax.ShapeDtypeStruct((t, D_MODEL), F32),
        scratch_shapes=[pltpu.VMEM((tm, D_MODEL), BF16), pltpu.VMEM((tm, D_MODEL), F32)],
        compiler_params=_params("arbitrary", "arbitrary"),
        name="ffn",
    )(x2d, g_ffn, w_in, w_in, w_out, g_final)


def _layer(x, s5_state, ret_state, pos0, w):
    n, l, _ = x.shape
    x2d = x.reshape(n * l, D_MODEL)
    proj = _inproj(x2d, w["norm_mix"], w["w_in"])
    s5_out, s5_re, s5_im = _s5_mixer(proj, s5_state, w["tabs"], w["wb"], w["cd"], w["d"],
                                     w["w_glu"], w["b_glu"], n, l)
    ret_out, ret_new = _retention(proj, ret_state, w["gn_w"], n, l, pos0)
    x1 = _outproj(x2d, s5_out, ret_out, w["w_out"])
    y = _ffn(x1, w["norm_ffn"], w["w_ffn_in"], w["w_ffn_out"], w["norm_final"])
    return (y.reshape(n, l, D_MODEL),
            s5_re.reshape(n, S5_GROUPS, S5_STATE), s5_im.reshape(n, S5_GROUPS, S5_STATE), ret_new)


def kernel(x_prompt, x_sample, state_s5_re, state_s5_im, state_ret, norm_mix, w_in, s5_lambda_re, s5_lambda_im, s5_log_step, s5_b_re, s5_b_im, s5_c_re, s5_c_im, s5_d, s5_w_glu, s5_b_glu, ret_gn_w, w_out, norm_ffn, w_ffn_in, w_ffn_out, norm_final):
    assert norm_mix.shape[0] == 1, "single-layer stack"
    tabr, tabi, bbr, bbi = _s5_params(s5_lambda_re[0], s5_lambda_im[0], s5_log_step[0],
                                      s5_b_re[0], s5_b_im[0])
    w = dict(
        norm_mix=norm_mix, w_in=w_in[0].astype(BF16),
        tabs=(tabr, tabi),
        wb=(_block_diag_in(bbr), _block_diag_in(bbi)),
        cd=(_block_diag_out(s5_c_re[0]), _block_diag_out(s5_c_im[0])),
        d=s5_d, w_glu=s5_w_glu[0].astype(BF16), b_glu=s5_b_glu,
        gn_w=ret_gn_w, w_out=w_out[0].astype(BF16), norm_ffn=norm_ffn,
        w_ffn_in=w_ffn_in[0].astype(BF16), w_ffn_out=w_ffn_out[0].astype(BF16),
        norm_final=norm_final.reshape(1, D_MODEL),
    )
    n_s = x_sample.shape[0]
    yp, p_re, p_im, p_ret = _layer(x_prompt, None, None, 0.0, w)
    ys, s_re, s_im, s_ret = _layer(
        x_sample,
        (state_s5_re[0].reshape(n_s, N_STATE), state_s5_im[0].reshape(n_s, N_STATE)),
        state_ret[0], float(PAST_LEN), w)
    return (yp, ys, p_re[None], p_im[None], p_ret[None], s_re[None], s_im[None], s_ret[None])
```

```python
import functools
import math

import jax
import jax.numpy as jnp
from jax import lax
from jax.experimental import pallas as pl
from jax.experimental.pallas import tpu as pltpu

F32 = jnp.float32
BF16 = jnp.bfloat16

D_MODEL = 2048
S5_WIDTH = 1024
S5_GROUP = 16
S5_GROUPS = 64
S5_STATE = 64
N_STATE = S5_GROUPS * S5_STATE
RET_WIDTH = 1024
RET_HEADS = 8
HEAD_DIM = 128
ROPE_BASE = 10000.0
D_FF = 5632
IN_WIDTH = S5_WIDTH + 4 * RET_WIDTH
NORM_EPS = 1e-6
PAST_LEN = 16384

SUBLANES = 8
SLAB_GROUPS = 8
N_SLABS = S5_GROUPS // SLAB_GROUPS
SLAB_IN = SLAB_GROUPS * S5_GROUP
SLAB_STATE = SLAB_GROUPS * S5_STATE
RET_TILE = 128
FFN_ROWS = 256

VMEM_LIMIT_BYTES = 56 * 1024 * 1024


def _params(*sem):
    return pltpu.CompilerParams(dimension_semantics=sem, vmem_limit_bytes=VMEM_LIMIT_BYTES)


def _rms(xf, g):
    ms = jnp.mean(xf * xf, axis=-1, keepdims=True)
    return xf * lax.rsqrt(ms + NORM_EPS) * g


def _inproj_kernel(x_ref, g_ref, w_ref, o_ref, h_scr, *, row_split, tn):
    tm = x_ref.shape[0]
    rs = tm // row_split
    for r in range(row_split):
        rows = slice(r * rs, (r + 1) * rs)
        h_scr[rows, :] = _rms(x_ref[rows, :], g_ref[...]).astype(BF16)
        for j in range(IN_WIDTH // tn):
            cols = slice(j * tn, (j + 1) * tn)
            o_ref[rows, cols] = jnp.dot(h_scr[rows, :], w_ref[:, cols],
                                        preferred_element_type=F32).astype(o_ref.dtype)


def _inproj(x2d, g, w, tm=512, tn=1024, row_split=2):
    t = x2d.shape[0]
    return pl.pallas_call(
        functools.partial(_inproj_kernel, row_split=row_split, tn=tn),
        grid=(t // tm,),
        in_specs=[
            pl.BlockSpec((tm, D_MODEL), lambda i: (i, 0)),
            pl.BlockSpec((1, D_MODEL), lambda i: (0, 0)),
            pl.BlockSpec((D_MODEL, IN_WIDTH), lambda i: (0, 0), pipeline_mode=pl.Buffered(1)),
        ],
        out_specs=pl.BlockSpec((tm, IN_WIDTH), lambda i: (i, 0)),
        out_shape=jax.ShapeDtypeStruct((t, IN_WIDTH), BF16),
        scratch_shapes=[pltpu.VMEM((tm, D_MODEL), BF16)],
        compiler_params=_params("arbitrary"),
        name="in_proj",
    )(x2d, g, w)


def _s5_param_kernel(lr_ref, li_ref, ls_ref, br_ref, bi_ref, tabr_ref, tabi_ref, bbr_ref, bbi_ref):
    lr = lr_ref[...]
    li = li_ref[...]
    dt = jnp.exp(ls_ref[...])
    mag = jnp.exp(lr * dt)
    ar = mag * jnp.cos(li * dt)
    ai = mag * jnp.sin(li * dt)
    den = lr * lr + li * li
    nr = ar - 1.0
    cr = (nr * lr + ai * li) / den
    ci = (ai * lr - nr * li) / den
    br = br_ref[...]
    bi = bi_ref[...]
    bbr_ref[...] = cr * br - ci * bi
    bbi_ref[...] = cr * bi + ci * br

    rows = lax.broadcasted_iota(jnp.int32, (SUBLANES, N_STATE), 0)
    pr, pi = ar, ai
    for r in range(SUBLANES):
        tabr_ref[3, r:r + 1, :] = pr
        tabi_ref[3, r:r + 1, :] = pi
        step = r + 1
        if step in (1, 2, 4):
            k = (1, 2, 4).index(step)
            keep = rows >= step
            tabr_ref[k] = jnp.where(keep, jnp.broadcast_to(pr, (SUBLANES, N_STATE)), 0.0)
            tabi_ref[k] = jnp.where(keep, jnp.broadcast_to(pi, (SUBLANES, N_STATE)), 0.0)
        pr, pi = pr * ar - pi * ai, pr * ai + pi * ar


def _s5_params(lam_re, lam_im, log_step, b_re, b_im):
    lr = lam_re.reshape(1, N_STATE)
    li = lam_im.reshape(1, N_STATE)
    ls = jnp.repeat(log_step, S5_STATE).reshape(1, N_STATE)
    br = jnp.transpose(b_re, (2, 0, 1)).reshape(S5_GROUP, N_STATE)
    bi = jnp.transpose(b_im, (2, 0, 1)).reshape(S5_GROUP, N_STATE)
    tab = jax.ShapeDtypeStruct((4, SUBLANES, N_STATE), F32)
    bb = jax.ShapeDtypeStruct((S5_GROUP, N_STATE), F32)
    return pl.pallas_call(
        _s5_param_kernel,
        out_shape=(tab, tab, bb, bb),
        name="s5_params",
    )(lr, li, ls, br, bi)


def _block_diag_in(bb):
    blk = bb.reshape(S5_GROUP, N_SLABS, SLAB_GROUPS, S5_STATE)
    eye = jnp.eye(SLAB_GROUPS, dtype=F32)
    w = jnp.einsum("hsgp,gk->sghkp", blk, eye)
    return w.reshape(N_SLABS, SLAB_IN, SLAB_STATE).astype(BF16)


def _block_diag_out(c):
    blk = c.reshape(N_SLABS, SLAB_GROUPS, S5_GROUP, S5_STATE)
    eye = jnp.eye(SLAB_GROUPS, dtype=F32)
    w = jnp.einsum("sghp,gk->sgpkh", blk, eye)
    return w.reshape(N_SLABS, SLAB_STATE, SLAB_IN).astype(BF16)


def _s5_kernel(*refs, tc, chain):
    if chain:
        (u_ref, tabr_ref, tabi_ref, wbr_ref, wbi_ref, cdr_ref, cdi_ref, d_ref, wg_ref, bg_ref,
         o_ref, xlr_ref, xli_ref, xr_s, xi_s, y_s, cbr_s, cbi_s) = refs
    else:
        (u_ref, x0r_ref, x0i_ref, tabr_ref, tabi_ref, wbr_ref, wbi_ref, cdr_ref, cdi_ref, d_ref,
         wg_ref, bg_ref, o_ref, xlr_ref, xli_ref, xr_s, xi_s, y_s) = refs

    if chain:
        @pl.when(pl.program_id(1) == 0)
        def _():
            cbr_s[...] = jnp.zeros_like(cbr_s)
            cbi_s[...] = jnp.zeros_like(cbi_s)

    for s in range(N_SLABS):
        ub = u_ref[:, s * SLAB_IN:(s + 1) * SLAB_IN].astype(BF16)
        cs = slice(s * SLAB_STATE, (s + 1) * SLAB_STATE)
        xr_s[:, cs] = jnp.dot(ub, wbr_ref[s], preferred_element_type=F32)
        xi_s[:, cs] = jnp.dot(ub, wbi_ref[s], preferred_element_type=F32)

    def block(b, carry):
        r0 = pl.multiple_of(b * SUBLANES, SUBLANES)
        for s in range(N_SLABS):
            cs = slice(s * SLAB_STATE, (s + 1) * SLAB_STATE)
            xr = xr_s[pl.ds(r0, SUBLANES), cs]
            xi = xi_s[pl.ds(r0, SUBLANES), cs]
            for k, sh in enumerate((1, 2, 4)):
                pr = tabr_ref[k, :, cs]
                pi = tabi_ref[k, :, cs]
                sr = pltpu.roll(xr, sh, 0)
                si = pltpu.roll(xi, sh, 0)
                xr, xi = xr + (pr * sr - pi * si), xi + (pr * si + pi * sr)
            if chain:
                cr = cbr_s[:, cs]
                ci = cbi_s[:, cs]
            else:
                cr = jnp.broadcast_to(x0r_ref[pl.ds(b, 1), cs], (SUBLANES, SLAB_STATE))
                ci = jnp.broadcast_to(x0i_ref[pl.ds(b, 1), cs], (SUBLANES, SLAB_STATE))
            pr = tabr_ref[3, :, cs]
            pi = tabi_ref[3, :, cs]
            xr = xr + (pr * cr - pi * ci)
            xi = xi + (pr * ci + pi * cr)
            xr_s[pl.ds(r0, SUBLANES), cs] = xr
            xi_s[pl.ds(r0, SUBLANES), cs] = xi
            last_r = xr[SUBLANES - 1:SUBLANES, :]
            last_i = xi[SUBLANES - 1:SUBLANES, :]
            if chain:
                cbr_s[:, cs] = jnp.broadcast_to(last_r, (SUBLANES, SLAB_STATE))
                cbi_s[:, cs] = jnp.broadcast_to(last_i, (SUBLANES, SLAB_STATE))
            else:
                xlr_ref[pl.ds(b, 1), cs] = last_r
                xli_ref[pl.ds(b, 1), cs] = last_i
        return carry

    lax.fori_loop(0, tc // SUBLANES, block, 0)

    if chain:
        xlr_ref[0] = cbr_s[...]
        xli_ref[0] = cbi_s[...]

    for s in range(N_SLABS):
        cs = slice(s * SLAB_STATE, (s + 1) * SLAB_STATE)
        us = slice(s * SLAB_IN, (s + 1) * SLAB_IN)
        y = (jnp.dot(xr_s[:, cs].astype(BF16), cdr_ref[s], preferred_element_type=F32)
             - jnp.dot(xi_s[:, cs].astype(BF16), cdi_ref[s], preferred_element_type=F32))
        y = y + d_ref[:, us] * u_ref[:, us].astype(F32)
        y_s[:, us] = jax.nn.gelu(y)

    y = y_s[...]
    z = jnp.dot(y.astype(BF16), wg_ref[...], preferred_element_type=F32) + bg_ref[...]
    o_ref[...] = (y * jax.nn.sigmoid(z)).astype(o_ref.dtype)


def _s5_mixer(proj, x0, tabs, wb, cd, d, w_glu, b_glu, n_seq, seq_len, tc=256):
    t = proj.shape[0]
    chain = x0 is None
    tabr, tabi = tabs
    const3 = lambda *_: (0, 0, 0)
    const2 = lambda *_: (0, 0)
    weight_specs = [
        pl.BlockSpec((4, SUBLANES, N_STATE), const3),
        pl.BlockSpec((4, SUBLANES, N_STATE), const3),
        pl.BlockSpec((N_SLABS, SLAB_IN, SLAB_STATE), const3),
        pl.BlockSpec((N_SLABS, SLAB_IN, SLAB_STATE), const3),
        pl.BlockSpec((N_SLABS, SLAB_STATE, SLAB_IN), const3),
        pl.BlockSpec((N_SLABS, SLAB_STATE, SLAB_IN), const3),
        pl.BlockSpec((1, S5_WIDTH), const2),
        pl.BlockSpec((S5_WIDTH, S5_WIDTH), const2),
        pl.BlockSpec((1, S5_WIDTH), const2),
    ]
    weights = (tabr, tabi, wb[0], wb[1], cd[0], cd[1], d, w_glu, b_glu)
    scratch = [pltpu.VMEM((tc, N_STATE), F32), pltpu.VMEM((tc, N_STATE), F32),
               pltpu.VMEM((tc, S5_WIDTH), F32)]
    out_y = jax.ShapeDtypeStruct((t, S5_WIDTH), BF16)
    if chain:
        tiles = seq_len // tc
        grid = (n_seq, tiles)
        in_specs = [pl.BlockSpec((tc, S5_WIDTH), lambda n, i: (n * tiles + i, 0))] + weight_specs
        out_specs = (pl.BlockSpec((tc, S5_WIDTH), lambda n, i: (n * tiles + i, 0)),
                     pl.BlockSpec((1, SUBLANES, N_STATE), lambda n, i: (n, 0, 0)),
                     pl.BlockSpec((1, SUBLANES, N_STATE), lambda n, i: (n, 0, 0)))
        st = jax.ShapeDtypeStruct((n_seq, SUBLANES, N_STATE), F32)
        scratch += [pltpu.VMEM((SUBLANES, N_STATE), F32), pltpu.VMEM((SUBLANES, N_STATE), F32)]
        args = (proj,) + weights
        sem = ("arbitrary", "arbitrary")
    else:
        assert seq_len == SUBLANES
        seqs = tc // SUBLANES
        grid = (t // tc,)
        in_specs = [pl.BlockSpec((tc, S5_WIDTH), lambda i: (i, 0)),
                    pl.BlockSpec((seqs, N_STATE), lambda i: (i, 0)),
                    pl.BlockSpec((seqs, N_STATE), lambda i: (i, 0))] + weight_specs
        out_specs = (pl.BlockSpec((tc, S5_WIDTH), lambda i: (i, 0)),
                     pl.BlockSpec((seqs, N_STATE), lambda i: (i, 0)),
                     pl.BlockSpec((seqs, N_STATE), lambda i: (i, 0)))
        st = jax.ShapeDtypeStruct((n_seq, N_STATE), F32)
        args = (proj, x0[0], x0[1]) + weights
        sem = ("arbitrary",)
    y, xlr, xli = pl.pallas_call(
        functools.partial(_s5_kernel, tc=tc, chain=chain),
        grid=grid,
        in_specs=in_specs,
        out_specs=out_specs,
        out_shape=(out_y, st, st),
        scratch_shapes=scratch,
        compiler_params=_params(*sem),
        name="s5_chain" if chain else "s5_step",
    )(*args)
    if chain:
        xlr, xli = xlr[:, 0], xli[:, 0]
    return y, xlr, xli


def _ret_kernel(*refs, n_seq, carry):
    if carry:
        (q_ref, k_ref, v_ref, g_ref, cos_ref, sin_ref, mask_ref, qd_ref, kd_ref, cd_ref, gnw_ref,
         o_ref, rn_ref, qr_s, kdt_s, ob_s, r_s) = refs
    else:
        (q_ref, k_ref, v_ref, g_ref, cos_ref, sin_ref, mask_ref, qd_ref, kd_ref, cd_ref, gnw_ref,
         r0_ref, o_ref, rn_ref, qr_s, kdt_s, ob_s) = refs
    rows_per_seq = RET_TILE // n_seq

    if carry:
        @pl.when(pl.program_id(1) == 0)
        def _():
            r_s[...] = jnp.zeros_like(r_s)

    cos = cos_ref[...]
    sin = sin_ref[...]
    scale = HEAD_DIM ** -0.5
    nt = (((1,), (1,)), ((), ()))
    for h in range(RET_HEADS):
        hs = slice(h * HEAD_DIM, (h + 1) * HEAD_DIM)
        qh = q_ref[:, hs].astype(F32)
        kh = k_ref[:, hs].astype(F32)
        qr = qh * cos + pltpu.roll(qh, HEAD_DIM // 2, 1) * sin
        kr = (kh * cos + pltpu.roll(kh, HEAD_DIM // 2, 1) * sin) * scale
        qb = qr.astype(BF16)
        sc = lax.dot_general(qb, kr.astype(BF16), nt, preferred_element_type=F32) * mask_ref[h]
        ob_s[:, hs] = jnp.dot(sc.astype(BF16), v_ref[:, hs].astype(BF16), preferred_element_type=F32)
        qr_s[:, hs] = qr
        kdt_s[h] = (kr * kd_ref[h]).T

    lane = lax.broadcasted_iota(jnp.int32, (HEAD_DIM, RET_TILE), 1)

    def per_seq(s, c):
        r0 = pl.multiple_of(s * rows_per_seq, rows_per_seq)
        rows = pl.ds(r0, rows_per_seq)
        in_seq = (lane >= r0) & (lane < r0 + rows_per_seq)
        for h in range(RET_HEADS):
            hs = slice(h * HEAD_DIM, (h + 1) * HEAD_DIM)
            r_old = r_s[h] if carry else r0_ref[s, h]
            cross = jnp.dot(qr_s[rows, hs].astype(BF16), r_old.astype(BF16), preferred_element_type=F32)
            ob_s[rows, hs] = ob_s[rows, hs] + cross * qd_ref[h, rows, :]
            kdt = jnp.where(in_seq, kdt_s[h], 0.0).astype(BF16)
            r_new = r_old * cd_ref[h] + jnp.dot(kdt, v_ref[:, hs].astype(BF16), preferred_element_type=F32)
            if carry:
                r_s[h] = r_new
                rn_ref[0, h] = r_new
            else:
                rn_ref[s, h] = r_new
        return c

    lax.fori_loop(0, n_seq, per_seq, 0)

    for h in range(RET_HEADS):
        hs = slice(h * HEAD_DIM, (h + 1) * HEAD_DIM)
        o = ob_s[:, hs]
        mu = jnp.mean(o, axis=-1, keepdims=True)
        oc = o - mu
        var = jnp.mean(oc * oc, axis=-1, keepdims=True)
        on = oc * lax.rsqrt(var + NORM_EPS) * gnw_ref[:, hs]
        o_ref[:, hs] = (jax.nn.silu(g_ref[:, hs].astype(F32)) * on).astype(o_ref.dtype)


def _rotary_tables(pos0, rows, reps):
    half = HEAD_DIM // 2
    inv_freq = ROPE_BASE ** (-jnp.arange(half, dtype=F32) / half)
    pos = pos0 + jnp.arange(rows, dtype=F32)
    ang = pos[:, None] * inv_freq[None, :]
    cos, sin = jnp.cos(ang), jnp.sin(ang)
    cos2 = jnp.concatenate([cos, cos], axis=-1)
    sin2 = jnp.concatenate([-sin, sin], axis=-1)
    return jnp.tile(cos2, (reps, 1)), jnp.tile(sin2, (reps, 1))


def _decay_tables(chunk, n_seq):
    log_gamma = jnp.log(1.0 - 2.0 ** (-5.0 - jnp.arange(RET_HEADS, dtype=F32)))
    idx = jnp.arange(chunk, dtype=F32)
    diff = idx[:, None] - idx[None, :]
    mask = jnp.where(diff >= 0, jnp.exp(log_gamma[:, None, None] * jnp.maximum(diff, 0.0)), 0.0)
    q_decay = jnp.exp(log_gamma[:, None] * (idx + 1.0))
    k_decay = jnp.exp(log_gamma[:, None] * (chunk - 1.0 - idx))
    chunk_decay = jnp.exp(log_gamma * chunk)
    eye = jnp.eye(n_seq, dtype=F32)
    mask_t = jnp.einsum("hab,st->hsatb", mask, eye).reshape(RET_HEADS, RET_TILE, RET_TILE)
    qd_t = jnp.broadcast_to(jnp.tile(q_decay, (1, n_seq))[:, :, None], (RET_HEADS, RET_TILE, HEAD_DIM))
    kd_t = jnp.broadcast_to(jnp.tile(k_decay, (1, n_seq))[:, :, None], (RET_HEADS, RET_TILE, HEAD_DIM))
    return mask_t, qd_t, kd_t, chunk_decay


def _retention(proj, r0, gn_w, n_seq, seq_len, pos0):
    t = proj.shape[0]
    carry = r0 is None
    if carry:
        chunks = seq_len // RET_TILE
        tile_seqs = 1
        cos2, sin2 = _rotary_tables(pos0, seq_len, 1)
        grid = (n_seq, chunks)
        row = lambda n, c: n * chunks + c
        tab_map = lambda n, c: (c, 0)
        state_map = lambda n, c: (n, 0, 0, 0)
        state_block = (1, RET_HEADS, HEAD_DIM, HEAD_DIM)
        sem = ("arbitrary", "arbitrary")
    else:
        tile_seqs = RET_TILE // seq_len
        cos2, sin2 = _rotary_tables(pos0, seq_len, tile_seqs)
        grid = (t // RET_TILE,)
        row = lambda i: i
        tab_map = lambda i: (0, 0)
        state_map = lambda i: (i, 0, 0, 0)
        state_block = (tile_seqs, RET_HEADS, HEAD_DIM, HEAD_DIM)
        sem = ("arbitrary",)
    mask_t, qd_t, kd_t, cd = _decay_tables(RET_TILE // tile_seqs, tile_seqs)

    def col(cb):
        return pl.BlockSpec((RET_TILE, RET_WIDTH), lambda *a: (row(*a), cb))

    const3 = lambda *_: (0, 0, 0)
    in_specs = [
        col(1), col(2), col(3), col(4),
        pl.BlockSpec((RET_TILE, HEAD_DIM), tab_map),
        pl.BlockSpec((RET_TILE, HEAD_DIM), tab_map),
        pl.BlockSpec((RET_HEADS, RET_TILE, RET_TILE), const3),
        pl.BlockSpec((RET_HEADS, RET_TILE, HEAD_DIM), const3),
        pl.BlockSpec((RET_HEADS, RET_TILE, HEAD_DIM), const3),
        pl.BlockSpec(memory_space=pltpu.SMEM),
        pl.BlockSpec((1, RET_WIDTH), lambda *_: (0, 0)),
    ]
    args = [proj, proj, proj, proj, cos2, sin2, mask_t, qd_t, kd_t, cd, gn_w]
    scratch = [pltpu.VMEM((RET_TILE, RET_WIDTH), F32),
               pltpu.VMEM((RET_HEADS, HEAD_DIM, RET_TILE), F32),
               pltpu.VMEM((RET_TILE, RET_WIDTH), F32)]
    if carry:
        scratch.append(pltpu.VMEM((RET_HEADS, HEAD_DIM, HEAD_DIM), F32))
    else:
        in_specs.append(pl.BlockSpec(state_block, state_map))
        args.append(r0)
    return pl.pallas_call(
        functools.partial(_ret_kernel, n_seq=tile_seqs, carry=carry),
        grid=grid,
        in_specs=in_specs,
        out_specs=(pl.BlockSpec((RET_TILE, RET_WIDTH), lambda *a: (row(*a), 0)),
                   pl.BlockSpec(state_block, state_map)),
        out_shape=(jax.ShapeDtypeStruct((t, RET_WIDTH), BF16),
                   jax.ShapeDtypeStruct((n_seq, RET_HEADS, HEAD_DIM, HEAD_DIM), F32)),
        scratch_shapes=scratch,
        compiler_params=_params(*sem),
        name="ret_chain" if carry else "ret_step",
    )(*args)


def _outproj_kernel(x_ref, a_ref, b_ref, wa_ref, wb_ref, o_ref):
    o_ref[...] = (x_ref[...]
                  + jnp.dot(a_ref[...], wa_ref[...], preferred_element_type=F32)
                  + jnp.dot(b_ref[...], wb_ref[...], preferred_element_type=F32))


def _outproj(x2d, a, b, w, tm=512):
    t = x2d.shape[0]
    resident = pl.Buffered(1)
    return pl.pallas_call(
        _outproj_kernel,
        grid=(t // tm,),
        in_specs=[
            pl.BlockSpec((tm, D_MODEL), lambda i: (i, 0)),
            pl.BlockSpec((tm, S5_WIDTH), lambda i: (i, 0)),
            pl.BlockSpec((tm, RET_WIDTH), lambda i: (i, 0)),
            pl.BlockSpec((S5_WIDTH, D_MODEL), lambda i: (0, 0), pipeline_mode=resident),
            pl.BlockSpec((RET_WIDTH, D_MODEL), lambda i: (1, 0), pipeline_mode=resident),
        ],
        out_specs=pl.BlockSpec((tm, D_MODEL), lambda i: (i, 0)),
        out_shape=jax.ShapeDtypeStruct((t, D_MODEL), F32),
        compiler_params=_params("arbitrary"),
        name="out_proj",
    )(x2d, a, b, w, w)


def _ffn_kernel(x_ref, gn_ref, wg_ref, wu_ref, wo_ref, gf_ref, o_ref, h_scr):
    j = pl.program_id(1)

    @pl.when(j == 0)
    def _():
        h_scr[...] = _rms(x_ref[...], gn_ref[...]).astype(BF16)
        o_ref[...] = x_ref[...]

    def rows_block(r, c):
        rows = pl.ds(pl.multiple_of(r * FFN_ROWS, FFN_ROWS), FFN_ROWS)
        h = h_scr[rows, :]
        gate = jnp.dot(h, wg_ref[...], preferred_element_type=F32)
        up = jnp.dot(h, wu_ref[...], preferred_element_type=F32)
        act = (jax.nn.silu(gate) * up).astype(BF16)
        o_ref[rows, :] += jnp.dot(act, wo_ref[...], preferred_element_type=F32)
        return c

    lax.fori_loop(0, x_ref.shape[0] // FFN_ROWS, rows_block, 0)

    @pl.when(j == pl.num_programs(1) - 1)
    def _():
        o_ref[...] = _rms(o_ref[...], gf_ref[...])


def _ffn(x2d, g_ffn, w_in, w_out, g_final, tm=1024, tf=512):
    t = x2d.shape[0]
    nf = D_FF // tf
    return pl.pallas_call(
        _ffn_kernel,
        grid=(t // tm, nf),
        in_specs=[
            pl.BlockSpec((tm, D_MODEL), lambda i, j: (i, 0)),
            pl.BlockSpec((1, D_MODEL), lambda i, j: (0, 0)),
            pl.BlockSpec((D_MODEL, tf), lambda i, j: (0, j)),
            pl.BlockSpec((D_MODEL, tf), lambda i, j: (0, j + nf)),
            pl.BlockSpec((tf, D_MODEL), lambda i, j: (j, 0)),
            pl.BlockSpec((1, D_MODEL), lambda i, j: (0, 0)),
        ],
        out_specs=pl.BlockSpec((tm, D_MODEL), lambda i, j: (i, 0)),
        out_shape=jax.ShapeDtypeStruct((t, D_MODEL), F32),
        scratch_shapes=[pltpu.VMEM((tm, D_MODEL), BF16)],
        compiler_params=_params("arbitrary", "arbitrary"),
        name="ffn",
    )(x2d, g_ffn, w_in, w_in, w_out, g_final)


def _layer(x, s5_state, ret_state, pos0, w):
    n, l, _ = x.shape
    x2d = x.reshape(n * l, D_MODEL)
    proj = _inproj(x2d, w["norm_mix"], w["w_in"])
    s5_out, s5_re, s5_im = _s5_mixer(proj, s5_state, w["tabs"], w["wb"], w["cd"], w["d"],
                                     w["w_glu"], w["b_glu"], n, l)
    ret_out, ret_new = _retention(proj, ret_state, w["gn_w"], n, l, pos0)
    x1 = _outproj(x2d, s5_out, ret_out, w["w_out"])
    y = _ffn(x1, w["norm_ffn"], w["w_ffn_in"], w["w_ffn_out"], w["norm_final"])
    return (y.reshape(n, l, D_MODEL),
            s5_re.reshape(n, S5_GROUPS, S5_STATE), s5_im.reshape(n, S5_GROUPS, S5_STATE), ret_new)


def kernel(x_prompt, x_sample, state_s5_re, state_s5_im, state_ret, norm_mix, w_in, s5_lambda_re, s5_lambda_im, s5_log_step, s5_b_re, s5_b_im, s5_c_re, s5_c_im, s5_d, s5_w_glu, s5_b_glu, ret_gn_w, w_out, norm_ffn, w_ffn_in, w_ffn_out, norm_final):
    assert norm_mix.shape[0] == 1, "single-layer stack"
    tabr, tabi, bbr, bbi = _s5_params(s5_lambda_re[0], s5_lambda_im[0], s5_log_step[0],
                                      s5_b_re[0], s5_b_im[0])
    w = dict(
        norm_mix=norm_mix, w_in=w_in[0].astype(BF16),
        tabs=(tabr, tabi),
        wb=(_block_diag_in(bbr), _block_diag_in(bbi)),
        cd=(_block_diag_out(s5_c_re[0]), _block_diag_out(s5_c_im[0])),
        d=s5_d, w_glu=s5_w_glu[0].astype(BF16), b_glu=s5_b_glu,
        gn_w=ret_gn_w, w_out=w_out[0].astype(BF16), norm_ffn=norm_ffn,
        w_ffn_in=w_ffn_in[0].astype(BF16), w_ffn_out=w_ffn_out[0].astype(BF16),
        norm_final=norm_final.reshape(1, D_MODEL),
    )
    n_s = x_sample.shape[0]
    yp, p_re, p_im, p_ret = _layer(x_prompt, None, None, 0.0, w)
    ys, s_re, s_im, s_ret = _layer(
        x_sample,
        (state_s5_re[0].reshape(n_s, N_STATE), state_s5_im[0].reshape(n_s, N_STATE)),
        state_ret[0], float(PAST_LEN), w)
    return (yp, ys, p_re[None], p_im[None], p_ret[None], s_re[None], s_im[None], s_ret[None])
```

```python
import functools
import math

import jax
import jax.numpy as jnp
from jax import lax
from jax.experimental import pallas as pl
from jax.experimental.pallas import tpu as pltpu

F32 = jnp.float32
BF16 = jnp.bfloat16

D_MODEL = 2048
S5_WIDTH = 1024
S5_GROUP = 16
S5_GROUPS = 64
S5_STATE = 64
N_STATE = S5_GROUPS * S5_STATE
RET_WIDTH = 1024
RET_HEADS = 8
HEAD_DIM = 128
ROPE_BASE = 10000.0
D_FF = 5632
IN_WIDTH = S5_WIDTH + 4 * RET_WIDTH
NORM_EPS = 1e-6
PAST_LEN = 16384

SUBLANES = 8
SLAB_GROUPS = 8
N_SLABS = S5_GROUPS // SLAB_GROUPS
SLAB_IN = SLAB_GROUPS * S5_GROUP
SLAB_STATE = SLAB_GROUPS * S5_STATE
RET_TILE = 128
FFN_ROWS = 512

VMEM_LIMIT_BYTES = 60 * 1024 * 1024


def _params(*sem):
    return pltpu.CompilerParams(dimension_semantics=sem, vmem_limit_bytes=VMEM_LIMIT_BYTES)


def _rms(xf, g):
    ms = jnp.mean(xf * xf, axis=-1, keepdims=True)
    return xf * lax.rsqrt(ms + NORM_EPS) * g


def _inproj_kernel(x_ref, g_ref, w_ref, o_ref, *rest, row_split, tn, interleave):
    if interleave:
        ou_ref, h_scr = rest
        seq = pl.program_id(1)
    else:
        (h_scr,) = rest
    tm = x_ref.shape[0]
    rs = tm // row_split
    for r in range(row_split):
        rows = slice(r * rs, (r + 1) * rs)
        h_scr[rows, :] = _rms(x_ref[rows, :], g_ref[...]).astype(BF16)
        for j in range(IN_WIDTH // tn):
            cols = slice(j * tn, (j + 1) * tn)
            res = jnp.dot(h_scr[rows, :], w_ref[:, cols], preferred_element_type=F32)
            if interleave and j * tn < S5_WIDTH:
                dst = pl.ds(r * rs * interleave + seq, rs, stride=interleave)
                for s in range(tn // SLAB_IN):
                    ou_ref[j * (tn // SLAB_IN) + s, dst, :] = res[:, s * SLAB_IN:(s + 1) * SLAB_IN]
            o_ref[rows, cols] = res.astype(o_ref.dtype)


def _inproj(x2d, g, w, n_seq, interleave, tm, tn=1024, row_split=1):
    t = x2d.shape[0]
    resident = pl.Buffered(1)
    out_shape = [jax.ShapeDtypeStruct((t, IN_WIDTH), BF16)]
    if interleave:
        tiles = t // n_seq // tm
        grid = (tiles, n_seq)
        row = lambda i, n: (n * tiles + i, 0)
        const = lambda i, n: (0, 0)
        out_specs = [pl.BlockSpec((tm, IN_WIDTH), row),
                     pl.BlockSpec((N_SLABS, n_seq * tm, SLAB_IN), lambda i, n: (0, i, 0))]
        out_shape.append(jax.ShapeDtypeStruct((N_SLABS, t, SLAB_IN), F32))
        sem = ("arbitrary", "arbitrary")
    else:
        grid = (t // tm,)
        row = lambda i: (i, 0)
        const = lambda i: (0, 0)
        out_specs = [pl.BlockSpec((tm, IN_WIDTH), row)]
        sem = ("arbitrary",)
    return pl.pallas_call(
        functools.partial(_inproj_kernel, row_split=row_split, tn=tn,
                          interleave=n_seq if interleave else 0),
        grid=grid,
        in_specs=[
            pl.BlockSpec((tm, D_MODEL), row),
            pl.BlockSpec((1, D_MODEL), const),
            pl.BlockSpec((D_MODEL, IN_WIDTH), const, pipeline_mode=resident),
        ],
        out_specs=out_specs,
        out_shape=out_shape,
        scratch_shapes=[pltpu.VMEM((tm, D_MODEL), BF16)],
        compiler_params=_params(*sem),
        name="in_proj_seq" if interleave else "in_proj",
    )(x2d, g, w)


def _s5_param_kernel(lr_ref, li_ref, ls_ref, br_ref, bi_ref,
                     tabr_ref, tabi_ref, pairr_ref, pairi_ref, bbr_ref, bbi_ref, abr_ref, abi_ref):
    lr = lr_ref[...]
    li = li_ref[...]
    dt = jnp.exp(ls_ref[...])
    mag = jnp.exp(lr * dt)
    ar = mag * jnp.cos(li * dt)
    ai = mag * jnp.sin(li * dt)
    den = lr * lr + li * li
    nr = ar - 1.0
    cr = (nr * lr + ai * li) / den
    ci = (ai * lr - nr * li) / den
    br = br_ref[...]
    bi = bi_ref[...]
    bbr = cr * br - ci * bi
    bbi = cr * bi + ci * br
    bbr_ref[...] = bbr
    bbi_ref[...] = bbi
    abr_ref[...] = ar * bbr - ai * bbi
    abi_ref[...] = ar * bbi + ai * bbr

    rows = lax.broadcasted_iota(jnp.int32, (SUBLANES, N_STATE), 0)
    second = rows >= SUBLANES // 2
    full = lambda v: jnp.broadcast_to(v, (SUBLANES, N_STATE))
    pairr_ref[...] = jnp.where(second, full(ar * ar - ai * ai), full(ar))
    pairi_ref[...] = jnp.where(second, full(ar * ai + ai * ar), full(ai))

    pr, pi = ar, ai
    for r in range(SUBLANES):
        tabr_ref[3, r:r + 1, :] = pr
        tabi_ref[3, r:r + 1, :] = pi
        step = r + 1
        if step in (1, 2, 4):
            k = (1, 2, 4).index(step)
            keep = rows >= step
            tabr_ref[k] = jnp.where(keep, jnp.broadcast_to(pr, (SUBLANES, N_STATE)), 0.0)
            tabi_ref[k] = jnp.where(keep, jnp.broadcast_to(pi, (SUBLANES, N_STATE)), 0.0)
        pr, pi = pr * ar - pi * ai, pr * ai + pi * ar


def _s5_params(lam_re, lam_im, log_step, b_re, b_im):
    lr = lam_re.reshape(1, N_STATE)
    li = lam_im.reshape(1, N_STATE)
    ls = jnp.repeat(log_step, S5_STATE).reshape(1, N_STATE)
    br = jnp.transpose(b_re, (2, 0, 1)).reshape(S5_GROUP, N_STATE)
    bi = jnp.transpose(b_im, (2, 0, 1)).reshape(S5_GROUP, N_STATE)
    tab = jax.ShapeDtypeStruct((4, SUBLANES, N_STATE), F32)
    pair = jax.ShapeDtypeStruct((SUBLANES, N_STATE), F32)
    bb = jax.ShapeDtypeStruct((S5_GROUP, N_STATE), F32)
    return pl.pallas_call(
        _s5_param_kernel,
        out_shape=(tab, tab, pair, pair, bb, bb, bb, bb),
        name="s5_params",
    )(lr, li, ls, br, bi)


def _block_diag_in(bb):
    blk = bb.reshape(S5_GROUP, N_SLABS, SLAB_GROUPS, S5_STATE)
    eye = jnp.eye(SLAB_GROUPS, dtype=F32)
    w = jnp.einsum("hsgp,gk->sghkp", blk, eye)
    return w.reshape(N_SLABS, SLAB_IN, SLAB_STATE).astype(BF16)


def _block_diag_out(c):
    blk = c.reshape(N_SLABS, SLAB_GROUPS, S5_GROUP, S5_STATE)
    eye = jnp.eye(SLAB_GROUPS, dtype=F32)
    w = jnp.einsum("sghp,gk->sgpkh", blk, eye)
    return w.reshape(N_SLABS, SLAB_STATE, SLAB_IN).astype(BF16)


def _cmul_add(xr, xi, pr, pi, vr, vi):
    return xr + (pr * vr - pi * vi), xi + (pr * vi + pi * vr)


def _s5_weight_specs(tab_shape, k_in):
    const3 = lambda *_: (0, 0, 0)
    const2 = lambda *_: (0, 0)
    tab_map = const3 if len(tab_shape) == 3 else const2
    return [
        pl.BlockSpec(tab_shape, tab_map),
        pl.BlockSpec(tab_shape, tab_map),
        pl.BlockSpec((N_SLABS, k_in, SLAB_STATE), const3),
        pl.BlockSpec((N_SLABS, k_in, SLAB_STATE), const3),
        pl.BlockSpec((N_SLABS, SLAB_STATE, SLAB_IN), const3),
        pl.BlockSpec((N_SLABS, SLAB_STATE, SLAB_IN), const3),
        pl.BlockSpec((1, S5_WIDTH), const2),
        pl.BlockSpec((S5_WIDTH, S5_WIDTH), const2),
        pl.BlockSpec((1, S5_WIDTH), const2),
    ]


def _s5_step_kernel(u_ref, x0r_ref, x0i_ref, tabr_ref, tabi_ref, wbr_ref, wbi_ref, cdr_ref, cdi_ref,
                    d_ref, wg_ref, bg_ref, o_ref, xlr_ref, xli_ref, xr_s, xi_s, y_s):
    for s in range(N_SLABS):
        ub = u_ref[:, s * SLAB_IN:(s + 1) * SLAB_IN]
        cs = slice(s * SLAB_STATE, (s + 1) * SLAB_STATE)
        xr_s[:, cs] = jnp.dot(ub, wbr_ref[s], preferred_element_type=F32)
        xi_s[:, cs] = jnp.dot(ub, wbi_ref[s], preferred_element_type=F32)

    def block(b, carry):
        r0 = pl.multiple_of(b * SUBLANES, SUBLANES)
        for s in range(N_SLABS):
            cs = slice(s * SLAB_STATE, (s + 1) * SLAB_STATE)
            xr = xr_s[pl.ds(r0, SUBLANES), cs]
            xi = xi_s[pl.ds(r0, SUBLANES), cs]
            for k, sh in enumerate((1, 2, 4)):
                xr, xi = _cmul_add(xr, xi, tabr_ref[k, :, cs], tabi_ref[k, :, cs],
                                   pltpu.roll(xr, sh, 0), pltpu.roll(xi, sh, 0))
            cr = jnp.broadcast_to(x0r_ref[pl.ds(b, 1), cs], (SUBLANES, SLAB_STATE))
            ci = jnp.broadcast_to(x0i_ref[pl.ds(b, 1), cs], (SUBLANES, SLAB_STATE))
            xr, xi = _cmul_add(xr, xi, tabr_ref[3, :, cs], tabi_ref[3, :, cs], cr, ci)
            xr_s[pl.ds(r0, SUBLANES), cs] = xr
            xi_s[pl.ds(r0, SUBLANES), cs] = xi
            xlr_ref[pl.ds(b, 1), cs] = xr[SUBLANES - 1:SUBLANES, :]
            xli_ref[pl.ds(b, 1), cs] = xi[SUBLANES - 1:SUBLANES, :]
        return carry

    lax.fori_loop(0, u_ref.shape[0] // SUBLANES, block, 0)

    for s in range(N_SLABS):
        cs = slice(s * SLAB_STATE, (s + 1) * SLAB_STATE)
        us = slice(s * SLAB_IN, (s + 1) * SLAB_IN)
        y = (jnp.dot(xr_s[:, cs].astype(BF16), cdr_ref[s], preferred_element_type=F32)
             - jnp.dot(xi_s[:, cs].astype(BF16), cdi_ref[s], preferred_element_type=F32))
        y = y + d_ref[:, us] * u_ref[:, us].astype(F32)
        y_s[:, us] = jax.nn.gelu(y)

    y = y_s[...]
    z = jnp.dot(y.astype(BF16), wg_ref[...], preferred_element_type=F32) + bg_ref[...]
    o_ref[...] = (y * jax.nn.sigmoid(z)).astype(o_ref.dtype)


def _s5_step(proj, x0, tabs, wb, cd, d, w_glu, b_glu, tc=256):
    t = proj.shape[0]
    seqs = tc // SUBLANES
    row = lambda i: (i, 0)
    st = jax.ShapeDtypeStruct((t // SUBLANES, N_STATE), F32)
    return pl.pallas_call(
        _s5_step_kernel,
        grid=(t // tc,),
        in_specs=[pl.BlockSpec((tc, S5_WIDTH), row), pl.BlockSpec((seqs, N_STATE), row),
                  pl.BlockSpec((seqs, N_STATE), row)]
        + _s5_weight_specs((4, SUBLANES, N_STATE), SLAB_IN),
        out_specs=(pl.BlockSpec((tc, S5_WIDTH), row), pl.BlockSpec((seqs, N_STATE), row),
                   pl.BlockSpec((seqs, N_STATE), row)),
        out_shape=(jax.ShapeDtypeStruct((t, S5_WIDTH), BF16), st, st),
        scratch_shapes=[pltpu.VMEM((tc, N_STATE), F32), pltpu.VMEM((tc, N_STATE), F32),
                        pltpu.VMEM((tc, S5_WIDTH), F32)],
        compiler_params=_params("arbitrary"),
        name="s5_step",
    )(proj, x0[0], x0[1], tabs[0], tabs[1], wb[0], wb[1], cd[0], cd[1], d, w_glu, b_glu)


def _s5_seq_kernel(u_ref, pairr_ref, pairi_ref, wbr_ref, wbi_ref, cdr_ref, cdi_ref, d_ref, wg_ref,
                   bg_ref, o_ref, xlr_ref, xli_ref, xr_s, xi_s, y_s, o_s, cbr_s, cbi_s):
    n_rows = u_ref.shape[1]
    half = SUBLANES // 2
    tt = n_rows // half

    @pl.when(pl.program_id(0) == 0)
    def _():
        cbr_s[...] = jnp.zeros_like(cbr_s)
        cbi_s[...] = jnp.zeros_like(cbi_s)

    first = lax.broadcasted_iota(jnp.int32, (SUBLANES, SLAB_STATE), 0) < half
    second_step = (lax.broadcasted_iota(jnp.int32, (n_rows, SLAB_IN), 0) & half) != 0
    for s in range(N_SLABS):
        cs = slice(s * SLAB_STATE, (s + 1) * SLAB_STATE)
        us = slice(s * SLAB_IN, (s + 1) * SLAB_IN)
        u = u_ref[s]
        u_prev = jnp.where(second_step, pltpu.roll(u, half, 0), 0.0)
        ub = jnp.concatenate([u, u_prev], axis=1).astype(BF16)
        xr_s[:, cs] = jnp.dot(ub, wbr_ref[s], preferred_element_type=F32)
        xi_s[:, cs] = jnp.dot(ub, wbi_ref[s], preferred_element_type=F32)

        pr, pi = pairr_ref[:, cs], pairi_ref[:, cs]
        cr, ci = cbr_s[:, cs], cbi_s[:, cs]
        for b in range(n_rows // SUBLANES):
            rows = slice(b * SUBLANES, (b + 1) * SUBLANES)
            xr, xi = _cmul_add(xr_s[rows, cs], xi_s[rows, cs], pr, pi, cr, ci)
            xr_s[rows, cs] = xr
            xi_s[rows, cs] = xi
            cr = jnp.where(first, pltpu.roll(xr, half, 0), xr)
            ci = jnp.where(first, pltpu.roll(xi, half, 0), xi)
        cbr_s[:, cs] = cr
        cbi_s[:, cs] = ci

        y = (jnp.dot(xr_s[:, cs].astype(BF16), cdr_ref[s], preferred_element_type=F32)
             - jnp.dot(xi_s[:, cs].astype(BF16), cdi_ref[s], preferred_element_type=F32))
        y_s[:, us] = jax.nn.gelu(y + d_ref[:, us] * u)

    xlr_ref[...] = cbr_s[...]
    xli_ref[...] = cbi_s[...]

    y = y_s[...]
    z = jnp.dot(y.astype(BF16), wg_ref[...], preferred_element_type=F32) + bg_ref[...]
    o = y * jax.nn.sigmoid(z)
    for s in range(N_SLABS):
        o_s[s] = o[:, s * SLAB_IN:(s + 1) * SLAB_IN]
    for n in range(half):
        for s in range(N_SLABS):
            o_ref[n, :, s * SLAB_IN:(s + 1) * SLAB_IN] = (
                o_s[s, pl.ds(n, tt, stride=half), :].astype(o_ref.dtype))


def _s5_seq(u_t, pairs, wb, cd, d, w_glu, b_glu, n_seq, seq_len, tt=64):
    assert 2 * n_seq == SUBLANES
    n_rows = n_seq * tt
    st = jax.ShapeDtypeStruct((SUBLANES, N_STATE), F32)
    y, xlr, xli = pl.pallas_call(
        _s5_seq_kernel,
        grid=(seq_len // tt,),
        in_specs=[pl.BlockSpec((N_SLABS, n_rows, SLAB_IN), lambda i: (0, i, 0))]
        + _s5_weight_specs((SUBLANES, N_STATE), 2 * SLAB_IN),
        out_specs=(pl.BlockSpec((n_seq, tt, S5_WIDTH), lambda i: (0, i, 0)),
                   pl.BlockSpec((SUBLANES, N_STATE), lambda i: (0, 0)),
                   pl.BlockSpec((SUBLANES, N_STATE), lambda i: (0, 0))),
        out_shape=(jax.ShapeDtypeStruct((n_seq, seq_len, S5_WIDTH), BF16), st, st),
        scratch_shapes=[pltpu.VMEM((n_rows, N_STATE), F32), pltpu.VMEM((n_rows, N_STATE), F32),
                        pltpu.VMEM((n_rows, S5_WIDTH), F32),
                        pltpu.VMEM((N_SLABS, n_rows, SLAB_IN), F32),
                        pltpu.VMEM((SUBLANES, N_STATE), F32), pltpu.VMEM((SUBLANES, N_STATE), F32)],
        compiler_params=_params("arbitrary"),
        name="s5_seq",
    )(u_t, pairs[0], pairs[1], wb[0], wb[1], cd[0], cd[1], d, w_glu, b_glu)
    return y.reshape(n_seq * seq_len, S5_WIDTH), xlr[:n_seq], xli[:n_seq]


def _ret_kernel(*refs, n_seq, carry):
    if carry:
        (q_ref, k_ref, v_ref, g_ref, cos_ref, sin_ref, mask_ref, qd_ref, kd_ref, cd_ref, gnw_ref,
         o_ref, rn_ref, qr_s, kdt_s, ob_s, r_s) = refs
    else:
        (q_ref, k_ref, v_ref, g_ref, cos_ref, sin_ref, mask_ref, qd_ref, kd_ref, cd_ref, gnw_ref,
         r0_ref, o_ref, rn_ref, qr_s, kdt_s, ob_s) = refs
    rows_per_seq = RET_TILE // n_seq

    if carry:
        @pl.when(pl.program_id(1) == 0)
        def _():
            r_s[...] = jnp.zeros_like(r_s)

    cos = cos_ref[...]
    sin = sin_ref[...]
    scale = HEAD_DIM ** -0.5
    nt = (((1,), (1,)), ((), ()))
    for h in range(RET_HEADS):
        hs = slice(h * HEAD_DIM, (h + 1) * HEAD_DIM)
        qh = q_ref[:, hs].astype(F32)
        kh = k_ref[:, hs].astype(F32)
        qr = qh * cos + pltpu.roll(qh, HEAD_DIM // 2, 1) * sin
        kr = (kh * cos + pltpu.roll(kh, HEAD_DIM // 2, 1) * sin) * scale
        qb = qr.astype(BF16)
        sc = lax.dot_general(qb, kr.astype(BF16), nt, preferred_element_type=F32) * mask_ref[h]
        ob_s[:, hs] = jnp.dot(sc.astype(BF16), v_ref[:, hs].astype(BF16), preferred_element_type=F32)
        qr_s[:, hs] = qr
        kdt_s[h] = (kr * kd_ref[h]).T

    lane = lax.broadcasted_iota(jnp.int32, (HEAD_DIM, RET_TILE), 1)

    def per_seq(s, c):
        r0 = pl.multiple_of(s * rows_per_seq, rows_per_seq)
        rows = pl.ds(r0, rows_per_seq)
        in_seq = (lane >= r0) & (lane < r0 + rows_per_seq)
        for h in range(RET_HEADS):
            hs = slice(h * HEAD_DIM, (h + 1) * HEAD_DIM)
            r_old = r_s[h] if carry else r0_ref[s, h]
            cross = jnp.dot(qr_s[rows, hs].astype(BF16), r_old.astype(BF16), preferred_element_type=F32)
            ob_s[rows, hs] = ob_s[rows, hs] + cross * qd_ref[h, rows, :]
            kdt = jnp.where(in_seq, kdt_s[h], 0.0).astype(BF16)
            r_new = r_old * cd_ref[h] + jnp.dot(kdt, v_ref[:, hs].astype(BF16), preferred_element_type=F32)
            if carry:
                r_s[h] = r_new
                rn_ref[0, h] = r_new
            else:
                rn_ref[s, h] = r_new
        return c

    lax.fori_loop(0, n_seq, per_seq, 0)

    for h in range(RET_HEADS):
        hs = slice(h * HEAD_DIM, (h + 1) * HEAD_DIM)
        o = ob_s[:, hs]
        mu = jnp.mean(o, axis=-1, keepdims=True)
        oc = o - mu
        var = jnp.mean(oc * oc, axis=-1, keepdims=True)
        on = oc * lax.rsqrt(var + NORM_EPS) * gnw_ref[:, hs]
        o_ref[:, hs] = (jax.nn.silu(g_ref[:, hs].astype(F32)) * on).astype(o_ref.dtype)


def _rotary_tables(pos0, rows, reps):
    half = HEAD_DIM // 2
    inv_freq = ROPE_BASE ** (-jnp.arange(half, dtype=F32) / half)
    pos = pos0 + jnp.arange(rows, dtype=F32)
    ang = pos[:, None] * inv_freq[None, :]
    cos, sin = jnp.cos(ang), jnp.sin(ang)
    cos2 = jnp.concatenate([cos, cos], axis=-1)
    sin2 = jnp.concatenate([-sin, sin], axis=-1)
    return jnp.tile(cos2, (reps, 1)), jnp.tile(sin2, (reps, 1))


def _decay_tables(chunk, n_seq):
    log_gamma = jnp.log(1.0 - 2.0 ** (-5.0 - jnp.arange(RET_HEADS, dtype=F32)))
    idx = jnp.arange(chunk, dtype=F32)
    diff = idx[:, None] - idx[None, :]
    mask = jnp.where(diff >= 0, jnp.exp(log_gamma[:, None, None] * jnp.maximum(diff, 0.0)), 0.0)
    q_decay = jnp.exp(log_gamma[:, None] * (idx + 1.0))
    k_decay = jnp.exp(log_gamma[:, None] * (chunk - 1.0 - idx))
    chunk_decay = jnp.exp(log_gamma * chunk)
    eye = jnp.eye(n_seq, dtype=F32)
    mask_t = jnp.einsum("hab,st->hsatb", mask, eye).reshape(RET_HEADS, RET_TILE, RET_TILE)
    qd_t = jnp.broadcast_to(jnp.tile(q_decay, (1, n_seq))[:, :, None], (RET_HEADS, RET_TILE, HEAD_DIM))
    kd_t = jnp.broadcast_to(jnp.tile(k_decay, (1, n_seq))[:, :, None], (RET_HEADS, RET_TILE, HEAD_DIM))
    return mask_t, qd_t, kd_t, chunk_decay


def _retention(proj, r0, gn_w, n_seq, seq_len, pos0):
    t = proj.shape[0]
    carry = r0 is None
    if carry:
        chunks = seq_len // RET_TILE
        tile_seqs = 1
        cos2, sin2 = _rotary_tables(pos0, seq_len, 1)
        grid = (n_seq, chunks)
        row = lambda n, c: n * chunks + c
        tab_map = lambda n, c: (c, 0)
        state_map = lambda n, c: (n, 0, 0, 0)
        state_block = (1, RET_HEADS, HEAD_DIM, HEAD_DIM)
        sem = ("arbitrary", "arbitrary")
    else:
        tile_seqs = RET_TILE // seq_len
        cos2, sin2 = _rotary_tables(pos0, seq_len, tile_seqs)
        grid = (t // RET_TILE,)
        row = lambda i: i
        tab_map = lambda i: (0, 0)
        state_map = lambda i: (i, 0, 0, 0)
        state_block = (tile_seqs, RET_HEADS, HEAD_DIM, HEAD_DIM)
        sem = ("arbitrary",)
    mask_t, qd_t, kd_t, cd = _decay_tables(RET_TILE // tile_seqs, tile_seqs)

    def col(cb):
        return pl.BlockSpec((RET_TILE, RET_WIDTH), lambda *a: (row(*a), cb))

    const3 = lambda *_: (0, 0, 0)
    in_specs = [
        col(1), col(2), col(3), col(4),
        pl.BlockSpec((RET_TILE, HEAD_DIM), tab_map),
        pl.BlockSpec((RET_TILE, HEAD_DIM), tab_map),
        pl.BlockSpec((RET_HEADS, RET_TILE, RET_TILE), const3),
        pl.BlockSpec((RET_HEADS, RET_TILE, HEAD_DIM), const3),
        pl.BlockSpec((RET_HEADS, RET_TILE, HEAD_DIM), const3),
        pl.BlockSpec(memory_space=pltpu.SMEM),
        pl.BlockSpec((1, RET_WIDTH), lambda *_: (0, 0)),
    ]
    args = [proj, proj, proj, proj, cos2, sin2, mask_t, qd_t, kd_t, cd, gn_w]
    scratch = [pltpu.VMEM((RET_TILE, RET_WIDTH), F32),
               pltpu.VMEM((RET_HEADS, HEAD_DIM, RET_TILE), F32),
               pltpu.VMEM((RET_TILE, RET_WIDTH), F32)]
    if carry:
        scratch.append(pltpu.VMEM((RET_HEADS, HEAD_DIM, HEAD_DIM), F32))
    else:
        in_specs.append(pl.BlockSpec(state_block, state_map))
        args.append(r0)
    return pl.pallas_call(
        functools.partial(_ret_kernel, n_seq=tile_seqs, carry=carry),
        grid=grid,
        in_specs=in_specs,
        out_specs=(pl.BlockSpec((RET_TILE, RET_WIDTH), lambda *a: (row(*a), 0)),
                   pl.BlockSpec(state_block, state_map)),
        out_shape=(jax.ShapeDtypeStruct((t, RET_WIDTH), BF16),
                   jax.ShapeDtypeStruct((n_seq, RET_HEADS, HEAD_DIM, HEAD_DIM), F32)),
        scratch_shapes=scratch,
        compiler_params=_params(*sem),
        name="ret_chain" if carry else "ret_step",
    )(*args)


def _outproj_kernel(x_ref, a_ref, b_ref, wa_ref, wb_ref, o_ref):
    o_ref[...] = (x_ref[...]
                  + jnp.dot(a_ref[...], wa_ref[...], preferred_element_type=F32)
                  + jnp.dot(b_ref[...], wb_ref[...], preferred_element_type=F32))


def _outproj(x2d, a, b, w, tm=512):
    t = x2d.shape[0]
    resident = pl.Buffered(1)
    return pl.pallas_call(
        _outproj_kernel,
        grid=(t // tm,),
        in_specs=[
            pl.BlockSpec((tm, D_MODEL), lambda i: (i, 0)),
            pl.BlockSpec((tm, S5_WIDTH), lambda i: (i, 0)),
            pl.BlockSpec((tm, RET_WIDTH), lambda i: (i, 0)),
            pl.BlockSpec((S5_WIDTH, D_MODEL), lambda i: (0, 0), pipeline_mode=resident),
            pl.BlockSpec((RET_WIDTH, D_MODEL), lambda i: (1, 0), pipeline_mode=resident),
        ],
        out_specs=pl.BlockSpec((tm, D_MODEL), lambda i: (i, 0)),
        out_shape=jax.ShapeDtypeStruct((t, D_MODEL), F32),
        compiler_params=_params("arbitrary"),
        name="out_proj",
    )(x2d, a, b, w, w)


def _ffn_kernel(x_ref, gn_ref, wg_ref, wu_ref, wo_ref, gf_ref, o_ref, h_scr):
    j = pl.program_id(1)

    @pl.when(j == 0)
    def _():
        h_scr[...] = _rms(x_ref[...], gn_ref[...]).astype(BF16)
        o_ref[...] = x_ref[...]

    for r in range(x_ref.shape[0] // FFN_ROWS):
        rows = slice(r * FFN_ROWS, (r + 1) * FFN_ROWS)
        h = h_scr[rows, :]
        gate = jnp.dot(h, wg_ref[...], preferred_element_type=F32)
        up = jnp.dot(h, wu_ref[...], preferred_element_type=F32)
        act = (jax.nn.silu(gate) * up).astype(BF16)
        o_ref[rows, :] += jnp.dot(act, wo_ref[...], preferred_element_type=F32)

    @pl.when(j == pl.num_programs(1) - 1)
    def _():
        o_ref[...] = _rms(o_ref[...], gf_ref[...])


def _ffn(x2d, g_ffn, w_in, w_out, g_final, tm=1024, tf=512):
    t = x2d.shape[0]
    nf = D_FF // tf
    return pl.pallas_call(
        _ffn_kernel,
        grid=(t // tm, nf),
        in_specs=[
            pl.BlockSpec((tm, D_MODEL), lambda i, j: (i, 0)),
            pl.BlockSpec((1, D_MODEL), lambda i, j: (0, 0)),
            pl.BlockSpec((D_MODEL, tf), lambda i, j: (0, j)),
            pl.BlockSpec((D_MODEL, tf), lambda i, j: (0, j + nf)),
            pl.BlockSpec((tf, D_MODEL), lambda i, j: (j, 0)),
            pl.BlockSpec((1, D_MODEL), lambda i, j: (0, 0)),
        ],
        out_specs=pl.BlockSpec((tm, D_MODEL), lambda i, j: (i, 0)),
        out_shape=jax.ShapeDtypeStruct((t, D_MODEL), F32),
        scratch_shapes=[pltpu.VMEM((tm, D_MODEL), BF16)],
        compiler_params=_params("arbitrary", "arbitrary"),
        name="ffn",
    )(x2d, g_ffn, w_in, w_in, w_out, g_final)


def _layer(x, s5_state, ret_state, pos0, w):
    n, l, _ = x.shape
    x2d = x.reshape(n * l, D_MODEL)
    s5_w = (w["cd"], w["d"], w["w_glu"], w["b_glu"])
    if s5_state is None:
        proj, u_t = _inproj(x2d, w["norm_mix"], w["w_in"], n, True, tm=256)
        s5_out, s5_re, s5_im = _s5_seq(u_t, w["pairs"], w["wb2"], *s5_w, n, l)
    else:
        (proj,) = _inproj(x2d, w["norm_mix"], w["w_in"], n, False, tm=512, row_split=2)
        s5_out, s5_re, s5_im = _s5_step(proj, s5_state, w["tabs"], w["wb"], *s5_w)
    ret_out, ret_new = _retention(proj, ret_state, w["gn_w"], n, l, pos0)
    x1 = _outproj(x2d, s5_out, ret_out, w["w_out"])
    y = _ffn(x1, w["norm_ffn"], w["w_ffn_in"], w["w_ffn_out"], w["norm_final"])
    return (y.reshape(n, l, D_MODEL),
            s5_re.reshape(n, S5_GROUPS, S5_STATE), s5_im.reshape(n, S5_GROUPS, S5_STATE), ret_new)


def kernel(x_prompt, x_sample, state_s5_re, state_s5_im, state_ret, norm_mix, w_in, s5_lambda_re, s5_lambda_im, s5_log_step, s5_b_re, s5_b_im, s5_c_re, s5_c_im, s5_d, s5_w_glu, s5_b_glu, ret_gn_w, w_out, norm_ffn, w_ffn_in, w_ffn_out, norm_final):
    assert norm_mix.shape[0] == 1, "single-layer stack"
    tabr, tabi, pairr, pairi, bbr, bbi, abr, abi = _s5_params(
        s5_lambda_re[0], s5_lambda_im[0], s5_log_step[0], s5_b_re[0], s5_b_im[0])
    wb = (_block_diag_in(bbr), _block_diag_in(bbi))
    w = dict(
        norm_mix=norm_mix, w_in=w_in[0].astype(BF16),
        tabs=(tabr, tabi), pairs=(pairr, pairi), wb=wb,
        wb2=(jnp.concatenate([wb[0], _block_diag_in(abr)], axis=1),
             jnp.concatenate([wb[1], _block_diag_in(abi)], axis=1)),
        cd=(_block_diag_out(s5_c_re[0]), _block_diag_out(s5_c_im[0])),
        d=s5_d, w_glu=s5_w_glu[0].astype(BF16), b_glu=s5_b_glu,
        gn_w=ret_gn_w, w_out=w_out[0].astype(BF16), norm_ffn=norm_ffn,
        w_ffn_in=w_ffn_in[0].astype(BF16), w_ffn_out=w_ffn_out[0].astype(BF16),
        norm_final=norm_final.reshape(1, D_MODEL),
    )
    n_s = x_sample.shape[0]
    yp, p_re, p_im, p_ret = _layer(x_prompt, None, None, 0.0, w)
    ys, s_re, s_im, s_ret = _layer(
        x_sample,
        (state_s5_re[0].reshape(n_s, N_STATE), state_s5_im[0].reshape(n_s, N_STATE)),
        state_ret[0], float(PAST_LEN), w)
    return (yp, ys, p_re[None], p_im[None], p_ret[None], s_re[None], s_im[None], s_ret[None])
```

```python
import functools
import math

import jax
import jax.numpy as jnp
from jax import lax
from jax.experimental import pallas as pl
from jax.experimental.pallas import tpu as pltpu

F32 = jnp.float32
BF16 = jnp.bfloat16

D_MODEL = 2048
S5_WIDTH = 1024
S5_GROUP = 16
S5_GROUPS = 64
S5_STATE = 64
N_STATE = S5_GROUPS * S5_STATE
RET_WIDTH = 1024
RET_HEADS = 8
HEAD_DIM = 128
ROPE_BASE = 10000.0
D_FF = 5632
IN_WIDTH = S5_WIDTH + 4 * RET_WIDTH
NORM_EPS = 1e-6
PAST_LEN = 16384

SUBLANES = 8
SLAB_GROUPS = 8
N_SLABS = S5_GROUPS // SLAB_GROUPS
SLAB_IN = SLAB_GROUPS * S5_GROUP
SLAB_STATE = SLAB_GROUPS * S5_STATE
RET_TILE = 128
FFN_ROWS = 512

VMEM_LIMIT_BYTES = 60 * 1024 * 1024


def _params(*sem):
    return pltpu.CompilerParams(dimension_semantics=sem, vmem_limit_bytes=VMEM_LIMIT_BYTES)


def _rms(xf, g):
    ms = jnp.mean(xf * xf, axis=-1, keepdims=True)
    return xf * lax.rsqrt(ms + NORM_EPS) * g


def _inproj_kernel(x_ref, g_ref, w_ref, o_ref, *rest, row_split, tn, interleave):
    if interleave:
        ou_ref, h_scr = rest
        seq = pl.program_id(1)
    else:
        (h_scr,) = rest
    tm = x_ref.shape[0]
    rs = tm // row_split
    for r in range(row_split):
        rows = slice(r * rs, (r + 1) * rs)
        h_scr[rows, :] = _rms(x_ref[rows, :], g_ref[...]).astype(BF16)
        for j in range(IN_WIDTH // tn):
            cols = slice(j * tn, (j + 1) * tn)
            res = jnp.dot(h_scr[rows, :], w_ref[:, cols], preferred_element_type=F32)
            if interleave and j * tn < S5_WIDTH:
                dst = pl.ds(r * rs * interleave + seq, rs, stride=interleave)
                for s in range(tn // SLAB_IN):
                    ou_ref[j * (tn // SLAB_IN) + s, dst, :] = res[:, s * SLAB_IN:(s + 1) * SLAB_IN]
            o_ref[rows, cols] = res.astype(o_ref.dtype)


def _inproj(x2d, g, w, n_seq, interleave, tm, tn=1024, row_split=1):
    t = x2d.shape[0]
    resident = pl.Buffered(1)
    out_shape = [jax.ShapeDtypeStruct((t, IN_WIDTH), BF16)]
    if interleave:
        tiles = t // n_seq // tm
        grid = (tiles, n_seq)
        row = lambda i, n: (n * tiles + i, 0)
        const = lambda i, n: (0, 0)
        out_specs = [pl.BlockSpec((tm, IN_WIDTH), row),
                     pl.BlockSpec((N_SLABS, n_seq * tm, SLAB_IN), lambda i, n: (0, i, 0))]
        out_shape.append(jax.ShapeDtypeStruct((N_SLABS, t, SLAB_IN), F32))
        sem = ("arbitrary", "arbitrary")
    else:
        grid = (t // tm,)
        row = lambda i: (i, 0)
        const = lambda i: (0, 0)
        out_specs = [pl.BlockSpec((tm, IN_WIDTH), row)]
        sem = ("arbitrary",)
    return pl.pallas_call(
        functools.partial(_inproj_kernel, row_split=row_split, tn=tn,
                          interleave=n_seq if interleave else 0),
        grid=grid,
        in_specs=[
            pl.BlockSpec((tm, D_MODEL), row),
            pl.BlockSpec((1, D_MODEL), const),
            pl.BlockSpec((D_MODEL, IN_WIDTH), const, pipeline_mode=resident),
        ],
        out_specs=out_specs,
        out_shape=out_shape,
        scratch_shapes=[pltpu.VMEM((tm, D_MODEL), BF16)],
        compiler_params=_params(*sem),
        name="in_proj_seq" if interleave else "in_proj",
    )(x2d, g, w)


def _s5_param_kernel(lr_ref, li_ref, ls_ref, br_ref, bi_ref,
                     tabr_ref, tabi_ref, pairr_ref, pairi_ref, bbr_ref, bbi_ref, abr_ref, abi_ref):
    lr = lr_ref[...]
    li = li_ref[...]
    dt = jnp.exp(ls_ref[...])
    mag = jnp.exp(lr * dt)
    ar = mag * jnp.cos(li * dt)
    ai = mag * jnp.sin(li * dt)
    den = lr * lr + li * li
    nr = ar - 1.0
    cr = (nr * lr + ai * li) / den
    ci = (ai * lr - nr * li) / den
    br = br_ref[...]
    bi = bi_ref[...]
    bbr = cr * br - ci * bi
    bbi = cr * bi + ci * br
    bbr_ref[...] = bbr
    bbi_ref[...] = bbi
    abr_ref[...] = ar * bbr - ai * bbi
    abi_ref[...] = ar * bbi + ai * bbr

    rows = lax.broadcasted_iota(jnp.int32, (SUBLANES, N_STATE), 0)
    second = rows >= SUBLANES // 2
    full = lambda v: jnp.broadcast_to(v, (SUBLANES, N_STATE))
    pairr_ref[...] = jnp.where(second, full(ar * ar - ai * ai), full(ar))
    pairi_ref[...] = jnp.where(second, full(ar * ai + ai * ar), full(ai))

    pr, pi = ar, ai
    for r in range(SUBLANES):
        tabr_ref[3, r:r + 1, :] = pr
        tabi_ref[3, r:r + 1, :] = pi
        step = r + 1
        if step in (1, 2, 4):
            k = (1, 2, 4).index(step)
            keep = rows >= step
            tabr_ref[k] = jnp.where(keep, jnp.broadcast_to(pr, (SUBLANES, N_STATE)), 0.0)
            tabi_ref[k] = jnp.where(keep, jnp.broadcast_to(pi, (SUBLANES, N_STATE)), 0.0)
        pr, pi = pr * ar - pi * ai, pr * ai + pi * ar


def _s5_params(lam_re, lam_im, log_step, b_re, b_im):
    lr = lam_re.reshape(1, N_STATE)
    li = lam_im.reshape(1, N_STATE)
    ls = jnp.repeat(log_step, S5_STATE).reshape(1, N_STATE)
    br = jnp.transpose(b_re, (2, 0, 1)).reshape(S5_GROUP, N_STATE)
    bi = jnp.transpose(b_im, (2, 0, 1)).reshape(S5_GROUP, N_STATE)
    tab = jax.ShapeDtypeStruct((4, SUBLANES, N_STATE), F32)
    pair = jax.ShapeDtypeStruct((SUBLANES, N_STATE), F32)
    bb = jax.ShapeDtypeStruct((S5_GROUP, N_STATE), F32)
    return pl.pallas_call(
        _s5_param_kernel,
        out_shape=(tab, tab, pair, pair, bb, bb, bb, bb),
        name="s5_params",
    )(lr, li, ls, br, bi)


def _block_diag_in(bb):
    blk = bb.reshape(S5_GROUP, N_SLABS, SLAB_GROUPS, S5_STATE)
    eye = jnp.eye(SLAB_GROUPS, dtype=F32)
    w = jnp.einsum("hsgp,gk->sghkp", blk, eye)
    return w.reshape(N_SLABS, SLAB_IN, SLAB_STATE).astype(BF16)


def _block_diag_out(c):
    blk = c.reshape(N_SLABS, SLAB_GROUPS, S5_GROUP, S5_STATE)
    eye = jnp.eye(SLAB_GROUPS, dtype=F32)
    w = jnp.einsum("sghp,gk->sgpkh", blk, eye)
    return w.reshape(N_SLABS, SLAB_STATE, SLAB_IN).astype(BF16)


def _cmul_add(xr, xi, pr, pi, vr, vi):
    return xr + (pr * vr - pi * vi), xi + (pr * vi + pi * vr)


def _s5_weight_specs(tab_shape, k_in):
    const3 = lambda *_: (0, 0, 0)
    const2 = lambda *_: (0, 0)
    tab_map = const3 if len(tab_shape) == 3 else const2
    return [
        pl.BlockSpec(tab_shape, tab_map),
        pl.BlockSpec(tab_shape, tab_map),
        pl.BlockSpec((N_SLABS, k_in, SLAB_STATE), const3),
        pl.BlockSpec((N_SLABS, k_in, SLAB_STATE), const3),
        pl.BlockSpec((N_SLABS, SLAB_STATE, SLAB_IN), const3),
        pl.BlockSpec((N_SLABS, SLAB_STATE, SLAB_IN), const3),
        pl.BlockSpec((1, S5_WIDTH), const2),
        pl.BlockSpec((S5_WIDTH, S5_WIDTH), const2),
        pl.BlockSpec((1, S5_WIDTH), const2),
    ]


def _s5_step_kernel(u_ref, x0r_ref, x0i_ref, tabr_ref, tabi_ref, wbr_ref, wbi_ref, cdr_ref, cdi_ref,
                    d_ref, wg_ref, bg_ref, o_ref, xlr_ref, xli_ref, xr_s, xi_s, y_s):
    for s in range(N_SLABS):
        ub = u_ref[:, s * SLAB_IN:(s + 1) * SLAB_IN]
        cs = slice(s * SLAB_STATE, (s + 1) * SLAB_STATE)
        xr_s[:, cs] = jnp.dot(ub, wbr_ref[s], preferred_element_type=F32)
        xi_s[:, cs] = jnp.dot(ub, wbi_ref[s], preferred_element_type=F32)

    def block(b, carry):
        r0 = pl.multiple_of(b * SUBLANES, SUBLANES)
        for s in range(N_SLABS):
            cs = slice(s * SLAB_STATE, (s + 1) * SLAB_STATE)
            xr = xr_s[pl.ds(r0, SUBLANES), cs]
            xi = xi_s[pl.ds(r0, SUBLANES), cs]
            for k, sh in enumerate((1, 2, 4)):
                xr, xi = _cmul_add(xr, xi, tabr_ref[k, :, cs], tabi_ref[k, :, cs],
                                   pltpu.roll(xr, sh, 0), pltpu.roll(xi, sh, 0))
            cr = jnp.broadcast_to(x0r_ref[pl.ds(b, 1), cs], (SUBLANES, SLAB_STATE))
            ci = jnp.broadcast_to(x0i_ref[pl.ds(b, 1), cs], (SUBLANES, SLAB_STATE))
            xr, xi = _cmul_add(xr, xi, tabr_ref[3, :, cs], tabi_ref[3, :, cs], cr, ci)
            xr_s[pl.ds(r0, SUBLANES), cs] = xr
            xi_s[pl.ds(r0, SUBLANES), cs] = xi
            xlr_ref[pl.ds(b, 1), cs] = xr[SUBLANES - 1:SUBLANES, :]
            xli_ref[pl.ds(b, 1), cs] = xi[SUBLANES - 1:SUBLANES, :]
        return carry

    lax.fori_loop(0, u_ref.shape[0] // SUBLANES, block, 0)

    for s in range(N_SLABS):
        cs = slice(s * SLAB_STATE, (s + 1) * SLAB_STATE)
        us = slice(s * SLAB_IN, (s + 1) * SLAB_IN)
        y = (jnp.dot(xr_s[:, cs].astype(BF16), cdr_ref[s], preferred_element_type=F32)
             - jnp.dot(xi_s[:, cs].astype(BF16), cdi_ref[s], preferred_element_type=F32))
        y = y + d_ref[:, us] * u_ref[:, us].astype(F32)
        y_s[:, us] = jax.nn.gelu(y)

    y = y_s[...]
    z = jnp.dot(y.astype(BF16), wg_ref[...], preferred_element_type=F32) + bg_ref[...]
    o_ref[...] = (y * jax.nn.sigmoid(z)).astype(o_ref.dtype)


def _s5_step(proj, x0, tabs, wb, cd, d, w_glu, b_glu, tc=256):
    t = proj.shape[0]
    seqs = tc // SUBLANES
    row = lambda i: (i, 0)
    st = jax.ShapeDtypeStruct((t // SUBLANES, N_STATE), F32)
    return pl.pallas_call(
        _s5_step_kernel,
        grid=(t // tc,),
        in_specs=[pl.BlockSpec((tc, S5_WIDTH), row), pl.BlockSpec((seqs, N_STATE), row),
                  pl.BlockSpec((seqs, N_STATE), row)]
        + _s5_weight_specs((4, SUBLANES, N_STATE), SLAB_IN),
        out_specs=(pl.BlockSpec((tc, S5_WIDTH), row), pl.BlockSpec((seqs, N_STATE), row),
                   pl.BlockSpec((seqs, N_STATE), row)),
        out_shape=(jax.ShapeDtypeStruct((t, S5_WIDTH), BF16), st, st),
        scratch_shapes=[pltpu.VMEM((tc, N_STATE), F32), pltpu.VMEM((tc, N_STATE), F32),
                        pltpu.VMEM((tc, S5_WIDTH), F32)],
        compiler_params=_params("arbitrary"),
        name="s5_step",
    )(proj, x0[0], x0[1], tabs[0], tabs[1], wb[0], wb[1], cd[0], cd[1], d, w_glu, b_glu)


def _s5_seq_kernel(u_ref, pairr_ref, pairi_ref, wbr_ref, wbi_ref, cdr_ref, cdi_ref, d_ref, wg_ref,
                   bg_ref, o_ref, xlr_ref, xli_ref, xr_s, xi_s, y_s, o_s, cbr_s, cbi_s):
    n_rows = u_ref.shape[1]
    half = SUBLANES // 2
    tt = n_rows // half

    @pl.when(pl.program_id(0) == 0)
    def _():
        cbr_s[...] = jnp.zeros_like(cbr_s)
        cbi_s[...] = jnp.zeros_like(cbi_s)

    first = lax.broadcasted_iota(jnp.int32, (SUBLANES, SLAB_STATE), 0) < half
    second_step = (lax.broadcasted_iota(jnp.int32, (n_rows, SLAB_IN), 0) & half) != 0
    for s in range(N_SLABS):
        cs = slice(s * SLAB_STATE, (s + 1) * SLAB_STATE)
        us = slice(s * SLAB_IN, (s + 1) * SLAB_IN)
        u = u_ref[s]
        u_prev = jnp.where(second_step, pltpu.roll(u, half, 0), 0.0)
        ub = jnp.concatenate([u, u_prev], axis=1).astype(BF16)
        xr_s[:, cs] = jnp.dot(ub, wbr_ref[s], preferred_element_type=F32)
        xi_s[:, cs] = jnp.dot(ub, wbi_ref[s], preferred_element_type=F32)

        pr, pi = pairr_ref[:, cs], pairi_ref[:, cs]
        cr, ci = cbr_s[:, cs], cbi_s[:, cs]
        for b in range(n_rows // SUBLANES):
            rows = slice(b * SUBLANES, (b + 1) * SUBLANES)
            xr, xi = _cmul_add(xr_s[rows, cs], xi_s[rows, cs], pr, pi, cr, ci)
            xr_s[rows, cs] = xr
            xi_s[rows, cs] = xi
            cr = jnp.where(first, pltpu.roll(xr, half, 0), xr)
            ci = jnp.where(first, pltpu.roll(xi, half, 0), xi)
        cbr_s[:, cs] = cr
        cbi_s[:, cs] = ci

        y = (jnp.dot(xr_s[:, cs].astype(BF16), cdr_ref[s], preferred_element_type=F32)
             - jnp.dot(xi_s[:, cs].astype(BF16), cdi_ref[s], preferred_element_type=F32))
        y_s[:, us] = jax.nn.gelu(y + d_ref[:, us] * u)

    xlr_ref[...] = cbr_s[...]
    xli_ref[...] = cbi_s[...]

    y = y_s[...]
    z = jnp.dot(y.astype(BF16), wg_ref[...], preferred_element_type=F32) + bg_ref[...]
    o = y * jax.nn.sigmoid(z)
    for s in range(N_SLABS):
        o_s[s] = o[:, s * SLAB_IN:(s + 1) * SLAB_IN]
    for n in range(half):
        for s in range(N_SLABS):
            o_ref[n, :, s * SLAB_IN:(s + 1) * SLAB_IN] = (
                o_s[s, pl.ds(n, tt, stride=half), :].astype(o_ref.dtype))


def _s5_seq(u_t, pairs, wb, cd, d, w_glu, b_glu, n_seq, seq_len, tt=64):
    assert 2 * n_seq == SUBLANES
    n_rows = n_seq * tt
    st = jax.ShapeDtypeStruct((SUBLANES, N_STATE), F32)
    y, xlr, xli = pl.pallas_call(
        _s5_seq_kernel,
        grid=(seq_len // tt,),
        in_specs=[pl.BlockSpec((N_SLABS, n_rows, SLAB_IN), lambda i: (0, i, 0))]
        + _s5_weight_specs((SUBLANES, N_STATE), 2 * SLAB_IN),
        out_specs=(pl.BlockSpec((n_seq, tt, S5_WIDTH), lambda i: (0, i, 0)),
                   pl.BlockSpec((SUBLANES, N_STATE), lambda i: (0, 0)),
                   pl.BlockSpec((SUBLANES, N_STATE), lambda i: (0, 0))),
        out_shape=(jax.ShapeDtypeStruct((n_seq, seq_len, S5_WIDTH), BF16), st, st),
        scratch_shapes=[pltpu.VMEM((n_rows, N_STATE), F32), pltpu.VMEM((n_rows, N_STATE), F32),
                        pltpu.VMEM((n_rows, S5_WIDTH), F32),
                        pltpu.VMEM((N_SLABS, n_rows, SLAB_IN), F32),
                        pltpu.VMEM((SUBLANES, N_STATE), F32), pltpu.VMEM((SUBLANES, N_STATE), F32)],
        compiler_params=_params("arbitrary"),
        name="s5_seq",
    )(u_t, pairs[0], pairs[1], wb[0], wb[1], cd[0], cd[1], d, w_glu, b_glu)
    return y.reshape(n_seq * seq_len, S5_WIDTH), xlr[:n_seq], xli[:n_seq]


_NT = (((1,), (1,)), ((), ()))
_TN = (((0,), (0,)), ((), ()))


def _rotary(x, cos2, sin2):
    return x * cos2 + pltpu.roll(x, HEAD_DIM // 2, 1) * sin2


def _norm_gate(o, g, gn_w):
    mu = jnp.mean(o, axis=-1, keepdims=True)
    oc = o - mu
    var = jnp.mean(oc * oc, axis=-1, keepdims=True)
    return jax.nn.silu(g) * (oc * lax.rsqrt(var + NORM_EPS) * gn_w)


def _ret_seq_kernel(q_ref, k_ref, v_ref, g_ref, cos_ref, sin_ref, mask_ref, qd_ref, kd_ref, cd_ref,
                    gnw_ref, o_ref, rn_ref, r_s, *, chunks):
    @pl.when(pl.program_id(1) == 0)
    def _():
        r_s[...] = jnp.zeros_like(r_s)

    scale = HEAD_DIM ** -0.5
    for c in range(chunks):
        rows = slice(c * RET_TILE, (c + 1) * RET_TILE)
        cos = cos_ref[rows, :]
        sin = sin_ref[rows, :]
        for h in range(RET_HEADS):
            hs = slice(h * HEAD_DIM, (h + 1) * HEAD_DIM)
            qr = _rotary(q_ref[rows, hs].astype(F32), cos, sin)
            kr = _rotary(k_ref[rows, hs].astype(F32), cos, sin) * scale
            vb = v_ref[rows, hs]
            sc = lax.dot_general(qr.astype(BF16), kr.astype(BF16), _NT,
                                 preferred_element_type=F32) * mask_ref[h]
            r_old = r_s[h]
            lhs = jnp.concatenate([sc.astype(BF16), (qr * qd_ref[h]).astype(BF16)], axis=1)
            rhs = jnp.concatenate([vb, r_old.astype(BF16)], axis=0)
            o = jnp.dot(lhs, rhs, preferred_element_type=F32)
            r_s[h] = r_old * cd_ref[h] + lax.dot_general(
                (kr * kd_ref[h]).astype(BF16), vb, _TN, preferred_element_type=F32)
            o_ref[rows, hs] = _norm_gate(o, g_ref[rows, hs].astype(F32),
                                         gnw_ref[:, hs]).astype(o_ref.dtype)
    rn_ref[0] = r_s[...]


def _ret_step_kernel(q_ref, k_ref, v_ref, g_ref, cos_ref, sin_ref, mask_ref, qd_ref, kd_ref, cd_ref,
                     gnw_ref, r0_ref, o_ref, rn_ref, qr_s, kdt_s, ob_s, *, n_seq):
    rows_per_seq = RET_TILE // n_seq
    cos = cos_ref[...]
    sin = sin_ref[...]
    scale = HEAD_DIM ** -0.5
    for h in range(RET_HEADS):
        hs = slice(h * HEAD_DIM, (h + 1) * HEAD_DIM)
        qr = _rotary(q_ref[:, hs].astype(F32), cos, sin)
        kr = _rotary(k_ref[:, hs].astype(F32), cos, sin) * scale
        sc = lax.dot_general(qr.astype(BF16), kr.astype(BF16), _NT,
                             preferred_element_type=F32) * mask_ref[h]
        ob_s[:, hs] = jnp.dot(sc.astype(BF16), v_ref[:, hs], preferred_element_type=F32)
        qr_s[:, hs] = qr
        kdt_s[h] = (kr * kd_ref[h]).T

    lane = lax.broadcasted_iota(jnp.int32, (HEAD_DIM, RET_TILE), 1)

    def per_seq(s, c):
        r0 = pl.multiple_of(s * rows_per_seq, rows_per_seq)
        rows = pl.ds(r0, rows_per_seq)
        in_seq = (lane >= r0) & (lane < r0 + rows_per_seq)
        for h in range(RET_HEADS):
            hs = slice(h * HEAD_DIM, (h + 1) * HEAD_DIM)
            r_old = r0_ref[s, h]
            cross = jnp.dot(qr_s[rows, hs].astype(BF16), r_old.astype(BF16), preferred_element_type=F32)
            ob_s[rows, hs] = ob_s[rows, hs] + cross * qd_ref[h, rows, :]
            kdt = jnp.where(in_seq, kdt_s[h], 0.0).astype(BF16)
            rn_ref[s, h] = r_old * cd_ref[h] + jnp.dot(kdt, v_ref[:, hs], preferred_element_type=F32)
        return c

    lax.fori_loop(0, n_seq, per_seq, 0)

    for h in range(RET_HEADS):
        hs = slice(h * HEAD_DIM, (h + 1) * HEAD_DIM)
        o_ref[:, hs] = _norm_gate(ob_s[:, hs], g_ref[:, hs].astype(F32),
                                  gnw_ref[:, hs]).astype(o_ref.dtype)


def _rotary_tables(pos0, rows, reps):
    half = HEAD_DIM // 2
    inv_freq = ROPE_BASE ** (-jnp.arange(half, dtype=F32) / half)
    pos = pos0 + jnp.arange(rows, dtype=F32)
    ang = pos[:, None] * inv_freq[None, :]
    cos, sin = jnp.cos(ang), jnp.sin(ang)
    cos2 = jnp.concatenate([cos, cos], axis=-1)
    sin2 = jnp.concatenate([-sin, sin], axis=-1)
    return jnp.tile(cos2, (reps, 1)), jnp.tile(sin2, (reps, 1))


def _decay_tables(chunk, n_seq):
    log_gamma = jnp.log(1.0 - 2.0 ** (-5.0 - jnp.arange(RET_HEADS, dtype=F32)))
    idx = jnp.arange(chunk, dtype=F32)
    diff = idx[:, None] - idx[None, :]
    mask = jnp.where(diff >= 0, jnp.exp(log_gamma[:, None, None] * jnp.maximum(diff, 0.0)), 0.0)
    q_decay = jnp.exp(log_gamma[:, None] * (idx + 1.0))
    k_decay = jnp.exp(log_gamma[:, None] * (chunk - 1.0 - idx))
    chunk_decay = jnp.exp(log_gamma * chunk)
    eye = jnp.eye(n_seq, dtype=F32)
    mask_t = jnp.einsum("hab,st->hsatb", mask, eye).reshape(RET_HEADS, RET_TILE, RET_TILE)
    qd_t = jnp.broadcast_to(jnp.tile(q_decay, (1, n_seq))[:, :, None], (RET_HEADS, RET_TILE, HEAD_DIM))
    kd_t = jnp.broadcast_to(jnp.tile(k_decay, (1, n_seq))[:, :, None], (RET_HEADS, RET_TILE, HEAD_DIM))
    return mask_t, qd_t, kd_t, chunk_decay


def _retention(proj, r0, gn_w, n_seq, seq_len, pos0, chunks=4):
    t = proj.shape[0]
    const3 = lambda *_: (0, 0, 0)
    if r0 is None:
        tile, tile_seqs = chunks * RET_TILE, 1
        steps = seq_len // tile
        cos2, sin2 = _rotary_tables(pos0, seq_len, 1)
        grid = (n_seq, steps)
        row = lambda n, c: n * steps + c
        tab_map = lambda n, c: (c, 0)
        state_map = lambda n, c: (n, 0, 0, 0)
        state_block = (1, RET_HEADS, HEAD_DIM, HEAD_DIM)
        sem = ("arbitrary", "arbitrary")
        body = functools.partial(_ret_seq_kernel, chunks=chunks)
        scratch = [pltpu.VMEM((RET_HEADS, HEAD_DIM, HEAD_DIM), F32)]
        name = "ret_seq"
    else:
        tile, tile_seqs = RET_TILE, RET_TILE // seq_len
        cos2, sin2 = _rotary_tables(pos0, seq_len, tile_seqs)
        grid = (t // tile,)
        row = lambda i: i
        tab_map = lambda i: (0, 0)
        state_map = lambda i: (i, 0, 0, 0)
        state_block = (tile_seqs, RET_HEADS, HEAD_DIM, HEAD_DIM)
        sem = ("arbitrary",)
        body = functools.partial(_ret_step_kernel, n_seq=tile_seqs)
        scratch = [pltpu.VMEM((RET_TILE, RET_WIDTH), F32),
                   pltpu.VMEM((RET_HEADS, HEAD_DIM, RET_TILE), F32),
                   pltpu.VMEM((RET_TILE, RET_WIDTH), F32)]
        name = "ret_step"
    mask_t, qd_t, kd_t, cd = _decay_tables(RET_TILE // tile_seqs, tile_seqs)

    def col(cb):
        return pl.BlockSpec((tile, RET_WIDTH), lambda *a: (row(*a), cb))

    in_specs = [
        col(1), col(2), col(3), col(4),
        pl.BlockSpec((tile, HEAD_DIM), tab_map),
        pl.BlockSpec((tile, HEAD_DIM), tab_map),
        pl.BlockSpec((RET_HEADS, RET_TILE, RET_TILE), const3),
        pl.BlockSpec((RET_HEADS, RET_TILE, HEAD_DIM), const3),
        pl.BlockSpec((RET_HEADS, RET_TILE, HEAD_DIM), const3),
        pl.BlockSpec(memory_space=pltpu.SMEM),
        pl.BlockSpec((1, RET_WIDTH), lambda *_: (0, 0)),
    ]
    args = [proj, proj, proj, proj, cos2, sin2, mask_t, qd_t, kd_t, cd, gn_w]
    if r0 is not None:
        in_specs.append(pl.BlockSpec(state_block, state_map))
        args.append(r0)
    return pl.pallas_call(
        body,
        grid=grid,
        in_specs=in_specs,
        out_specs=(pl.BlockSpec((tile, RET_WIDTH), lambda *a: (row(*a), 0)),
                   pl.BlockSpec(state_block, state_map)),
        out_shape=(jax.ShapeDtypeStruct((t, RET_WIDTH), BF16),
                   jax.ShapeDtypeStruct((n_seq, RET_HEADS, HEAD_DIM, HEAD_DIM), F32)),
        scratch_shapes=scratch,
        compiler_params=_params(*sem),
        name=name,
    )(*args)


def _outproj_kernel(x_ref, a_ref, b_ref, wa_ref, wb_ref, o_ref):
    o_ref[...] = (x_ref[...]
                  + jnp.dot(a_ref[...], wa_ref[...], preferred_element_type=F32)
                  + jnp.dot(b_ref[...], wb_ref[...], preferred_element_type=F32))


def _outproj(x2d, a, b, w, tm=512):
    t = x2d.shape[0]
    resident = pl.Buffered(1)
    return pl.pallas_call(
        _outproj_kernel,
        grid=(t // tm,),
        in_specs=[
            pl.BlockSpec((tm, D_MODEL), lambda i: (i, 0)),
            pl.BlockSpec((tm, S5_WIDTH), lambda i: (i, 0)),
            pl.BlockSpec((tm, RET_WIDTH), lambda i: (i, 0)),
            pl.BlockSpec((S5_WIDTH, D_MODEL), lambda i: (0, 0), pipeline_mode=resident),
            pl.BlockSpec((RET_WIDTH, D_MODEL), lambda i: (1, 0), pipeline_mode=resident),
        ],
        out_specs=pl.BlockSpec((tm, D_MODEL), lambda i: (i, 0)),
        out_shape=jax.ShapeDtypeStruct((t, D_MODEL), F32),
        compiler_params=_params("arbitrary"),
        name="out_proj",
    )(x2d, a, b, w, w)


def _ffn_kernel(x_ref, gn_ref, wg_ref, wu_ref, wo_ref, gf_ref, o_ref, h_scr):
    j = pl.program_id(1)

    @pl.when(j == 0)
    def _():
        h_scr[...] = _rms(x_ref[...], gn_ref[...]).astype(BF16)
        o_ref[...] = x_ref[...]

    for r in range(x_ref.shape[0] // FFN_ROWS):
        rows = slice(r * FFN_ROWS, (r + 1) * FFN_ROWS)
        h = h_scr[rows, :]
        gate = jnp.dot(h, wg_ref[...], preferred_element_type=F32)
        up = jnp.dot(h, wu_ref[...], preferred_element_type=F32)
        act = (jax.nn.silu(gate) * up).astype(BF16)
        o_ref[rows, :] += jnp.dot(act, wo_ref[...], preferred_element_type=F32)

    @pl.when(j == pl.num_programs(1) - 1)
    def _():
        o_ref[...] = _rms(o_ref[...], gf_ref[...])


def _ffn(x2d, g_ffn, w_in, w_out, g_final, tm=1024, tf=512):
    t = x2d.shape[0]
    nf = D_FF // tf
    return pl.pallas_call(
        _ffn_kernel,
        grid=(t // tm, nf),
        in_specs=[
            pl.BlockSpec((tm, D_MODEL), lambda i, j: (i, 0)),
            pl.BlockSpec((1, D_MODEL), lambda i, j: (0, 0)),
            pl.BlockSpec((D_MODEL, tf), lambda i, j: (0, j)),
            pl.BlockSpec((D_MODEL, tf), lambda i, j: (0, j + nf)),
            pl.BlockSpec((tf, D_MODEL), lambda i, j: (j, 0)),
            pl.BlockSpec((1, D_MODEL), lambda i, j: (0, 0)),
        ],
        out_specs=pl.BlockSpec((tm, D_MODEL), lambda i, j: (i, 0)),
        out_shape=jax.ShapeDtypeStruct((t, D_MODEL), F32),
        scratch_shapes=[pltpu.VMEM((tm, D_MODEL), BF16)],
        compiler_params=_params("arbitrary", "arbitrary"),
        name="ffn",
    )(x2d, g_ffn, w_in, w_in, w_out, g_final)


def _layer(x, s5_state, ret_state, pos0, w):
    n, l, _ = x.shape
    x2d = x.reshape(n * l, D_MODEL)
    s5_w = (w["cd"], w["d"], w["w_glu"], w["b_glu"])
    if s5_state is None:
        proj, u_t = _inproj(x2d, w["norm_mix"], w["w_in"], n, True, tm=256)
        s5_out, s5_re, s5_im = _s5_seq(u_t, w["pairs"], w["wb2"], *s5_w, n, l)
    else:
        (proj,) = _inproj(x2d, w["norm_mix"], w["w_in"], n, False, tm=512, row_split=2)
        s5_out, s5_re, s5_im = _s5_step(proj, s5_state, w["tabs"], w["wb"], *s5_w)
    ret_out, ret_new = _retention(proj, ret_state, w["gn_w"], n, l, pos0)
    x1 = _outproj(x2d, s5_out, ret_out, w["w_out"])
    y = _ffn(x1, w["norm_ffn"], w["w_ffn_in"], w["w_ffn_out"], w["norm_final"])
    return (y.reshape(n, l, D_MODEL),
            s5_re.reshape(n, S5_GROUPS, S5_STATE), s5_im.reshape(n, S5_GROUPS, S5_STATE), ret_new)


def kernel(x_prompt, x_sample, state_s5_re, state_s5_im, state_ret, norm_mix, w_in, s5_lambda_re, s5_lambda_im, s5_log_step, s5_b_re, s5_b_im, s5_c_re, s5_c_im, s5_d, s5_w_glu, s5_b_glu, ret_gn_w, w_out, norm_ffn, w_ffn_in, w_ffn_out, norm_final):
    assert norm_mix.shape[0] == 1, "single-layer stack"
    tabr, tabi, pairr, pairi, bbr, bbi, abr, abi = _s5_params(
        s5_lambda_re[0], s5_lambda_im[0], s5_log_step[0], s5_b_re[0], s5_b_im[0])
    wb = (_block_diag_in(bbr), _block_diag_in(bbi))
    w = dict(
        norm_mix=norm_mix, w_in=w_in[0].astype(BF16),
        tabs=(tabr, tabi), pairs=(pairr, pairi), wb=wb,
        wb2=(jnp.concatenate([wb[0], _block_diag_in(abr)], axis=1),
             jnp.concatenate([wb[1], _block_diag_in(abi)], axis=1)),
        cd=(_block_diag_out(s5_c_re[0]), _block_diag_out(s5_c_im[0])),
        d=s5_d, w_glu=s5_w_glu[0].astype(BF16), b_glu=s5_b_glu,
        gn_w=ret_gn_w, w_out=w_out[0].astype(BF16), norm_ffn=norm_ffn,
        w_ffn_in=w_ffn_in[0].astype(BF16), w_ffn_out=w_ffn_out[0].astype(BF16),
        norm_final=norm_final.reshape(1, D_MODEL),
    )
    n_s = x_sample.shape[0]
    yp, p_re, p_im, p_ret = _layer(x_prompt, None, None, 0.0, w)
    ys, s_re, s_im, s_ret = _layer(
        x_sample,
        (state_s5_re[0].reshape(n_s, N_STATE), state_s5_im[0].reshape(n_s, N_STATE)),
        state_ret[0], float(PAST_LEN), w)
    return (yp, ys, p_re[None], p_im[None], p_ret[None], s_re[None], s_im[None], s_ret[None])
```

```python
import functools
import math

import jax
import jax.numpy as jnp
from jax import lax
from jax.experimental import pallas as pl
from jax.experimental.pallas import tpu as pltpu

F32 = jnp.float32
BF16 = jnp.bfloat16

D_MODEL = 2048
S5_WIDTH = 1024
S5_GROUP = 16
S5_GROUPS = 64
S5_STATE = 64
N_STATE = S5_GROUPS * S5_STATE
RET_WIDTH = 1024
RET_HEADS = 8
HEAD_DIM = 128
ROPE_BASE = 10000.0
D_FF = 5632
IN_WIDTH = S5_WIDTH + 4 * RET_WIDTH
NORM_EPS = 1e-6
PAST_LEN = 16384

SUBLANES = 8
SLAB_GROUPS = 8
N_SLABS = S5_GROUPS // SLAB_GROUPS
SLAB_IN = SLAB_GROUPS * S5_GROUP
SLAB_STATE = SLAB_GROUPS * S5_STATE
RET_TILE = 128
FFN_ROWS = 512

VMEM_LIMIT_BYTES = 60 * 1024 * 1024


def _params(*sem):
    return pltpu.CompilerParams(dimension_semantics=sem, vmem_limit_bytes=VMEM_LIMIT_BYTES)


def _rms(xf, g):
    ms = jnp.mean(xf * xf, axis=-1, keepdims=True)
    return xf * lax.rsqrt(ms + NORM_EPS) * g


def _inproj_seq_kernel(x_ref, g_ref, w_ref, o_ref, ou_ref, h_scr, *, tn, n_seq):
    seq = pl.program_id(1)
    tm = x_ref.shape[0]
    h_scr[...] = _rms(x_ref[...], g_ref[...]).astype(BF16)
    for j in range(IN_WIDTH // tn):
        cols = slice(j * tn, (j + 1) * tn)
        res = jnp.dot(h_scr[...], w_ref[:, cols], preferred_element_type=F32)
        if j * tn < S5_WIDTH:
            dst = pl.ds(seq, tm, stride=n_seq)
            for s in range(tn // SLAB_IN):
                ou_ref[j * (tn // SLAB_IN) + s, dst, :] = res[:, s * SLAB_IN:(s + 1) * SLAB_IN]
        o_ref[:, cols] = res.astype(o_ref.dtype)


def _inproj_seq(x2d, g, w, n_seq, tm=256, tn=1024):
    t = x2d.shape[0]
    tiles = t // n_seq // tm
    row = lambda i, n: (n * tiles + i, 0)
    const = lambda i, n: (0, 0)
    return pl.pallas_call(
        functools.partial(_inproj_seq_kernel, tn=tn, n_seq=n_seq),
        grid=(tiles, n_seq),
        in_specs=[
            pl.BlockSpec((tm, D_MODEL), row),
            pl.BlockSpec((1, D_MODEL), const),
            pl.BlockSpec((D_MODEL, IN_WIDTH), const, pipeline_mode=pl.Buffered(1)),
        ],
        out_specs=[pl.BlockSpec((tm, IN_WIDTH), row),
                   pl.BlockSpec((N_SLABS, n_seq * tm, SLAB_IN), lambda i, n: (0, i, 0))],
        out_shape=[jax.ShapeDtypeStruct((t, IN_WIDTH), BF16),
                   jax.ShapeDtypeStruct((N_SLABS, t, SLAB_IN), F32)],
        scratch_shapes=[pltpu.VMEM((tm, D_MODEL), BF16)],
        compiler_params=_params("arbitrary", "arbitrary"),
        name="in_proj_seq",
    )(x2d, g, w)


def _inproj_cast_kernel(x_ref, g_ref, w_ref, o_ref, wb_ref, h_scr):
    @pl.when(pl.program_id(0) == 0)
    def _():
        h_scr[...] = _rms(x_ref[...], g_ref[...]).astype(BF16)

    wb_ref[...] = w_ref[...].astype(BF16)
    o_ref[...] = jnp.dot(h_scr[...], wb_ref[...], preferred_element_type=F32).astype(o_ref.dtype)


def _inproj_cast(x2d, g, w_f32, tn=1024):
    t = x2d.shape[0]
    return pl.pallas_call(
        _inproj_cast_kernel,
        grid=(IN_WIDTH // tn,),
        in_specs=[
            pl.BlockSpec((t, D_MODEL), lambda j: (0, 0), pipeline_mode=pl.Buffered(1)),
            pl.BlockSpec((1, D_MODEL), lambda j: (0, 0)),
            pl.BlockSpec((D_MODEL, tn), lambda j: (0, j)),
        ],
        out_specs=[pl.BlockSpec((t, tn), lambda j: (0, j)),
                   pl.BlockSpec((D_MODEL, tn), lambda j: (0, j))],
        out_shape=[jax.ShapeDtypeStruct((t, IN_WIDTH), BF16),
                   jax.ShapeDtypeStruct((D_MODEL, IN_WIDTH), BF16)],
        scratch_shapes=[pltpu.VMEM((t, D_MODEL), BF16)],
        compiler_params=_params("arbitrary"),
        name="in_proj_cast",
    )(x2d, g, w_f32)


def _s5_param_kernel(lr_ref, li_ref, ls_ref, br_ref, bi_ref,
                     tabr_ref, tabi_ref, pairr_ref, pairi_ref, bbr_ref, bbi_ref, abr_ref, abi_ref):
    lr = lr_ref[...]
    li = li_ref[...]
    dt = jnp.exp(ls_ref[...])
    mag = jnp.exp(lr * dt)
    ar = mag * jnp.cos(li * dt)
    ai = mag * jnp.sin(li * dt)
    den = lr * lr + li * li
    nr = ar - 1.0
    cr = (nr * lr + ai * li) / den
    ci = (ai * lr - nr * li) / den
    br = br_ref[...]
    bi = bi_ref[...]
    bbr = cr * br - ci * bi
    bbi = cr * bi + ci * br
    bbr_ref[...] = bbr
    bbi_ref[...] = bbi
    abr_ref[...] = ar * bbr - ai * bbi
    abi_ref[...] = ar * bbi + ai * bbr

    rows = lax.broadcasted_iota(jnp.int32, (SUBLANES, N_STATE), 0)
    second = rows >= SUBLANES // 2
    full = lambda v: jnp.broadcast_to(v, (SUBLANES, N_STATE))
    pairr_ref[...] = jnp.where(second, full(ar * ar - ai * ai), full(ar))
    pairi_ref[...] = jnp.where(second, full(ar * ai + ai * ar), full(ai))

    pr, pi = ar, ai
    for r in range(SUBLANES):
        tabr_ref[3, r:r + 1, :] = pr
        tabi_ref[3, r:r + 1, :] = pi
        step = r + 1
        if step in (1, 2, 4):
            k = (1, 2, 4).index(step)
            keep = rows >= step
            tabr_ref[k] = jnp.where(keep, jnp.broadcast_to(pr, (SUBLANES, N_STATE)), 0.0)
            tabi_ref[k] = jnp.where(keep, jnp.broadcast_to(pi, (SUBLANES, N_STATE)), 0.0)
        pr, pi = pr * ar - pi * ai, pr * ai + pi * ar


def _s5_params(lam_re, lam_im, log_step, b_re, b_im):
    lr = lam_re.reshape(1, N_STATE)
    li = lam_im.reshape(1, N_STATE)
    ls = jnp.repeat(log_step, S5_STATE).reshape(1, N_STATE)
    br = jnp.transpose(b_re, (2, 0, 1)).reshape(S5_GROUP, N_STATE)
    bi = jnp.transpose(b_im, (2, 0, 1)).reshape(S5_GROUP, N_STATE)
    tab = jax.ShapeDtypeStruct((4, SUBLANES, N_STATE), F32)
    pair = jax.ShapeDtypeStruct((SUBLANES, N_STATE), F32)
    bb = jax.ShapeDtypeStruct((S5_GROUP, N_STATE), F32)
    return pl.pallas_call(
        _s5_param_kernel,
        out_shape=(tab, tab, pair, pair, bb, bb, bb, bb),
        name="s5_params",
    )(lr, li, ls, br, bi)


def _block_diag_in(bb):
    blk = bb.reshape(S5_GROUP, N_SLABS, SLAB_GROUPS, S5_STATE)
    eye = jnp.eye(SLAB_GROUPS, dtype=F32)
    w = jnp.einsum("hsgp,gk->sghkp", blk, eye)
    return w.reshape(N_SLABS, SLAB_IN, SLAB_STATE).astype(BF16)


def _block_diag_out(c):
    blk = c.reshape(N_SLABS, SLAB_GROUPS, S5_GROUP, S5_STATE)
    eye = jnp.eye(SLAB_GROUPS, dtype=F32)
    w = jnp.einsum("sghp,gk->sgpkh", blk, eye)
    return w.reshape(N_SLABS, SLAB_STATE, SLAB_IN).astype(BF16)


def _cmul_add(xr, xi, pr, pi, vr, vi):
    return xr + (pr * vr - pi * vi), xi + (pr * vi + pi * vr)


def _s5_weight_specs(tab_shape, k_in):
    const3 = lambda *_: (0, 0, 0)
    const2 = lambda *_: (0, 0)
    tab_map = const3 if len(tab_shape) == 3 else const2
    return [
        pl.BlockSpec(tab_shape, tab_map),
        pl.BlockSpec(tab_shape, tab_map),
        pl.BlockSpec((N_SLABS, k_in, SLAB_STATE), const3),
        pl.BlockSpec((N_SLABS, k_in, SLAB_STATE), const3),
        pl.BlockSpec((N_SLABS, SLAB_STATE, SLAB_IN), const3),
        pl.BlockSpec((N_SLABS, SLAB_STATE, SLAB_IN), const3),
        pl.BlockSpec((1, S5_WIDTH), const2),
        pl.BlockSpec((S5_WIDTH, S5_WIDTH), const2),
        pl.BlockSpec((1, S5_WIDTH), const2),
    ]


def _s5_step_kernel(u_ref, x0r_ref, x0i_ref, tabr_ref, tabi_ref, wbr_ref, wbi_ref, cdr_ref, cdi_ref,
                    d_ref, wg_ref, bg_ref, o_ref, xlr_ref, xli_ref, xr_s, xi_s, y_s):
    for s in range(N_SLABS):
        ub = u_ref[:, s * SLAB_IN:(s + 1) * SLAB_IN]
        cs = slice(s * SLAB_STATE, (s + 1) * SLAB_STATE)
        xr_s[:, cs] = jnp.dot(ub, wbr_ref[s], preferred_element_type=F32)
        xi_s[:, cs] = jnp.dot(ub, wbi_ref[s], preferred_element_type=F32)

    def block(b, carry):
        r0 = pl.multiple_of(b * SUBLANES, SUBLANES)
        for s in range(N_SLABS):
            cs = slice(s * SLAB_STATE, (s + 1) * SLAB_STATE)
            xr = xr_s[pl.ds(r0, SUBLANES), cs]
            xi = xi_s[pl.ds(r0, SUBLANES), cs]
            for k, sh in enumerate((1, 2, 4)):
                xr, xi = _cmul_add(xr, xi, tabr_ref[k, :, cs], tabi_ref[k, :, cs],
                                   pltpu.roll(xr, sh, 0), pltpu.roll(xi, sh, 0))
            cr = jnp.broadcast_to(x0r_ref[pl.ds(b, 1), cs], (SUBLANES, SLAB_STATE))
            ci = jnp.broadcast_to(x0i_ref[pl.ds(b, 1), cs], (SUBLANES, SLAB_STATE))
            xr, xi = _cmul_add(xr, xi, tabr_ref[3, :, cs], tabi_ref[3, :, cs], cr, ci)
            xr_s[pl.ds(r0, SUBLANES), cs] = xr
            xi_s[pl.ds(r0, SUBLANES), cs] = xi
            xlr_ref[pl.ds(b, 1), cs] = xr[SUBLANES - 1:SUBLANES, :]
            xli_ref[pl.ds(b, 1), cs] = xi[SUBLANES - 1:SUBLANES, :]
        return carry

    lax.fori_loop(0, u_ref.shape[0] // SUBLANES, block, 0)

    for s in range(N_SLABS):
        cs = slice(s * SLAB_STATE, (s + 1) * SLAB_STATE)
        us = slice(s * SLAB_IN, (s + 1) * SLAB_IN)
        y = (jnp.dot(xr_s[:, cs].astype(BF16), cdr_ref[s], preferred_element_type=F32)
             - jnp.dot(xi_s[:, cs].astype(BF16), cdi_ref[s], preferred_element_type=F32))
        y = y + d_ref[:, us] * u_ref[:, us].astype(F32)
        y_s[:, us] = jax.nn.gelu(y)

    y = y_s[...]
    z = jnp.dot(y.astype(BF16), wg_ref[...], preferred_element_type=F32) + bg_ref[...]
    o_ref[...] = (y * jax.nn.sigmoid(z)).astype(o_ref.dtype)


def _s5_step(proj, x0, tabs, wb, cd, d, w_glu, b_glu, tc=256):
    t = proj.shape[0]
    seqs = tc // SUBLANES
    row = lambda i: (i, 0)
    st = jax.ShapeDtypeStruct((t // SUBLANES, N_STATE), F32)
    return pl.pallas_call(
        _s5_step_kernel,
        grid=(t // tc,),
        in_specs=[pl.BlockSpec((tc, S5_WIDTH), row), pl.BlockSpec((seqs, N_STATE), row),
                  pl.BlockSpec((seqs, N_STATE), row)]
        + _s5_weight_specs((4, SUBLANES, N_STATE), SLAB_IN),
        out_specs=(pl.BlockSpec((tc, S5_WIDTH), row), pl.BlockSpec((seqs, N_STATE), row),
                   pl.BlockSpec((seqs, N_STATE), row)),
        out_shape=(jax.ShapeDtypeStruct((t, S5_WIDTH), BF16), st, st),
        scratch_shapes=[pltpu.VMEM((tc, N_STATE), F32), pltpu.VMEM((tc, N_STATE), F32),
                        pltpu.VMEM((tc, S5_WIDTH), F32)],
        compiler_params=_params("arbitrary"),
        name="s5_step",
    )(proj, x0[0], x0[1], tabs[0], tabs[1], wb[0], wb[1], cd[0], cd[1], d, w_glu, b_glu)


def _s5_seq_kernel(u_ref, pairr_ref, pairi_ref, wbr_ref, wbi_ref, cdr_ref, cdi_ref, d_ref, wg_ref,
                   bg_ref, o_ref, xlr_ref, xli_ref, xr_s, xi_s, y_s, o_s, cbr_s, cbi_s):
    n_rows = u_ref.shape[1]
    half = SUBLANES // 2
    tt = n_rows // half

    @pl.when(pl.program_id(0) == 0)
    def _():
        cbr_s[...] = jnp.zeros_like(cbr_s)
        cbi_s[...] = jnp.zeros_like(cbi_s)

    first = lax.broadcasted_iota(jnp.int32, (SUBLANES, SLAB_STATE), 0) < half
    second_step = (lax.broadcasted_iota(jnp.int32, (n_rows, SLAB_IN), 0) & half) != 0
    for s in range(N_SLABS):
        cs = slice(s * SLAB_STATE, (s + 1) * SLAB_STATE)
        us = slice(s * SLAB_IN, (s + 1) * SLAB_IN)
        u = u_ref[s]
        u_prev = jnp.where(second_step, pltpu.roll(u, half, 0), 0.0)
        ub = jnp.concatenate([u, u_prev], axis=1).astype(BF16)
        xr_s[:, cs] = jnp.dot(ub, wbr_ref[s], preferred_element_type=F32)
        xi_s[:, cs] = jnp.dot(ub, wbi_ref[s], preferred_element_type=F32)

        pr, pi = pairr_ref[:, cs], pairi_ref[:, cs]
        cr, ci = cbr_s[:, cs], cbi_s[:, cs]
        for b in range(n_rows // SUBLANES):
            rows = slice(b * SUBLANES, (b + 1) * SUBLANES)
            xr, xi = _cmul_add(xr_s[rows, cs], xi_s[rows, cs], pr, pi, cr, ci)
            xr_s[rows, cs] = xr
            xi_s[rows, cs] = xi
            cr = jnp.where(first, pltpu.roll(xr, half, 0), xr)
            ci = jnp.where(first, pltpu.roll(xi, half, 0), xi)
        cbr_s[:, cs] = cr
        cbi_s[:, cs] = ci

        y = (jnp.dot(xr_s[:, cs].astype(BF16), cdr_ref[s], preferred_element_type=F32)
             - jnp.dot(xi_s[:, cs].astype(BF16), cdi_ref[s], preferred_element_type=F32))
        y_s[:, us] = jax.nn.gelu(y + d_ref[:, us] * u)

    xlr_ref[...] = cbr_s[...]
    xli_ref[...] = cbi_s[...]

    y = y_s[...]
    z = jnp.dot(y.astype(BF16), wg_ref[...], preferred_element_type=F32) + bg_ref[...]
    o = y * jax.nn.sigmoid(z)
    for s in range(N_SLABS):
        o_s[s] = o[:, s * SLAB_IN:(s + 1) * SLAB_IN]
    for n in range(half):
        for s in range(N_SLABS):
            o_ref[n, :, s * SLAB_IN:(s + 1) * SLAB_IN] = (
                o_s[s, pl.ds(n, tt, stride=half), :].astype(o_ref.dtype))


def _s5_seq(u_t, pairs, wb, cd, d, w_glu, b_glu, n_seq, seq_len, tt=64):
    assert 2 * n_seq == SUBLANES
    n_rows = n_seq * tt
    st = jax.ShapeDtypeStruct((SUBLANES, N_STATE), F32)
    y, xlr, xli = pl.pallas_call(
        _s5_seq_kernel,
        grid=(seq_len // tt,),
        in_specs=[pl.BlockSpec((N_SLABS, n_rows, SLAB_IN), lambda i: (0, i, 0))]
        + _s5_weight_specs((SUBLANES, N_STATE), 2 * SLAB_IN),
        out_specs=(pl.BlockSpec((n_seq, tt, S5_WIDTH), lambda i: (0, i, 0)),
                   pl.BlockSpec((SUBLANES, N_STATE), lambda i: (0, 0)),
                   pl.BlockSpec((SUBLANES, N_STATE), lambda i: (0, 0))),
        out_shape=(jax.ShapeDtypeStruct((n_seq, seq_len, S5_WIDTH), BF16), st, st),
        scratch_shapes=[pltpu.VMEM((n_rows, N_STATE), F32), pltpu.VMEM((n_rows, N_STATE), F32),
                        pltpu.VMEM((n_rows, S5_WIDTH), F32),
                        pltpu.VMEM((N_SLABS, n_rows, SLAB_IN), F32),
                        pltpu.VMEM((SUBLANES, N_STATE), F32), pltpu.VMEM((SUBLANES, N_STATE), F32)],
        compiler_params=_params("arbitrary"),
        name="s5_seq",
    )(u_t, pairs[0], pairs[1], wb[0], wb[1], cd[0], cd[1], d, w_glu, b_glu)
    return y.reshape(n_seq * seq_len, S5_WIDTH), xlr[:n_seq], xli[:n_seq]


_NT = (((1,), (1,)), ((), ()))
_TN = (((0,), (0,)), ((), ()))


def _rotary(x, cos2, sin2):
    return x * cos2 + pltpu.roll(x, HEAD_DIM // 2, 1) * sin2


def _norm_gate(o, g, gn_w):
    mu = jnp.mean(o, axis=-1, keepdims=True)
    oc = o - mu
    var = jnp.mean(oc * oc, axis=-1, keepdims=True)
    return jax.nn.silu(g) * (oc * lax.rsqrt(var + NORM_EPS) * gn_w)


def _ret_seq_kernel(q_ref, k_ref, v_ref, g_ref, cos_ref, sin_ref, mask_ref, qd_ref, kd_ref, cd_ref,
                    gnw_ref, o_ref, rn_ref, r_s, *, chunks):
    @pl.when(pl.program_id(1) == 0)
    def _():
        r_s[...] = jnp.zeros_like(r_s)

    scale = HEAD_DIM ** -0.5
    for c in range(chunks):
        rows = slice(c * RET_TILE, (c + 1) * RET_TILE)
        cos = cos_ref[rows, :]
        sin = sin_ref[rows, :]
        for h in range(RET_HEADS):
            hs = slice(h * HEAD_DIM, (h + 1) * HEAD_DIM)
            qr = _rotary(q_ref[rows, hs].astype(F32), cos, sin)
            kr = _rotary(k_ref[rows, hs].astype(F32), cos, sin) * scale
            vb = v_ref[rows, hs]
            sc = lax.dot_general(qr.astype(BF16), kr.astype(BF16), _NT,
                                 preferred_element_type=F32) * mask_ref[h]
            r_old = r_s[h]
            lhs = jnp.concatenate([sc.astype(BF16), (qr * qd_ref[h]).astype(BF16)], axis=1)
            rhs = jnp.concatenate([vb, r_old.astype(BF16)], axis=0)
            o = jnp.dot(lhs, rhs, preferred_element_type=F32)
            r_s[h] = r_old * cd_ref[h] + lax.dot_general(
                (kr * kd_ref[h]).astype(BF16), vb, _TN, preferred_element_type=F32)
            o_ref[rows, hs] = _norm_gate(o, g_ref[rows, hs].astype(F32),
                                         gnw_ref[:, hs]).astype(o_ref.dtype)
    rn_ref[0] = r_s[...]


def _ret_step_kernel(q_ref, k_ref, v_ref, g_ref, cos_ref, sin_ref, mask_ref, qd_ref, kd_ref, cd_ref,
                     gnw_ref, r0_ref, o_ref, rn_ref, qr_s, kdt_s, ob_s, *, n_seq):
    rows_per_seq = RET_TILE // n_seq
    cos = cos_ref[...]
    sin = sin_ref[...]
    scale = HEAD_DIM ** -0.5
    for h in range(RET_HEADS):
        hs = slice(h * HEAD_DIM, (h + 1) * HEAD_DIM)
        qr = _rotary(q_ref[:, hs].astype(F32), cos, sin)
        kr = _rotary(k_ref[:, hs].astype(F32), cos, sin) * scale
        sc = lax.dot_general(qr.astype(BF16), kr.astype(BF16), _NT,
                             preferred_element_type=F32) * mask_ref[h]
        ob_s[:, hs] = jnp.dot(sc.astype(BF16), v_ref[:, hs], preferred_element_type=F32)
        qr_s[:, hs] = qr
        kdt_s[h] = (kr * kd_ref[h]).T

    lane = lax.broadcasted_iota(jnp.int32, (HEAD_DIM, RET_TILE), 1)

    def per_seq(s, c):
        r0 = pl.multiple_of(s * rows_per_seq, rows_per_seq)
        rows = pl.ds(r0, rows_per_seq)
        in_seq = (lane >= r0) & (lane < r0 + rows_per_seq)
        for h in range(RET_HEADS):
            hs = slice(h * HEAD_DIM, (h + 1) * HEAD_DIM)
            r_old = r0_ref[s, h]
            cross = jnp.dot(qr_s[rows, hs].astype(BF16), r_old.astype(BF16), preferred_element_type=F32)
            ob_s[rows, hs] = ob_s[rows, hs] + cross * qd_ref[h, rows, :]
            kdt = jnp.where(in_seq, kdt_s[h], 0.0).astype(BF16)
            rn_ref[s, h] = r_old * cd_ref[h] + jnp.dot(kdt, v_ref[:, hs], preferred_element_type=F32)
        return c

    lax.fori_loop(0, n_seq, per_seq, 0)

    for h in range(RET_HEADS):
        hs = slice(h * HEAD_DIM, (h + 1) * HEAD_DIM)
        o_ref[:, hs] = _norm_gate(ob_s[:, hs], g_ref[:, hs].astype(F32),
                                  gnw_ref[:, hs]).astype(o_ref.dtype)


def _rotary_tables(pos0, rows, reps):
    half = HEAD_DIM // 2
    inv_freq = ROPE_BASE ** (-jnp.arange(half, dtype=F32) / half)
    pos = pos0 + jnp.arange(rows, dtype=F32)
    ang = pos[:, None] * inv_freq[None, :]
    cos, sin = jnp.cos(ang), jnp.sin(ang)
    cos2 = jnp.concatenate([cos, cos], axis=-1)
    sin2 = jnp.concatenate([-sin, sin], axis=-1)
    return jnp.tile(cos2, (reps, 1)), jnp.tile(sin2, (reps, 1))


def _decay_tables(chunk, n_seq):
    log_gamma = jnp.log(1.0 - 2.0 ** (-5.0 - jnp.arange(RET_HEADS, dtype=F32)))
    idx = jnp.arange(chunk, dtype=F32)
    diff = idx[:, None] - idx[None, :]
    mask = jnp.where(diff >= 0, jnp.exp(log_gamma[:, None, None] * jnp.maximum(diff, 0.0)), 0.0)
    q_decay = jnp.exp(log_gamma[:, None] * (idx + 1.0))
    k_decay = jnp.exp(log_gamma[:, None] * (chunk - 1.0 - idx))
    chunk_decay = jnp.exp(log_gamma * chunk)
    eye = jnp.eye(n_seq, dtype=F32)
    mask_t = jnp.einsum("hab,st->hsatb", mask, eye).reshape(RET_HEADS, RET_TILE, RET_TILE)
    qd_t = jnp.broadcast_to(jnp.tile(q_decay, (1, n_seq))[:, :, None], (RET_HEADS, RET_TILE, HEAD_DIM))
    kd_t = jnp.broadcast_to(jnp.tile(k_decay, (1, n_seq))[:, :, None], (RET_HEADS, RET_TILE, HEAD_DIM))
    return mask_t, qd_t, kd_t, chunk_decay


def _retention(proj, r0, gn_w, n_seq, seq_len, pos0, chunks=4):
    t = proj.shape[0]
    const3 = lambda *_: (0, 0, 0)
    if r0 is None:
        tile, tile_seqs = chunks * RET_TILE, 1
        steps = seq_len // tile
        cos2, sin2 = _rotary_tables(pos0, seq_len, 1)
        grid = (n_seq, steps)
        row = lambda n, c: n * steps + c
        tab_map = lambda n, c: (c, 0)
        state_map = lambda n, c: (n, 0, 0, 0)
        state_block = (1, RET_HEADS, HEAD_DIM, HEAD_DIM)
        sem = ("arbitrary", "arbitrary")
        body = functools.partial(_ret_seq_kernel, chunks=chunks)
        scratch = [pltpu.VMEM((RET_HEADS, HEAD_DIM, HEAD_DIM), F32)]
        name = "ret_seq"
    else:
        tile, tile_seqs = RET_TILE, RET_TILE // seq_len
        cos2, sin2 = _rotary_tables(pos0, seq_len, tile_seqs)
        grid = (t // tile,)
        row = lambda i: i
        tab_map = lambda i: (0, 0)
        state_map = lambda i: (i, 0, 0, 0)
        state_block = (tile_seqs, RET_HEADS, HEAD_DIM, HEAD_DIM)
        sem = ("arbitrary",)
        body = functools.partial(_ret_step_kernel, n_seq=tile_seqs)
        scratch = [pltpu.VMEM((RET_TILE, RET_WIDTH), F32),
                   pltpu.VMEM((RET_HEADS, HEAD_DIM, RET_TILE), F32),
                   pltpu.VMEM((RET_TILE, RET_WIDTH), F32)]
        name = "ret_step"
    mask_t, qd_t, kd_t, cd = _decay_tables(RET_TILE // tile_seqs, tile_seqs)

    def col(cb):
        return pl.BlockSpec((tile, RET_WIDTH), lambda *a: (row(*a), cb))

    in_specs = [
        col(1), col(2), col(3), col(4),
        pl.BlockSpec((tile, HEAD_DIM), tab_map),
        pl.BlockSpec((tile, HEAD_DIM), tab_map),
        pl.BlockSpec((RET_HEADS, RET_TILE, RET_TILE), const3),
        pl.BlockSpec((RET_HEADS, RET_TILE, HEAD_DIM), const3),
        pl.BlockSpec((RET_HEADS, RET_TILE, HEAD_DIM), const3),
        pl.BlockSpec(memory_space=pltpu.SMEM),
        pl.BlockSpec((1, RET_WIDTH), lambda *_: (0, 0)),
    ]
    args = [proj, proj, proj, proj, cos2, sin2, mask_t, qd_t, kd_t, cd, gn_w]
    if r0 is not None:
        in_specs.append(pl.BlockSpec(state_block, state_map))
        args.append(r0)
    return pl.pallas_call(
        body,
        grid=grid,
        in_specs=in_specs,
        out_specs=(pl.BlockSpec((tile, RET_WIDTH), lambda *a: (row(*a), 0)),
                   pl.BlockSpec(state_block, state_map)),
        out_shape=(jax.ShapeDtypeStruct((t, RET_WIDTH), BF16),
                   jax.ShapeDtypeStruct((n_seq, RET_HEADS, HEAD_DIM, HEAD_DIM), F32)),
        scratch_shapes=scratch,
        compiler_params=_params(*sem),
        name=name,
    )(*args)


def _outproj_kernel(x_ref, a_ref, b_ref, wa_ref, wb_ref, o_ref):
    o_ref[...] = (x_ref[...]
                  + jnp.dot(a_ref[...], wa_ref[...], preferred_element_type=F32)
                  + jnp.dot(b_ref[...], wb_ref[...], preferred_element_type=F32))


def _outproj(x2d, a, b, w, tm=512):
    t = x2d.shape[0]
    resident = pl.Buffered(1)
    return pl.pallas_call(
        _outproj_kernel,
        grid=(t // tm,),
        in_specs=[
            pl.BlockSpec((tm, D_MODEL), lambda i: (i, 0)),
            pl.BlockSpec((tm, S5_WIDTH), lambda i: (i, 0)),
            pl.BlockSpec((tm, RET_WIDTH), lambda i: (i, 0)),
            pl.BlockSpec((S5_WIDTH, D_MODEL), lambda i: (0, 0), pipeline_mode=resident),
            pl.BlockSpec((RET_WIDTH, D_MODEL), lambda i: (1, 0), pipeline_mode=resident),
        ],
        out_specs=pl.BlockSpec((tm, D_MODEL), lambda i: (i, 0)),
        out_shape=jax.ShapeDtypeStruct((t, D_MODEL), F32),
        compiler_params=_params("arbitrary"),
        name="out_proj",
    )(x2d, a, b, w, w)


def _ffn_kernel(x_ref, gn_ref, wg_ref, wu_ref, wo_ref, gf_ref, o_ref, *rest, cast):
    if cast:
        wgb_ref, wub_ref, wob_ref, h_scr = rest
        wgb_ref[...] = wg_ref[...].astype(BF16)
        wub_ref[...] = wu_ref[...].astype(BF16)
        wob_ref[...] = wo_ref[...].astype(BF16)
        wg_ref, wu_ref, wo_ref = wgb_ref, wub_ref, wob_ref
    else:
        (h_scr,) = rest
    j = pl.program_id(1)

    @pl.when(j == 0)
    def _():
        h_scr[...] = _rms(x_ref[...], gn_ref[...]).astype(BF16)
        o_ref[...] = x_ref[...]

    for r in range(x_ref.shape[0] // FFN_ROWS):
        rows = slice(r * FFN_ROWS, (r + 1) * FFN_ROWS)
        h = h_scr[rows, :]
        gate = jnp.dot(h, wg_ref[...], preferred_element_type=F32)
        up = jnp.dot(h, wu_ref[...], preferred_element_type=F32)
        act = (jax.nn.silu(gate) * up).astype(BF16)
        o_ref[rows, :] += jnp.dot(act, wo_ref[...], preferred_element_type=F32)

    @pl.when(j == pl.num_programs(1) - 1)
    def _():
        o_ref[...] = _rms(o_ref[...], gf_ref[...])


def _ffn(x2d, g_ffn, w_gate, w_up, w_down, g_final, tm=1024, tf=512):
    t = x2d.shape[0]
    nf = D_FF // tf
    cast = w_down.dtype == F32
    (wg, g0), (wu, u0) = w_gate, w_up
    g0, u0 = g0 * nf, u0 * nf
    x_mode = {}
    out_specs = [pl.BlockSpec((tm, D_MODEL), lambda i, j: (i, 0))]
    out_shape = [jax.ShapeDtypeStruct((t, D_MODEL), F32)]
    if cast:
        assert t == tm, "the bf16 weight outputs are written once per row tile"
        x_mode = dict(pipeline_mode=pl.Buffered(1))
        out_specs += [pl.BlockSpec((D_MODEL, tf), lambda i, j: (0, j)),
                      pl.BlockSpec((D_MODEL, tf), lambda i, j: (0, j)),
                      pl.BlockSpec((tf, D_MODEL), lambda i, j: (j, 0))]
        out_shape += [jax.ShapeDtypeStruct((D_MODEL, D_FF), BF16),
                      jax.ShapeDtypeStruct((D_MODEL, D_FF), BF16),
                      jax.ShapeDtypeStruct((D_FF, D_MODEL), BF16)]
    return pl.pallas_call(
        functools.partial(_ffn_kernel, cast=cast),
        grid=(t // tm, nf),
        in_specs=[
            pl.BlockSpec((tm, D_MODEL), lambda i, j: (i, 0), **x_mode),
            pl.BlockSpec((1, D_MODEL), lambda i, j: (0, 0)),
            pl.BlockSpec((D_MODEL, tf), lambda i, j: (0, j + g0)),
            pl.BlockSpec((D_MODEL, tf), lambda i, j: (0, j + u0)),
            pl.BlockSpec((tf, D_MODEL), lambda i, j: (j, 0)),
            pl.BlockSpec((1, D_MODEL), lambda i, j: (0, 0)),
        ],
        out_specs=out_specs,
        out_shape=out_shape,
        scratch_shapes=[pltpu.VMEM((tm, D_MODEL), BF16)],
        compiler_params=_params("arbitrary", "arbitrary"),
        name="ffn_cast" if cast else "ffn",
    )(x2d, g_ffn, wg, wu, w_down, g_final)


def _finish(x2d, proj, s5, ret_state, pos0, w, ffn_w, ffn_tf, n, l):
    s5_out, s5_re, s5_im = s5
    ret_out, ret_new = _retention(proj, ret_state, w["gn_w"], n, l, pos0)
    x1 = _outproj(x2d, s5_out, ret_out, w["w_out"])
    y, *ffn_bf16 = _ffn(x1, w["norm_ffn"], *ffn_w, w["norm_final"], tf=ffn_tf)
    return (y.reshape(n, l, D_MODEL), s5_re.reshape(n, S5_GROUPS, S5_STATE),
            s5_im.reshape(n, S5_GROUPS, S5_STATE), ret_new), ffn_bf16


def _sample_layer(x, s5_state, ret_state, w, w_in_f32, ffn_f32):
    n, l, _ = x.shape
    x2d = x.reshape(n * l, D_MODEL)
    proj, w_in_b = _inproj_cast(x2d, w["norm_mix"], w_in_f32)
    s5 = _s5_step(proj, s5_state, w["tabs"], w["wb"], w["cd"], w["d"], w["w_glu"], w["b_glu"])
    outs, ffn_b = _finish(x2d, proj, s5, ret_state, float(PAST_LEN), w, ffn_f32, 256, n, l)
    return outs, w_in_b, ffn_b


def _prompt_layer(x, w, w_in_b, ffn_b):
    n, l, _ = x.shape
    x2d = x.reshape(n * l, D_MODEL)
    proj, u_t = _inproj_seq(x2d, w["norm_mix"], w_in_b, n)
    s5 = _s5_seq(u_t, w["pairs"], w["wb2"], w["cd"], w["d"], w["w_glu"], w["b_glu"], n, l)
    outs, _ = _finish(x2d, proj, s5, None, 0.0, w, ffn_b, 512, n, l)
    return outs


def kernel(x_prompt, x_sample, state_s5_re, state_s5_im, state_ret, norm_mix, w_in, s5_lambda_re, s5_lambda_im, s5_log_step, s5_b_re, s5_b_im, s5_c_re, s5_c_im, s5_d, s5_w_glu, s5_b_glu, ret_gn_w, w_out, norm_ffn, w_ffn_in, w_ffn_out, norm_final):
    assert norm_mix.shape[0] == 1, "single-layer stack"
    tabr, tabi, pairr, pairi, bbr, bbi, abr, abi = _s5_params(
        s5_lambda_re[0], s5_lambda_im[0], s5_log_step[0], s5_b_re[0], s5_b_im[0])
    wb = (_block_diag_in(bbr), _block_diag_in(bbi))
    w = dict(
        norm_mix=norm_mix,
        tabs=(tabr, tabi), pairs=(pairr, pairi), wb=wb,
        wb2=(jnp.concatenate([wb[0], _block_diag_in(abr)], axis=1),
             jnp.concatenate([wb[1], _block_diag_in(abi)], axis=1)),
        cd=(_block_diag_out(s5_c_re[0]), _block_diag_out(s5_c_im[0])),
        d=s5_d, w_glu=s5_w_glu[0].astype(BF16), b_glu=s5_b_glu,
        gn_w=ret_gn_w, w_out=w_out[0].astype(BF16), norm_ffn=norm_ffn,
        norm_final=norm_final.reshape(1, D_MODEL),
    )
    n_s = x_sample.shape[0]
    ffn_f32 = ((w_ffn_in[0], 0), (w_ffn_in[0], 1), w_ffn_out[0])
    (ys, s_re, s_im, s_ret), w_in_b, (wg_b, wu_b, wd_b) = _sample_layer(
        x_sample,
        (state_s5_re[0].reshape(n_s, N_STATE), state_s5_im[0].reshape(n_s, N_STATE)),
        state_ret[0], w, w_in[0], ffn_f32)
    yp, p_re, p_im, p_ret = _prompt_layer(x_prompt, w, w_in_b, ((wg_b, 0), (wu_b, 0), wd_b))
    return (yp, ys, p_re[None], p_im[None], p_ret[None], s_re[None], s_im[None], s_ret[None])
```

```python
import functools
import math

import jax
import jax.numpy as jnp
import numpy as np
from jax import lax
from jax.experimental import pallas as pl
from jax.experimental.pallas import tpu as pltpu

F32 = jnp.float32
BF16 = jnp.bfloat16

D_MODEL = 2048
S5_WIDTH = 1024
S5_GROUP = 16
S5_GROUPS = 64
S5_STATE = 64
N_STATE = S5_GROUPS * S5_STATE
RET_WIDTH = 1024
RET_HEADS = 8
HEAD_DIM = 128
ROPE_BASE = 10000.0
D_FF = 5632
IN_WIDTH = S5_WIDTH + 4 * RET_WIDTH
NORM_EPS = 1e-6
PAST_LEN = 16384

SUBLANES = 8
SLAB_GROUPS = 8
N_SLABS = S5_GROUPS // SLAB_GROUPS
SLAB_IN = SLAB_GROUPS * S5_GROUP
SLAB_STATE = SLAB_GROUPS * S5_STATE
RET_TILE = 128
FFN_ROWS = 512

VMEM_LIMIT_BYTES = 60 * 1024 * 1024


def _params(*sem):
    return pltpu.CompilerParams(dimension_semantics=sem, vmem_limit_bytes=VMEM_LIMIT_BYTES)


def _rms(xf, g):
    ms = jnp.mean(xf * xf, axis=-1, keepdims=True)
    return xf * lax.rsqrt(ms + NORM_EPS) * g


def _inproj_seq_kernel(x_ref, g_ref, w_ref, o_ref, ou_ref, h_scr, *, tn, n_seq):
    seq = pl.program_id(1)
    tm = x_ref.shape[0]
    h_scr[...] = _rms(x_ref[...], g_ref[...]).astype(BF16)
    for j in range(IN_WIDTH // tn):
        cols = slice(j * tn, (j + 1) * tn)
        res = jnp.dot(h_scr[...], w_ref[:, cols], preferred_element_type=F32)
        if j * tn < S5_WIDTH:
            dst = pl.ds(seq, tm, stride=n_seq)
            for s in range(tn // SLAB_IN):
                ou_ref[j * (tn // SLAB_IN) + s, dst, :] = res[:, s * SLAB_IN:(s + 1) * SLAB_IN]
        o_ref[:, cols] = res.astype(o_ref.dtype)


def _inproj_seq(x2d, g, w, n_seq, tm=256, tn=1024):
    t = x2d.shape[0]
    tiles = t // n_seq // tm
    row = lambda i, n: (n * tiles + i, 0)
    const = lambda i, n: (0, 0)
    return pl.pallas_call(
        functools.partial(_inproj_seq_kernel, tn=tn, n_seq=n_seq),
        grid=(tiles, n_seq),
        in_specs=[
            pl.BlockSpec((tm, D_MODEL), row),
            pl.BlockSpec((1, D_MODEL), const),
            pl.BlockSpec((D_MODEL, IN_WIDTH), const, pipeline_mode=pl.Buffered(1)),
        ],
        out_specs=[pl.BlockSpec((tm, IN_WIDTH), row),
                   pl.BlockSpec((N_SLABS, n_seq * tm, SLAB_IN), lambda i, n: (0, i, 0))],
        out_shape=[jax.ShapeDtypeStruct((t, IN_WIDTH), BF16),
                   jax.ShapeDtypeStruct((N_SLABS, t, SLAB_IN), F32)],
        scratch_shapes=[pltpu.VMEM((tm, D_MODEL), BF16)],
        compiler_params=_params("arbitrary", "arbitrary"),
        name="in_proj_seq",
    )(x2d, g, w)


def _inproj_cast_kernel(x_ref, g_ref, w_ref, o_ref, wb_ref, h_scr):
    @pl.when(pl.program_id(0) == 0)
    def _():
        h_scr[...] = _rms(x_ref[...], g_ref[...]).astype(BF16)

    wb_ref[...] = w_ref[...].astype(BF16)
    o_ref[...] = jnp.dot(h_scr[...], wb_ref[...], preferred_element_type=F32).astype(o_ref.dtype)


def _inproj_cast(x2d, g, w_f32, tn=1024):
    t = x2d.shape[0]
    return pl.pallas_call(
        _inproj_cast_kernel,
        grid=(IN_WIDTH // tn,),
        in_specs=[
            pl.BlockSpec((t, D_MODEL), lambda j: (0, 0), pipeline_mode=pl.Buffered(1)),
            pl.BlockSpec((1, D_MODEL), lambda j: (0, 0)),
            pl.BlockSpec((D_MODEL, tn), lambda j: (0, j)),
        ],
        out_specs=[pl.BlockSpec((t, tn), lambda j: (0, j)),
                   pl.BlockSpec((D_MODEL, tn), lambda j: (0, j))],
        out_shape=[jax.ShapeDtypeStruct((t, IN_WIDTH), BF16),
                   jax.ShapeDtypeStruct((D_MODEL, IN_WIDTH), BF16)],
        scratch_shapes=[pltpu.VMEM((t, D_MODEL), BF16)],
        compiler_params=_params("arbitrary"),
        name="in_proj_cast",
    )(x2d, g, w_f32)


def _s5_param_kernel(lr_ref, li_ref, ls_ref, br_ref, bi_ref, ctr_ref, cti_ref,
                     tabr_ref, tabi_ref, pairr_ref, pairi_ref, wbr_ref, wbi_ref, cdr_ref, cdi_ref):
    lr = lr_ref[...]
    li = li_ref[...]
    dt = jnp.exp(ls_ref[...])
    mag = jnp.exp(lr * dt)
    ar = mag * jnp.cos(li * dt)
    ai = mag * jnp.sin(li * dt)
    den = lr * lr + li * li
    nr = ar - 1.0
    cr = (nr * lr + ai * li) / den
    ci = (ai * lr - nr * li) / den
    br = br_ref[...]
    bi = bi_ref[...]
    bbr = cr * br - ci * bi
    bbi = cr * bi + ci * br
    abr = ar * bbr - ai * bbi
    abi = ar * bbi + ai * bbr

    in_shape = (SLAB_IN, SLAB_STATE)
    same_in = (lax.broadcasted_iota(jnp.int32, in_shape, 0) // S5_GROUP
               == lax.broadcasted_iota(jnp.int32, in_shape, 1) // S5_STATE)
    out_shape = (SLAB_STATE, SLAB_IN)
    same_out = (lax.broadcasted_iota(jnp.int32, out_shape, 0) // S5_STATE
                == lax.broadcasted_iota(jnp.int32, out_shape, 1) // S5_GROUP)
    for s in range(N_SLABS):
        cs = slice(s * SLAB_STATE, (s + 1) * SLAB_STATE)
        for dst, top, bot in ((wbr_ref, bbr, abr), (wbi_ref, bbi, abi)):
            for k, part in enumerate((top, bot)):
                blk = jnp.tile(part[:, cs], (SLAB_GROUPS, 1))
                dst[s, k * SLAB_IN:(k + 1) * SLAB_IN, :] = jnp.where(same_in, blk, 0.0).astype(BF16)
        cdr_ref[s] = jnp.where(same_out, ctr_ref[cs, :], 0.0).astype(BF16)
        cdi_ref[s] = jnp.where(same_out, cti_ref[cs, :], 0.0).astype(BF16)

    rows = lax.broadcasted_iota(jnp.int32, (SUBLANES, N_STATE), 0)
    second = rows >= SUBLANES // 2
    full = lambda v: jnp.broadcast_to(v, (SUBLANES, N_STATE))
    pairr_ref[...] = jnp.where(second, full(ar * ar - ai * ai), full(ar))
    pairi_ref[...] = jnp.where(second, full(ar * ai + ai * ar), full(ai))

    pr, pi = ar, ai
    for r in range(SUBLANES):
        tabr_ref[3, r:r + 1, :] = pr
        tabi_ref[3, r:r + 1, :] = pi
        step = r + 1
        if step in (1, 2, 4):
            k = (1, 2, 4).index(step)
            keep = rows >= step
            tabr_ref[k] = jnp.where(keep, jnp.broadcast_to(pr, (SUBLANES, N_STATE)), 0.0)
            tabi_ref[k] = jnp.where(keep, jnp.broadcast_to(pi, (SUBLANES, N_STATE)), 0.0)
        pr, pi = pr * ar - pi * ai, pr * ai + pi * ar


def _s5_params(lam_re, lam_im, log_step, b_re, b_im, c_re, c_im):
    lr = lam_re.reshape(1, N_STATE)
    li = lam_im.reshape(1, N_STATE)
    ls = jnp.repeat(log_step, S5_STATE).reshape(1, N_STATE)
    b_rows = lambda b: jnp.transpose(b, (2, 0, 1)).reshape(S5_GROUP, N_STATE)
    c_cols = lambda c: jnp.tile(jnp.transpose(c, (0, 2, 1)).reshape(N_STATE, S5_GROUP),
                                (1, SLAB_GROUPS))
    tab = jax.ShapeDtypeStruct((4, SUBLANES, N_STATE), F32)
    pair = jax.ShapeDtypeStruct((SUBLANES, N_STATE), F32)
    wb = jax.ShapeDtypeStruct((N_SLABS, 2 * SLAB_IN, SLAB_STATE), BF16)
    cd = jax.ShapeDtypeStruct((N_SLABS, SLAB_STATE, SLAB_IN), BF16)
    tabr, tabi, pairr, pairi, wbr, wbi, cdr, cdi = pl.pallas_call(
        _s5_param_kernel,
        out_shape=(tab, tab, pair, pair, wb, wb, cd, cd),
        name="s5_params",
    )(lr, li, ls, b_rows(b_re), b_rows(b_im), c_cols(c_re), c_cols(c_im))
    return (tabr, tabi), (pairr, pairi), (wbr, wbi), (cdr, cdi)


def _cmul_add(xr, xi, pr, pi, vr, vi):
    return xr + (pr * vr - pi * vi), xi + (pr * vi + pi * vr)


def _s5_weight_specs(tab_shape, k_in):
    const3 = lambda *_: (0, 0, 0)
    const2 = lambda *_: (0, 0)
    tab_map = const3 if len(tab_shape) == 3 else const2
    return [
        pl.BlockSpec(tab_shape, tab_map),
        pl.BlockSpec(tab_shape, tab_map),
        pl.BlockSpec((N_SLABS, k_in, SLAB_STATE), const3),
        pl.BlockSpec((N_SLABS, k_in, SLAB_STATE), const3),
        pl.BlockSpec((N_SLABS, SLAB_STATE, SLAB_IN), const3),
        pl.BlockSpec((N_SLABS, SLAB_STATE, SLAB_IN), const3),
        pl.BlockSpec((1, S5_WIDTH), const2),
        pl.BlockSpec((S5_WIDTH, S5_WIDTH), const2),
        pl.BlockSpec((1, S5_WIDTH), const2),
    ]


def _s5_step_kernel(u_ref, x0r_ref, x0i_ref, tabr_ref, tabi_ref, wbr_ref, wbi_ref, cdr_ref, cdi_ref,
                    d_ref, wg_ref, bg_ref, o_ref, xlr_ref, xli_ref, wgb_ref, xr_s, xi_s, y_s):
    @pl.when(pl.program_id(0) == 0)
    def _():
        wgb_ref[...] = wg_ref[...].astype(BF16)

    for s in range(N_SLABS):
        ub = u_ref[:, s * SLAB_IN:(s + 1) * SLAB_IN]
        cs = slice(s * SLAB_STATE, (s + 1) * SLAB_STATE)
        xr_s[:, cs] = jnp.dot(ub, wbr_ref[s], preferred_element_type=F32)
        xi_s[:, cs] = jnp.dot(ub, wbi_ref[s], preferred_element_type=F32)

    def block(b, carry):
        r0 = pl.multiple_of(b * SUBLANES, SUBLANES)
        for s in range(N_SLABS):
            cs = slice(s * SLAB_STATE, (s + 1) * SLAB_STATE)
            xr = xr_s[pl.ds(r0, SUBLANES), cs]
            xi = xi_s[pl.ds(r0, SUBLANES), cs]
            for k, sh in enumerate((1, 2, 4)):
                xr, xi = _cmul_add(xr, xi, tabr_ref[k, :, cs], tabi_ref[k, :, cs],
                                   pltpu.roll(xr, sh, 0), pltpu.roll(xi, sh, 0))
            cr = jnp.broadcast_to(x0r_ref[pl.ds(b, 1), cs], (SUBLANES, SLAB_STATE))
            ci = jnp.broadcast_to(x0i_ref[pl.ds(b, 1), cs], (SUBLANES, SLAB_STATE))
            xr, xi = _cmul_add(xr, xi, tabr_ref[3, :, cs], tabi_ref[3, :, cs], cr, ci)
            xr_s[pl.ds(r0, SUBLANES), cs] = xr
            xi_s[pl.ds(r0, SUBLANES), cs] = xi
            xlr_ref[pl.ds(b, 1), cs] = xr[SUBLANES - 1:SUBLANES, :]
            xli_ref[pl.ds(b, 1), cs] = xi[SUBLANES - 1:SUBLANES, :]
        return carry

    lax.fori_loop(0, u_ref.shape[0] // SUBLANES, block, 0)

    for s in range(N_SLABS):
        cs = slice(s * SLAB_STATE, (s + 1) * SLAB_STATE)
        us = slice(s * SLAB_IN, (s + 1) * SLAB_IN)
        y = (jnp.dot(xr_s[:, cs].astype(BF16), cdr_ref[s], preferred_element_type=F32)
             - jnp.dot(xi_s[:, cs].astype(BF16), cdi_ref[s], preferred_element_type=F32))
        y = y + d_ref[:, us] * u_ref[:, us].astype(F32)
        y_s[:, us] = jax.nn.gelu(y)

    y = y_s[...]
    z = jnp.dot(y.astype(BF16), wgb_ref[...], preferred_element_type=F32) + bg_ref[...]
    o_ref[...] = (y * jax.nn.sigmoid(z)).astype(o_ref.dtype)


def _s5_step(proj, x0, tabs, wb, cd, d, w_glu, b_glu, tc=256):
    t = proj.shape[0]
    seqs = tc // SUBLANES
    row = lambda i: (i, 0)
    st = jax.ShapeDtypeStruct((t // SUBLANES, N_STATE), F32)
    return pl.pallas_call(
        _s5_step_kernel,
        grid=(t // tc,),
        in_specs=[pl.BlockSpec((tc, S5_WIDTH), row), pl.BlockSpec((seqs, N_STATE), row),
                  pl.BlockSpec((seqs, N_STATE), row)]
        + _s5_weight_specs((4, SUBLANES, N_STATE), SLAB_IN),
        out_specs=(pl.BlockSpec((tc, S5_WIDTH), row), pl.BlockSpec((seqs, N_STATE), row),
                   pl.BlockSpec((seqs, N_STATE), row),
                   pl.BlockSpec((S5_WIDTH, S5_WIDTH), lambda i: (0, 0))),
        out_shape=(jax.ShapeDtypeStruct((t, S5_WIDTH), BF16), st, st,
                   jax.ShapeDtypeStruct((S5_WIDTH, S5_WIDTH), BF16)),
        scratch_shapes=[pltpu.VMEM((tc, N_STATE), F32), pltpu.VMEM((tc, N_STATE), F32),
                        pltpu.VMEM((tc, S5_WIDTH), F32)],
        compiler_params=_params("arbitrary"),
        name="s5_step",
    )(proj, x0[0], x0[1], tabs[0], tabs[1], wb[0], wb[1], cd[0], cd[1], d, w_glu, b_glu)


def _s5_seq_kernel(u_ref, pairr_ref, pairi_ref, wbr_ref, wbi_ref, cdr_ref, cdi_ref, d_ref, wg_ref,
                   bg_ref, o_ref, xlr_ref, xli_ref, xr_s, xi_s, y_s, o_s, cbr_s, cbi_s):
    n_rows = u_ref.shape[1]
    half = SUBLANES // 2
    tt = n_rows // half

    @pl.when(pl.program_id(0) == 0)
    def _():
        cbr_s[...] = jnp.zeros_like(cbr_s)
        cbi_s[...] = jnp.zeros_like(cbi_s)

    first = lax.broadcasted_iota(jnp.int32, (SUBLANES, SLAB_STATE), 0) < half
    second_step = (lax.broadcasted_iota(jnp.int32, (n_rows, SLAB_IN), 0) & half) != 0
    for s in range(N_SLABS):
        cs = slice(s * SLAB_STATE, (s + 1) * SLAB_STATE)
        us = slice(s * SLAB_IN, (s + 1) * SLAB_IN)
        u = u_ref[s]
        u_prev = jnp.where(second_step, pltpu.roll(u, half, 0), 0.0)
        ub = jnp.concatenate([u, u_prev], axis=1).astype(BF16)
        xr_s[:, cs] = jnp.dot(ub, wbr_ref[s], preferred_element_type=F32)
        xi_s[:, cs] = jnp.dot(ub, wbi_ref[s], preferred_element_type=F32)

        pr, pi = pairr_ref[:, cs], pairi_ref[:, cs]
        cr, ci = cbr_s[:, cs], cbi_s[:, cs]
        for b in range(n_rows // SUBLANES):
            rows = slice(b * SUBLANES, (b + 1) * SUBLANES)
            xr, xi = _cmul_add(xr_s[rows, cs], xi_s[rows, cs], pr, pi, cr, ci)
            xr_s[rows, cs] = xr
            xi_s[rows, cs] = xi
            cr = jnp.where(first, pltpu.roll(xr, half, 0), xr)
            ci = jnp.where(first, pltpu.roll(xi, half, 0), xi)
        cbr_s[:, cs] = cr
        cbi_s[:, cs] = ci

        y = (jnp.dot(xr_s[:, cs].astype(BF16), cdr_ref[s], preferred_element_type=F32)
             - jnp.dot(xi_s[:, cs].astype(BF16), cdi_ref[s], preferred_element_type=F32))
        y_s[:, us] = jax.nn.gelu(y + d_ref[:, us] * u)

    xlr_ref[...] = cbr_s[...]
    xli_ref[...] = cbi_s[...]

    y = y_s[...]
    z = jnp.dot(y.astype(BF16), wg_ref[...], preferred_element_type=F32) + bg_ref[...]
    o = y * jax.nn.sigmoid(z)
    for s in range(N_SLABS):
        o_s[s] = o[:, s * SLAB_IN:(s + 1) * SLAB_IN]
    for n in range(half):
        for s in range(N_SLABS):
            o_ref[n, :, s * SLAB_IN:(s + 1) * SLAB_IN] = (
                o_s[s, pl.ds(n, tt, stride=half), :].astype(o_ref.dtype))


def _s5_seq(u_t, pairs, wb, cd, d, w_glu, b_glu, n_seq, seq_len, tt=64):
    assert 2 * n_seq == SUBLANES
    n_rows = n_seq * tt
    st = jax.ShapeDtypeStruct((SUBLANES, N_STATE), F32)
    y, xlr, xli = pl.pallas_call(
        _s5_seq_kernel,
        grid=(seq_len // tt,),
        in_specs=[pl.BlockSpec((N_SLABS, n_rows, SLAB_IN), lambda i: (0, i, 0))]
        + _s5_weight_specs((SUBLANES, N_STATE), 2 * SLAB_IN),
        out_specs=(pl.BlockSpec((n_seq, tt, S5_WIDTH), lambda i: (0, i, 0)),
                   pl.BlockSpec((SUBLANES, N_STATE), lambda i: (0, 0)),
                   pl.BlockSpec((SUBLANES, N_STATE), lambda i: (0, 0))),
        out_shape=(jax.ShapeDtypeStruct((n_seq, seq_len, S5_WIDTH), BF16), st, st),
        scratch_shapes=[pltpu.VMEM((n_rows, N_STATE), F32), pltpu.VMEM((n_rows, N_STATE), F32),
                        pltpu.VMEM((n_rows, S5_WIDTH), F32),
                        pltpu.VMEM((N_SLABS, n_rows, SLAB_IN), F32),
                        pltpu.VMEM((SUBLANES, N_STATE), F32), pltpu.VMEM((SUBLANES, N_STATE), F32)],
        compiler_params=_params("arbitrary"),
        name="s5_seq",
    )(u_t, pairs[0], pairs[1], wb[0], wb[1], cd[0], cd[1], d, w_glu, b_glu)
    return y.reshape(n_seq * seq_len, S5_WIDTH), xlr[:n_seq], xli[:n_seq]


_NT = (((1,), (1,)), ((), ()))
_TN = (((0,), (0,)), ((), ()))


def _rotary(x, cos2, sin2):
    return x * cos2 + pltpu.roll(x, HEAD_DIM // 2, 1) * sin2


def _norm_gate(o, g, gn_w):
    mu = jnp.mean(o, axis=-1, keepdims=True)
    oc = o - mu
    var = jnp.mean(oc * oc, axis=-1, keepdims=True)
    return jax.nn.silu(g) * (oc * lax.rsqrt(var + NORM_EPS) * gn_w)


def _ret_seq_kernel(q_ref, k_ref, v_ref, g_ref, cos_ref, sin_ref, mask_ref, qd_ref, kd_ref, cd_ref,
                    gnw_ref, o_ref, rn_ref, r_s, *, chunks):
    @pl.when(pl.program_id(1) == 0)
    def _():
        r_s[...] = jnp.zeros_like(r_s)

    scale = HEAD_DIM ** -0.5
    for c in range(chunks):
        rows = slice(c * RET_TILE, (c + 1) * RET_TILE)
        cos = cos_ref[rows, :]
        sin = sin_ref[rows, :]
        for h in range(RET_HEADS):
            hs = slice(h * HEAD_DIM, (h + 1) * HEAD_DIM)
            qr = _rotary(q_ref[rows, hs].astype(F32), cos, sin)
            kr = _rotary(k_ref[rows, hs].astype(F32), cos, sin) * scale
            vb = v_ref[rows, hs]
            sc = lax.dot_general(qr.astype(BF16), kr.astype(BF16), _NT,
                                 preferred_element_type=F32) * mask_ref[h]
            r_old = r_s[h]
            lhs = jnp.concatenate([sc.astype(BF16), (qr * qd_ref[h]).astype(BF16)], axis=1)
            rhs = jnp.concatenate([vb, r_old.astype(BF16)], axis=0)
            o = jnp.dot(lhs, rhs, preferred_element_type=F32)
            r_s[h] = r_old * cd_ref[h] + lax.dot_general(
                (kr * kd_ref[h]).astype(BF16), vb, _TN, preferred_element_type=F32)
            o_ref[rows, hs] = _norm_gate(o, g_ref[rows, hs].astype(F32),
                                         gnw_ref[:, hs]).astype(o_ref.dtype)
    rn_ref[0] = r_s[...]


def _ret_step_kernel(q_ref, k_ref, v_ref, g_ref, cos_ref, sin_ref, mask_ref, qd_ref, kd_ref, cd_ref,
                     gnw_ref, r0_ref, o_ref, rn_ref, qr_s, kdt_s, ob_s, *, n_seq):
    rows_per_seq = RET_TILE // n_seq
    cos = cos_ref[...]
    sin = sin_ref[...]
    scale = HEAD_DIM ** -0.5
    for h in range(RET_HEADS):
        hs = slice(h * HEAD_DIM, (h + 1) * HEAD_DIM)
        qr = _rotary(q_ref[:, hs].astype(F32), cos, sin)
        kr = _rotary(k_ref[:, hs].astype(F32), cos, sin) * scale
        sc = lax.dot_general(qr.astype(BF16), kr.astype(BF16), _NT,
                             preferred_element_type=F32) * mask_ref[h]
        ob_s[:, hs] = jnp.dot(sc.astype(BF16), v_ref[:, hs], preferred_element_type=F32)
        qr_s[:, hs] = qr
        kdt_s[h] = (kr * kd_ref[h]).T

    lane = lax.broadcasted_iota(jnp.int32, (HEAD_DIM, RET_TILE), 1)

    def per_seq(s, c):
        r0 = pl.multiple_of(s * rows_per_seq, rows_per_seq)
        rows = pl.ds(r0, rows_per_seq)
        in_seq = (lane >= r0) & (lane < r0 + rows_per_seq)
        for h in range(RET_HEADS):
            hs = slice(h * HEAD_DIM, (h + 1) * HEAD_DIM)
            r_old = r0_ref[s, h]
            cross = jnp.dot(qr_s[rows, hs].astype(BF16), r_old.astype(BF16), preferred_element_type=F32)
            ob_s[rows, hs] = ob_s[rows, hs] + cross * qd_ref[h, rows, :]
            kdt = jnp.where(in_seq, kdt_s[h], 0.0).astype(BF16)
            rn_ref[s, h] = r_old * cd_ref[h] + jnp.dot(kdt, v_ref[:, hs], preferred_element_type=F32)
        return c

    lax.fori_loop(0, n_seq, per_seq, 0)

    for h in range(RET_HEADS):
        hs = slice(h * HEAD_DIM, (h + 1) * HEAD_DIM)
        o_ref[:, hs] = _norm_gate(ob_s[:, hs], g_ref[:, hs].astype(F32),
                                  gnw_ref[:, hs]).astype(o_ref.dtype)


def _rotary_tables(pos0, rows, reps):
    half = HEAD_DIM // 2
    inv_freq = ROPE_BASE ** (-np.arange(half, dtype=np.float64) / half)
    pos = pos0 + np.arange(rows, dtype=np.float64)
    ang = pos[:, None] * inv_freq[None, :]
    cos, sin = np.cos(ang), np.sin(ang)
    cos2 = np.concatenate([cos, cos], axis=-1)
    sin2 = np.concatenate([-sin, sin], axis=-1)
    return (jnp.asarray(np.tile(cos2, (reps, 1)), dtype=F32),
            jnp.asarray(np.tile(sin2, (reps, 1)), dtype=F32))


def _decay_tables(chunk, n_seq):
    log_gamma = np.log(1.0 - 2.0 ** (-5.0 - np.arange(RET_HEADS, dtype=np.float64)))
    idx = np.arange(chunk, dtype=np.float64)
    diff = idx[:, None] - idx[None, :]
    mask = np.where(diff >= 0, np.exp(log_gamma[:, None, None] * np.maximum(diff, 0.0)), 0.0)
    q_decay = np.exp(log_gamma[:, None] * (idx + 1.0))
    k_decay = np.exp(log_gamma[:, None] * (chunk - 1.0 - idx))
    chunk_decay = np.exp(log_gamma * chunk)
    mask_t = np.einsum("hab,st->hsatb", mask, np.eye(n_seq)).reshape(RET_HEADS, RET_TILE, RET_TILE)
    qd_t = np.broadcast_to(np.tile(q_decay, (1, n_seq))[:, :, None], (RET_HEADS, RET_TILE, HEAD_DIM))
    kd_t = np.broadcast_to(np.tile(k_decay, (1, n_seq))[:, :, None], (RET_HEADS, RET_TILE, HEAD_DIM))
    as_f32 = lambda a: jnp.asarray(np.ascontiguousarray(a), dtype=F32)
    return as_f32(mask_t), as_f32(qd_t), as_f32(kd_t), as_f32(chunk_decay)


def _retention(proj, r0, gn_w, n_seq, seq_len, pos0, chunks=4):
    t = proj.shape[0]
    const3 = lambda *_: (0, 0, 0)
    if r0 is None:
        tile, tile_seqs = chunks * RET_TILE, 1
        steps = seq_len // tile
        cos2, sin2 = _rotary_tables(pos0, seq_len, 1)
        grid = (n_seq, steps)
        row = lambda n, c: n * steps + c
        tab_map = lambda n, c: (c, 0)
        state_map = lambda n, c: (n, 0, 0, 0)
        state_block = (1, RET_HEADS, HEAD_DIM, HEAD_DIM)
        sem = ("arbitrary", "arbitrary")
        body = functools.partial(_ret_seq_kernel, chunks=chunks)
        scratch = [pltpu.VMEM((RET_HEADS, HEAD_DIM, HEAD_DIM), F32)]
        name = "ret_seq"
    else:
        tile, tile_seqs = RET_TILE, RET_TILE // seq_len
        cos2, sin2 = _rotary_tables(pos0, seq_len, tile_seqs)
        grid = (t // tile,)
        row = lambda i: i
        tab_map = lambda i: (0, 0)
        state_map = lambda i: (i, 0, 0, 0)
        state_block = (tile_seqs, RET_HEADS, HEAD_DIM, HEAD_DIM)
        sem = ("arbitrary",)
        body = functools.partial(_ret_step_kernel, n_seq=tile_seqs)
        scratch = [pltpu.VMEM((RET_TILE, RET_WIDTH), F32),
                   pltpu.VMEM((RET_HEADS, HEAD_DIM, RET_TILE), F32),
                   pltpu.VMEM((RET_TILE, RET_WIDTH), F32)]
        name = "ret_step"
    mask_t, qd_t, kd_t, cd = _decay_tables(RET_TILE // tile_seqs, tile_seqs)

    def col(cb):
        return pl.BlockSpec((tile, RET_WIDTH), lambda *a: (row(*a), cb))

    in_specs = [
        col(1), col(2), col(3), col(4),
        pl.BlockSpec((tile, HEAD_DIM), tab_map),
        pl.BlockSpec((tile, HEAD_DIM), tab_map),
        pl.BlockSpec((RET_HEADS, RET_TILE, RET_TILE), const3),
        pl.BlockSpec((RET_HEADS, RET_TILE, HEAD_DIM), const3),
        pl.BlockSpec((RET_HEADS, RET_TILE, HEAD_DIM), const3),
        pl.BlockSpec(memory_space=pltpu.SMEM),
        pl.BlockSpec((1, RET_WIDTH), lambda *_: (0, 0)),
    ]
    args = [proj, proj, proj, proj, cos2, sin2, mask_t, qd_t, kd_t, cd, gn_w]
    if r0 is not None:
        in_specs.append(pl.BlockSpec(state_block, state_map))
        args.append(r0)
    return pl.pallas_call(
        body,
        grid=grid,
        in_specs=in_specs,
        out_specs=(pl.BlockSpec((tile, RET_WIDTH), lambda *a: (row(*a), 0)),
                   pl.BlockSpec(state_block, state_map)),
        out_shape=(jax.ShapeDtypeStruct((t, RET_WIDTH), BF16),
                   jax.ShapeDtypeStruct((n_seq, RET_HEADS, HEAD_DIM, HEAD_DIM), F32)),
        scratch_shapes=scratch,
        compiler_params=_params(*sem),
        name=name,
    )(*args)


def _outproj_kernel(x_ref, a_ref, b_ref, wa_ref, wb_ref, o_ref, *w_bf16, cast):
    if cast:
        (wo_ref,) = w_bf16

        @pl.when(pl.program_id(0) == 0)
        def _():
            wo_ref[:S5_WIDTH, :] = wa_ref[...].astype(BF16)
            wo_ref[S5_WIDTH:, :] = wb_ref[...].astype(BF16)

        wa, wb = wo_ref[:S5_WIDTH, :], wo_ref[S5_WIDTH:, :]
    else:
        wa, wb = wa_ref[...], wb_ref[...]
    o_ref[...] = (x_ref[...]
                  + jnp.dot(a_ref[...], wa, preferred_element_type=F32)
                  + jnp.dot(b_ref[...], wb, preferred_element_type=F32))


def _outproj(x2d, a, b, w, tm=512):
    t = x2d.shape[0]
    cast = w.dtype == F32
    resident = pl.Buffered(1)
    out_specs = [pl.BlockSpec((tm, D_MODEL), lambda i: (i, 0))]
    out_shape = [jax.ShapeDtypeStruct((t, D_MODEL), F32)]
    if cast:
        out_specs.append(pl.BlockSpec((S5_WIDTH + RET_WIDTH, D_MODEL), lambda i: (0, 0)))
        out_shape.append(jax.ShapeDtypeStruct((S5_WIDTH + RET_WIDTH, D_MODEL), BF16))
    return pl.pallas_call(
        functools.partial(_outproj_kernel, cast=cast),
        grid=(t // tm,),
        in_specs=[
            pl.BlockSpec((tm, D_MODEL), lambda i: (i, 0)),
            pl.BlockSpec((tm, S5_WIDTH), lambda i: (i, 0)),
            pl.BlockSpec((tm, RET_WIDTH), lambda i: (i, 0)),
            pl.BlockSpec((S5_WIDTH, D_MODEL), lambda i: (0, 0), pipeline_mode=resident),
            pl.BlockSpec((RET_WIDTH, D_MODEL), lambda i: (1, 0), pipeline_mode=resident),
        ],
        out_specs=out_specs,
        out_shape=out_shape,
        compiler_params=_params("arbitrary"),
        name="out_proj_cast" if cast else "out_proj",
    )(x2d, a, b, w, w)


def _ffn_kernel(x_ref, gn_ref, wg_ref, wu_ref, wo_ref, gf_ref, o_ref, *rest, cast):
    if cast:
        wgb_ref, wub_ref, wob_ref, h_scr = rest
        wgb_ref[...] = wg_ref[...].astype(BF16)
        wub_ref[...] = wu_ref[...].astype(BF16)
        wob_ref[...] = wo_ref[...].astype(BF16)
        wg_ref, wu_ref, wo_ref = wgb_ref, wub_ref, wob_ref
    else:
        (h_scr,) = rest
    j = pl.program_id(1)

    @pl.when(j == 0)
    def _():
        h_scr[...] = _rms(x_ref[...], gn_ref[...]).astype(BF16)
        o_ref[...] = x_ref[...]

    for r in range(x_ref.shape[0] // FFN_ROWS):
        rows = slice(r * FFN_ROWS, (r + 1) * FFN_ROWS)
        h = h_scr[rows, :]
        gate = jnp.dot(h, wg_ref[...], preferred_element_type=F32)
        up = jnp.dot(h, wu_ref[...], preferred_element_type=F32)
        act = (jax.nn.silu(gate) * up).astype(BF16)
        o_ref[rows, :] += jnp.dot(act, wo_ref[...], preferred_element_type=F32)

    @pl.when(j == pl.num_programs(1) - 1)
    def _():
        o_ref[...] = _rms(o_ref[...], gf_ref[...])


def _ffn(x2d, g_ffn, w_gate, w_up, w_down, g_final, tm=1024, tf=512):
    t = x2d.shape[0]
    nf = D_FF // tf
    cast = w_down.dtype == F32
    (wg, g0), (wu, u0) = w_gate, w_up
    g0, u0 = g0 * nf, u0 * nf
    x_mode = {}
    out_specs = [pl.BlockSpec((tm, D_MODEL), lambda i, j: (i, 0))]
    out_shape = [jax.ShapeDtypeStruct((t, D_MODEL), F32)]
    if cast:
        assert t == tm, "the bf16 weight outputs are written once per row tile"
        x_mode = dict(pipeline_mode=pl.Buffered(1))
        out_specs += [pl.BlockSpec((D_MODEL, tf), lambda i, j: (0, j)),
                      pl.BlockSpec((D_MODEL, tf), lambda i, j: (0, j)),
                      pl.BlockSpec((tf, D_MODEL), lambda i, j: (j, 0))]
        out_shape += [jax.ShapeDtypeStruct((D_MODEL, D_FF), BF16),
                      jax.ShapeDtypeStruct((D_MODEL, D_FF), BF16),
                      jax.ShapeDtypeStruct((D_FF, D_MODEL), BF16)]
    return pl.pallas_call(
        functools.partial(_ffn_kernel, cast=cast),
        grid=(t // tm, nf),
        in_specs=[
            pl.BlockSpec((tm, D_MODEL), lambda i, j: (i, 0), **x_mode),
            pl.BlockSpec((1, D_MODEL), lambda i, j: (0, 0)),
            pl.BlockSpec((D_MODEL, tf), lambda i, j: (0, j + g0)),
            pl.BlockSpec((D_MODEL, tf), lambda i, j: (0, j + u0)),
            pl.BlockSpec((tf, D_MODEL), lambda i, j: (j, 0)),
            pl.BlockSpec((1, D_MODEL), lambda i, j: (0, 0)),
        ],
        out_specs=out_specs,
        out_shape=out_shape,
        scratch_shapes=[pltpu.VMEM((tm, D_MODEL), BF16)],
        compiler_params=_params("arbitrary", "arbitrary"),
        name="ffn_cast" if cast else "ffn",
    )(x2d, g_ffn, wg, wu, w_down, g_final)


def _finish(x2d, proj, s5_out, ret_state, pos0, w, w_out, ffn_w, ffn_tf, n, l):
    ret_out, ret_new = _retention(proj, ret_state, w["gn_w"], n, l, pos0)
    x1, *w_out_b = _outproj(x2d, s5_out, ret_out, w_out)
    y, *ffn_b = _ffn(x1, w["norm_ffn"], *ffn_w, w["norm_final"], tf=ffn_tf)
    return y.reshape(n, l, D_MODEL), ret_new, w_out_b, ffn_b


def _as_groups(re, im, n):
    return re.reshape(n, S5_GROUPS, S5_STATE), im.reshape(n, S5_GROUPS, S5_STATE)


def _sample_layer(x, s5_state, ret_state, w, big):
    n, l, _ = x.shape
    x2d = x.reshape(n * l, D_MODEL)
    proj, w_in_b = _inproj_cast(x2d, w["norm_mix"], big["w_in"])
    s5_out, s5_re, s5_im, w_glu_b = _s5_step(proj, s5_state, w["tabs"], w["wb"], w["cd"], w["d"],
                                             big["w_glu"], w["b_glu"])
    ffn_f32 = ((big["w_ffn_in"], 0), (big["w_ffn_in"], 1), big["w_ffn_out"])
    y, ret_new, (w_out_b,), (wg_b, wu_b, wd_b) = _finish(
        x2d, proj, s5_out, ret_state, float(PAST_LEN), w, big["w_out"], ffn_f32, 256, n, l)
    big_b = dict(w_in=w_in_b, w_glu=w_glu_b, w_out=w_out_b, ffn=((wg_b, 0), (wu_b, 0), wd_b))
    return (y, *_as_groups(s5_re, s5_im, n), ret_new), big_b


def _prompt_layer(x, w, big_b):
    n, l, _ = x.shape
    x2d = x.reshape(n * l, D_MODEL)
    proj, u_t = _inproj_seq(x2d, w["norm_mix"], big_b["w_in"], n)
    s5_out, s5_re, s5_im = _s5_seq(u_t, w["pairs"], w["wb"], w["cd"], w["d"], big_b["w_glu"],
                                   w["b_glu"], n, l)
    y, ret_new, _, _ = _finish(x2d, proj, s5_out, None, 0.0, w, big_b["w_out"], big_b["ffn"],
                               512, n, l)
    return (y, *_as_groups(s5_re, s5_im, n), ret_new)


def kernel(x_prompt, x_sample, state_s5_re, state_s5_im, state_ret, norm_mix, w_in, s5_lambda_re, s5_lambda_im, s5_log_step, s5_b_re, s5_b_im, s5_c_re, s5_c_im, s5_d, s5_w_glu, s5_b_glu, ret_gn_w, w_out, norm_ffn, w_ffn_in, w_ffn_out, norm_final):
    assert norm_mix.shape[0] == 1, "single-layer stack"
    tabs, pairs, wb, cd = _s5_params(s5_lambda_re[0], s5_lambda_im[0], s5_log_step[0],
                                     s5_b_re[0], s5_b_im[0], s5_c_re[0], s5_c_im[0])
    w = dict(norm_mix=norm_mix, tabs=tabs, pairs=pairs, wb=wb, cd=cd, d=s5_d, b_glu=s5_b_glu,
             gn_w=ret_gn_w, norm_ffn=norm_ffn, norm_final=norm_final.reshape(1, D_MODEL))
    big = dict(w_in=w_in[0], w_glu=s5_w_glu[0], w_out=w_out[0],
               w_ffn_in=w_ffn_in[0], w_ffn_out=w_ffn_out[0])
    n_s = x_sample.shape[0]
    (ys, s_re, s_im, s_ret), big_b = _sample_layer(
        x_sample,
        (state_s5_re[0].reshape(n_s, N_STATE), state_s5_im[0].reshape(n_s, N_STATE)),
        state_ret[0], w, big)
    yp, p_re, p_im, p_ret = _prompt_layer(x_prompt, w, big_b)
    return (yp, ys, p_re[None], p_im[None], p_ret[None], s_re[None], s_im[None], s_ret[None])
```

```python
import functools
import math

import jax
import jax.numpy as jnp
import numpy as np
from jax import lax
from jax.experimental import pallas as pl
from jax.experimental.pallas import tpu as pltpu

F32 = jnp.float32
BF16 = jnp.bfloat16

D_MODEL = 2048
S5_WIDTH = 1024
S5_GROUP = 16
S5_GROUPS = 64
S5_STATE = 64
N_STATE = S5_GROUPS * S5_STATE
RET_WIDTH = 1024
RET_HEADS = 8
HEAD_DIM = 128
ROPE_BASE = 10000.0
D_FF = 5632
IN_WIDTH = S5_WIDTH + 4 * RET_WIDTH
NORM_EPS = 1e-6
PAST_LEN = 16384

SUBLANES = 8
SLAB_GROUPS = 8
N_SLABS = S5_GROUPS // SLAB_GROUPS
SLAB_IN = SLAB_GROUPS * S5_GROUP
SLAB_STATE = SLAB_GROUPS * S5_STATE
RET_TILE = 128
FFN_ROWS = 512

VMEM_LIMIT_BYTES = 60 * 1024 * 1024


def _params(*sem):
    return pltpu.CompilerParams(dimension_semantics=sem, vmem_limit_bytes=VMEM_LIMIT_BYTES)


def _rms(xf, g):
    ms = jnp.mean(xf * xf, axis=-1, keepdims=True)
    return xf * lax.rsqrt(ms + NORM_EPS) * g


def _inproj_seq_kernel(x_ref, g_ref, w_ref, o_ref, ou_ref, h_scr, *, tn, n_seq):
    seq = pl.program_id(1)
    tm = x_ref.shape[0]
    h_scr[...] = _rms(x_ref[...], g_ref[...]).astype(BF16)
    for j in range(IN_WIDTH // tn):
        cols = slice(j * tn, (j + 1) * tn)
        res = jnp.dot(h_scr[...], w_ref[:, cols], preferred_element_type=F32)
        if j * tn < S5_WIDTH:
            dst = pl.ds(seq, tm, stride=n_seq)
            for s in range(tn // SLAB_IN):
                ou_ref[j * (tn // SLAB_IN) + s, dst, :] = res[:, s * SLAB_IN:(s + 1) * SLAB_IN]
        o_ref[:, cols] = res.astype(o_ref.dtype)


def _inproj_seq(x2d, g, w, n_seq, tm=256, tn=1024):
    t = x2d.shape[0]
    tiles = t // n_seq // tm
    row = lambda i, n: (n * tiles + i, 0)
    const = lambda i, n: (0, 0)
    return pl.pallas_call(
        functools.partial(_inproj_seq_kernel, tn=tn, n_seq=n_seq),
        grid=(tiles, n_seq),
        in_specs=[
            pl.BlockSpec((tm, D_MODEL), row),
            pl.BlockSpec((1, D_MODEL), const),
            pl.BlockSpec((D_MODEL, IN_WIDTH), const, pipeline_mode=pl.Buffered(1)),
        ],
        out_specs=[pl.BlockSpec((tm, IN_WIDTH), row),
                   pl.BlockSpec((N_SLABS, n_seq * tm, SLAB_IN), lambda i, n: (0, i, 0))],
        out_shape=[jax.ShapeDtypeStruct((t, IN_WIDTH), BF16),
                   jax.ShapeDtypeStruct((N_SLABS, t, SLAB_IN), F32)],
        scratch_shapes=[pltpu.VMEM((tm, D_MODEL), BF16)],
        compiler_params=_params("arbitrary", "arbitrary"),
        name="in_proj_seq",
    )(x2d, g, w)


def _inproj_cast_kernel(x_ref, g_ref, w_ref, o_ref, wb_ref, ou_ref, h_scr, ru_s):
    j = pl.program_id(0)

    @pl.when(j == 0)
    def _():
        h_scr[...] = _rms(x_ref[...], g_ref[...]).astype(BF16)

    wb_ref[...] = w_ref[...].astype(BF16)
    res = jnp.dot(h_scr[...], wb_ref[...], preferred_element_type=F32)
    o_ref[...] = res.astype(o_ref.dtype)
    for s in range(N_SLABS):
        ru_s[s] = res[:, s * SLAB_IN:(s + 1) * SLAB_IN]

    @pl.when(j == 0)
    def _():
        seq_len, n_seq = ou_ref.shape[1], ou_ref.shape[2]
        for s in range(N_SLABS):
            for t in range(seq_len):
                ou_ref[s, t] = ru_s[s, pl.ds(t, n_seq, stride=seq_len), :]


def _inproj_cast(x2d, g, w_f32, seq_len):
    t = x2d.shape[0]
    tn = S5_WIDTH
    return pl.pallas_call(
        _inproj_cast_kernel,
        grid=(IN_WIDTH // tn,),
        in_specs=[
            pl.BlockSpec((t, D_MODEL), lambda j: (0, 0), pipeline_mode=pl.Buffered(1)),
            pl.BlockSpec((1, D_MODEL), lambda j: (0, 0)),
            pl.BlockSpec((D_MODEL, tn), lambda j: (0, j)),
        ],
        out_specs=[pl.BlockSpec((t, tn), lambda j: (0, j)),
                   pl.BlockSpec((D_MODEL, tn), lambda j: (0, j)),
                   pl.BlockSpec((N_SLABS, seq_len, t // seq_len, SLAB_IN), lambda j: (0, 0, 0, 0))],
        out_shape=[jax.ShapeDtypeStruct((t, IN_WIDTH), BF16),
                   jax.ShapeDtypeStruct((D_MODEL, IN_WIDTH), BF16),
                   jax.ShapeDtypeStruct((N_SLABS, seq_len, t // seq_len, SLAB_IN), F32)],
        scratch_shapes=[pltpu.VMEM((t, D_MODEL), BF16), pltpu.VMEM((N_SLABS, t, SLAB_IN), F32)],
        compiler_params=_params("arbitrary"),
        name="in_proj_cast",
    )(x2d, g, w_f32)


def _s5_param_kernel(lr_ref, li_ref, ls_ref, br_ref, bi_ref, ctr_ref, cti_ref,
                     pairr_ref, pairi_ref, wbr_ref, wbi_ref, cdr_ref, cdi_ref):
    lr = lr_ref[...]
    li = li_ref[...]
    dt = jnp.exp(ls_ref[...])
    mag = jnp.exp(lr * dt)
    ar = mag * jnp.cos(li * dt)
    ai = mag * jnp.sin(li * dt)
    den = lr * lr + li * li
    nr = ar - 1.0
    cr = (nr * lr + ai * li) / den
    ci = (ai * lr - nr * li) / den
    br = br_ref[...]
    bi = bi_ref[...]
    bbr = cr * br - ci * bi
    bbi = cr * bi + ci * br
    abr = ar * bbr - ai * bbi
    abi = ar * bbi + ai * bbr

    in_shape = (SLAB_IN, SLAB_STATE)
    same_in = (lax.broadcasted_iota(jnp.int32, in_shape, 0) // S5_GROUP
               == lax.broadcasted_iota(jnp.int32, in_shape, 1) // S5_STATE)
    out_shape = (SLAB_STATE, SLAB_IN)
    same_out = (lax.broadcasted_iota(jnp.int32, out_shape, 0) // S5_STATE
                == lax.broadcasted_iota(jnp.int32, out_shape, 1) // S5_GROUP)
    for s in range(N_SLABS):
        cs = slice(s * SLAB_STATE, (s + 1) * SLAB_STATE)
        for dst, top, bot in ((wbr_ref, bbr, abr), (wbi_ref, bbi, abi)):
            for k, part in enumerate((top, bot)):
                blk = jnp.tile(part[:, cs], (SLAB_GROUPS, 1))
                dst[s, k * SLAB_IN:(k + 1) * SLAB_IN, :] = jnp.where(same_in, blk, 0.0).astype(BF16)
        cdr_ref[s] = jnp.where(same_out, ctr_ref[cs, :], 0.0).astype(BF16)
        cdi_ref[s] = jnp.where(same_out, cti_ref[cs, :], 0.0).astype(BF16)

    rows = lax.broadcasted_iota(jnp.int32, (SUBLANES, N_STATE), 0)
    second = rows >= SUBLANES // 2
    full = lambda v: jnp.broadcast_to(v, (SUBLANES, N_STATE))
    pairr_ref[...] = jnp.where(second, full(ar * ar - ai * ai), full(ar))
    pairi_ref[...] = jnp.where(second, full(ar * ai + ai * ar), full(ai))


def _s5_params(lam_re, lam_im, log_step, b_re, b_im, c_re, c_im):
    lr = lam_re.reshape(1, N_STATE)
    li = lam_im.reshape(1, N_STATE)
    ls = jnp.repeat(log_step, S5_STATE).reshape(1, N_STATE)
    b_rows = lambda b: jnp.transpose(b, (2, 0, 1)).reshape(S5_GROUP, N_STATE)
    c_cols = lambda c: jnp.tile(jnp.transpose(c, (0, 2, 1)).reshape(N_STATE, S5_GROUP),
                                (1, SLAB_GROUPS))
    pair = jax.ShapeDtypeStruct((SUBLANES, N_STATE), F32)
    wb = jax.ShapeDtypeStruct((N_SLABS, 2 * SLAB_IN, SLAB_STATE), BF16)
    cd = jax.ShapeDtypeStruct((N_SLABS, SLAB_STATE, SLAB_IN), BF16)
    pairr, pairi, wbr, wbi, cdr, cdi = pl.pallas_call(
        _s5_param_kernel,
        out_shape=(pair, pair, wb, wb, cd, cd),
        name="s5_params",
    )(lr, li, ls, b_rows(b_re), b_rows(b_im), c_cols(c_re), c_cols(c_im))
    return (pairr, pairi), (wbr, wbi), (cdr, cdi)


def _cmul_add(xr, xi, pr, pi, vr, vi):
    return xr + (pr * vr - pi * vi), xi + (pr * vi + pi * vr)


def _s5_weight_specs(k_in):
    const3 = lambda *_: (0, 0, 0)
    const2 = lambda *_: (0, 0)
    return [
        pl.BlockSpec((SUBLANES, N_STATE), const2),
        pl.BlockSpec((SUBLANES, N_STATE), const2),
        pl.BlockSpec((N_SLABS, k_in, SLAB_STATE), const3),
        pl.BlockSpec((N_SLABS, k_in, SLAB_STATE), const3),
        pl.BlockSpec((N_SLABS, SLAB_STATE, SLAB_IN), const3),
        pl.BlockSpec((N_SLABS, SLAB_STATE, SLAB_IN), const3),
        pl.BlockSpec((1, S5_WIDTH), const2),
        pl.BlockSpec((S5_WIDTH, S5_WIDTH), const2),
        pl.BlockSpec((1, S5_WIDTH), const2),
    ]


def _s5_step_kernel(u_ref, x0r_ref, x0i_ref, pairr_ref, pairi_ref, wbr_ref, wbi_ref, cdr_ref, cdi_ref,
                    d_ref, wg_ref, bg_ref, o_ref, xlr_ref, xli_ref, wgb_ref, xr_s, xi_s, y_s, o_s):
    seq_len, n_seq = u_ref.shape[1], u_ref.shape[2]

    @pl.when(pl.program_id(0) == 0)
    def _():
        wgb_ref[...] = wg_ref[...].astype(BF16)

    for s in range(N_SLABS):
        cs = slice(s * SLAB_STATE, (s + 1) * SLAB_STATE)
        us = slice(s * SLAB_IN, (s + 1) * SLAB_IN)
        u = u_ref[s].reshape(seq_len * n_seq, SLAB_IN)
        ub = u.astype(BF16)
        xr_s[:, cs] = jnp.dot(ub, wbr_ref[s], preferred_element_type=F32)
        xi_s[:, cs] = jnp.dot(ub, wbi_ref[s], preferred_element_type=F32)

        a_r = jnp.broadcast_to(pairr_ref[0:1, cs], (SUBLANES, SLAB_STATE))
        a_i = jnp.broadcast_to(pairi_ref[0:1, cs], (SUBLANES, SLAB_STATE))
        for b in range(n_seq // SUBLANES):
            seqs = slice(b * SUBLANES, (b + 1) * SUBLANES)
            xr, xi = x0r_ref[seqs, cs], x0i_ref[seqs, cs]
            for t in range(seq_len):
                rows = slice(t * n_seq + b * SUBLANES, t * n_seq + (b + 1) * SUBLANES)
                xr, xi = _cmul_add(xr_s[rows, cs], xi_s[rows, cs], a_r, a_i, xr, xi)
                xr_s[rows, cs] = xr
                xi_s[rows, cs] = xi
            xlr_ref[seqs, cs] = xr
            xli_ref[seqs, cs] = xi

        y = (jnp.dot(xr_s[:, cs].astype(BF16), cdr_ref[s], preferred_element_type=F32)
             - jnp.dot(xi_s[:, cs].astype(BF16), cdi_ref[s], preferred_element_type=F32))
        y_s[:, us] = jax.nn.gelu(y + d_ref[:, us] * u)

    y = y_s[...]
    z = jnp.dot(y.astype(BF16), wgb_ref[...], preferred_element_type=F32) + bg_ref[...]
    o = y * jax.nn.sigmoid(z)
    for s in range(N_SLABS):
        for t in range(seq_len):
            o_s[s, pl.ds(t, n_seq, stride=seq_len), :] = (
                o[t * n_seq:(t + 1) * n_seq, s * SLAB_IN:(s + 1) * SLAB_IN])
    for s in range(N_SLABS):
        o_ref[:, s * SLAB_IN:(s + 1) * SLAB_IN] = o_s[s].astype(o_ref.dtype)


def _s5_step(u_t, x0, pairs, wb, cd, d, w_glu, b_glu, n_tile=32):
    _, seq_len, n_seq, _ = u_t.shape
    rows = n_tile * seq_len
    row = lambda i: (i, 0)
    st = jax.ShapeDtypeStruct((n_seq, N_STATE), F32)
    return pl.pallas_call(
        _s5_step_kernel,
        grid=(n_seq // n_tile,),
        in_specs=[pl.BlockSpec((N_SLABS, seq_len, n_tile, SLAB_IN), lambda i: (0, 0, i, 0)),
                  pl.BlockSpec((n_tile, N_STATE), row), pl.BlockSpec((n_tile, N_STATE), row)]
        + _s5_weight_specs(SLAB_IN),
        out_specs=(pl.BlockSpec((rows, S5_WIDTH), row), pl.BlockSpec((n_tile, N_STATE), row),
                   pl.BlockSpec((n_tile, N_STATE), row),
                   pl.BlockSpec((S5_WIDTH, S5_WIDTH), lambda i: (0, 0))),
        out_shape=(jax.ShapeDtypeStruct((n_seq * seq_len, S5_WIDTH), BF16), st, st,
                   jax.ShapeDtypeStruct((S5_WIDTH, S5_WIDTH), BF16)),
        scratch_shapes=[pltpu.VMEM((rows, N_STATE), F32), pltpu.VMEM((rows, N_STATE), F32),
                        pltpu.VMEM((rows, S5_WIDTH), F32),
                        pltpu.VMEM((N_SLABS, rows, SLAB_IN), F32)],
        compiler_params=_params("arbitrary"),
        name="s5_step",
    )(u_t, x0[0], x0[1], pairs[0], pairs[1], wb[0], wb[1], cd[0], cd[1], d, w_glu, b_glu)


def _s5_seq_kernel(u_ref, pairr_ref, pairi_ref, wbr_ref, wbi_ref, cdr_ref, cdi_ref, d_ref, wg_ref,
                   bg_ref, o_ref, xlr_ref, xli_ref, xr_s, xi_s, y_s, o_s, cbr_s, cbi_s):
    n_rows = u_ref.shape[1]
    half = SUBLANES // 2
    tt = n_rows // half

    @pl.when(pl.program_id(0) == 0)
    def _():
        cbr_s[...] = jnp.zeros_like(cbr_s)
        cbi_s[...] = jnp.zeros_like(cbi_s)

    first = lax.broadcasted_iota(jnp.int32, (SUBLANES, SLAB_STATE), 0) < half
    second_step = (lax.broadcasted_iota(jnp.int32, (n_rows, SLAB_IN), 0) & half) != 0
    for s in range(N_SLABS):
        cs = slice(s * SLAB_STATE, (s + 1) * SLAB_STATE)
        us = slice(s * SLAB_IN, (s + 1) * SLAB_IN)
        u = u_ref[s]
        u_prev = jnp.where(second_step, pltpu.roll(u, half, 0), 0.0)
        ub = jnp.concatenate([u, u_prev], axis=1).astype(BF16)
        xr_s[:, cs] = jnp.dot(ub, wbr_ref[s], preferred_element_type=F32)
        xi_s[:, cs] = jnp.dot(ub, wbi_ref[s], preferred_element_type=F32)

        pr, pi = pairr_ref[:, cs], pairi_ref[:, cs]
        cr, ci = cbr_s[:, cs], cbi_s[:, cs]
        for b in range(n_rows // SUBLANES):
            rows = slice(b * SUBLANES, (b + 1) * SUBLANES)
            xr, xi = _cmul_add(xr_s[rows, cs], xi_s[rows, cs], pr, pi, cr, ci)
            xr_s[rows, cs] = xr
            xi_s[rows, cs] = xi
            cr = jnp.where(first, pltpu.roll(xr, half, 0), xr)
            ci = jnp.where(first, pltpu.roll(xi, half, 0), xi)
        cbr_s[:, cs] = cr
        cbi_s[:, cs] = ci

        y = (jnp.dot(xr_s[:, cs].astype(BF16), cdr_ref[s], preferred_element_type=F32)
             - jnp.dot(xi_s[:, cs].astype(BF16), cdi_ref[s], preferred_element_type=F32))
        y_s[:, us] = jax.nn.gelu(y + d_ref[:, us] * u)

    xlr_ref[...] = cbr_s[...]
    xli_ref[...] = cbi_s[...]

    y = y_s[...]
    z = jnp.dot(y.astype(BF16), wg_ref[...], preferred_element_type=F32) + bg_ref[...]
    o = y * jax.nn.sigmoid(z)
    for s in range(N_SLABS):
        o_s[s] = o[:, s * SLAB_IN:(s + 1) * SLAB_IN]
    for n in range(half):
        for s in range(N_SLABS):
            o_ref[n, :, s * SLAB_IN:(s + 1) * SLAB_IN] = (
                o_s[s, pl.ds(n, tt, stride=half), :].astype(o_ref.dtype))


def _s5_seq(u_t, pairs, wb, cd, d, w_glu, b_glu, n_seq, seq_len, tt=64):
    assert 2 * n_seq == SUBLANES
    n_rows = n_seq * tt
    st = jax.ShapeDtypeStruct((SUBLANES, N_STATE), F32)
    y, xlr, xli = pl.pallas_call(
        _s5_seq_kernel,
        grid=(seq_len // tt,),
        in_specs=[pl.BlockSpec((N_SLABS, n_rows, SLAB_IN), lambda i: (0, i, 0))]
        + _s5_weight_specs(2 * SLAB_IN),
        out_specs=(pl.BlockSpec((n_seq, tt, S5_WIDTH), lambda i: (0, i, 0)),
                   pl.BlockSpec((SUBLANES, N_STATE), lambda i: (0, 0)),
                   pl.BlockSpec((SUBLANES, N_STATE), lambda i: (0, 0))),
        out_shape=(jax.ShapeDtypeStruct((n_seq, seq_len, S5_WIDTH), BF16), st, st),
        scratch_shapes=[pltpu.VMEM((n_rows, N_STATE), F32), pltpu.VMEM((n_rows, N_STATE), F32),
                        pltpu.VMEM((n_rows, S5_WIDTH), F32),
                        pltpu.VMEM((N_SLABS, n_rows, SLAB_IN), F32),
                        pltpu.VMEM((SUBLANES, N_STATE), F32), pltpu.VMEM((SUBLANES, N_STATE), F32)],
        compiler_params=_params("arbitrary"),
        name="s5_seq",
    )(u_t, pairs[0], pairs[1], wb[0], wb[1], cd[0], cd[1], d, w_glu, b_glu)
    return y.reshape(n_seq * seq_len, S5_WIDTH), xlr[:n_seq], xli[:n_seq]


_NT = (((1,), (1,)), ((), ()))
_TN = (((0,), (0,)), ((), ()))


def _rotary(x, cos2, sin2):
    return x * cos2 + pltpu.roll(x, HEAD_DIM // 2, 1) * sin2


def _norm_gate(o, g, gn_w):
    mu = jnp.mean(o, axis=-1, keepdims=True)
    oc = o - mu
    var = jnp.mean(oc * oc, axis=-1, keepdims=True)
    return jax.nn.silu(g) * (oc * lax.rsqrt(var + NORM_EPS) * gn_w)


def _ret_seq_kernel(q_ref, k_ref, v_ref, g_ref, cos_ref, sin_ref, mask_ref, qd_ref, kd_ref, cd_ref,
                    gnw_ref, o_ref, rn_ref, r_s, *, chunks):
    @pl.when(pl.program_id(1) == 0)
    def _():
        r_s[...] = jnp.zeros_like(r_s)

    scale = HEAD_DIM ** -0.5
    for c in range(chunks):
        rows = slice(c * RET_TILE, (c + 1) * RET_TILE)
        cos = cos_ref[rows, :]
        sin = sin_ref[rows, :]
        for h in range(RET_HEADS):
            hs = slice(h * HEAD_DIM, (h + 1) * HEAD_DIM)
            qr = _rotary(q_ref[rows, hs].astype(F32), cos, sin)
            kr = _rotary(k_ref[rows, hs].astype(F32), cos, sin) * scale
            vb = v_ref[rows, hs]
            sc = lax.dot_general(qr.astype(BF16), kr.astype(BF16), _NT,
                                 preferred_element_type=F32) * mask_ref[h]
            r_old = r_s[h]
            lhs = jnp.concatenate([sc.astype(BF16), (qr * qd_ref[h]).astype(BF16)], axis=1)
            rhs = jnp.concatenate([vb, r_old.astype(BF16)], axis=0)
            o = jnp.dot(lhs, rhs, preferred_element_type=F32)
            r_s[h] = r_old * cd_ref[h] + lax.dot_general(
                (kr * kd_ref[h]).astype(BF16), vb, _TN, preferred_element_type=F32)
            o_ref[rows, hs] = _norm_gate(o, g_ref[rows, hs].astype(F32),
                                         gnw_ref[:, hs]).astype(o_ref.dtype)
    rn_ref[0] = r_s[...]


def _ret_step_kernel(q_ref, k_ref, v_ref, g_ref, cos_ref, sin_ref, mask_ref, qd_ref, kd_ref, cd_ref,
                     gnw_ref, r0_ref, o_ref, rn_ref, qr_s, kdt_s, ob_s, *, n_seq):
    rows_per_seq = RET_TILE // n_seq
    cos = cos_ref[...]
    sin = sin_ref[...]
    scale = HEAD_DIM ** -0.5
    for h in range(RET_HEADS):
        hs = slice(h * HEAD_DIM, (h + 1) * HEAD_DIM)
        qr = _rotary(q_ref[:, hs].astype(F32), cos, sin)
        kr = _rotary(k_ref[:, hs].astype(F32), cos, sin) * scale
        sc = lax.dot_general(qr.astype(BF16), kr.astype(BF16), _NT,
                             preferred_element_type=F32) * mask_ref[h]
        ob_s[:, hs] = jnp.dot(sc.astype(BF16), v_ref[:, hs], preferred_element_type=F32)
        qr_s[:, hs] = qr
        kdt_s[h] = (kr * kd_ref[h]).T

    lane = lax.broadcasted_iota(jnp.int32, (HEAD_DIM, RET_TILE), 1)

    def per_seq(s, c):
        r0 = pl.multiple_of(s * rows_per_seq, rows_per_seq)
        rows = pl.ds(r0, rows_per_seq)
        in_seq = (lane >= r0) & (lane < r0 + rows_per_seq)
        for h in range(RET_HEADS):
            hs = slice(h * HEAD_DIM, (h + 1) * HEAD_DIM)
            r_old = r0_ref[s, h]
            cross = jnp.dot(qr_s[rows, hs].astype(BF16), r_old.astype(BF16), preferred_element_type=F32)
            ob_s[rows, hs] = ob_s[rows, hs] + cross * qd_ref[h, rows, :]
            kdt = jnp.where(in_seq, kdt_s[h], 0.0).astype(BF16)
            rn_ref[s, h] = r_old * cd_ref[h] + jnp.dot(kdt, v_ref[:, hs], preferred_element_type=F32)
        return c

    lax.fori_loop(0, n_seq, per_seq, 0)

    for h in range(RET_HEADS):
        hs = slice(h * HEAD_DIM, (h + 1) * HEAD_DIM)
        o_ref[:, hs] = _norm_gate(ob_s[:, hs], g_ref[:, hs].astype(F32),
                                  gnw_ref[:, hs]).astype(o_ref.dtype)


def _rotary_tables(pos0, rows, reps):
    half = HEAD_DIM // 2
    inv_freq = ROPE_BASE ** (-np.arange(half, dtype=np.float64) / half)
    pos = pos0 + np.arange(rows, dtype=np.float64)
    ang = pos[:, None] * inv_freq[None, :]
    cos, sin = np.cos(ang), np.sin(ang)
    cos2 = np.concatenate([cos, cos], axis=-1)
    sin2 = np.concatenate([-sin, sin], axis=-1)
    return (jnp.asarray(np.tile(cos2, (reps, 1)), dtype=F32),
            jnp.asarray(np.tile(sin2, (reps, 1)), dtype=F32))


def _decay_tables(chunk, n_seq):
    log_gamma = np.log(1.0 - 2.0 ** (-5.0 - np.arange(RET_HEADS, dtype=np.float64)))
    idx = np.arange(chunk, dtype=np.float64)
    diff = idx[:, None] - idx[None, :]
    mask = np.where(diff >= 0, np.exp(log_gamma[:, None, None] * np.maximum(diff, 0.0)), 0.0)
    q_decay = np.exp(log_gamma[:, None] * (idx + 1.0))
    k_decay = np.exp(log_gamma[:, None] * (chunk - 1.0 - idx))
    chunk_decay = np.exp(log_gamma * chunk)
    mask_t = np.einsum("hab,st->hsatb", mask, np.eye(n_seq)).reshape(RET_HEADS, RET_TILE, RET_TILE)
    qd_t = np.broadcast_to(np.tile(q_decay, (1, n_seq))[:, :, None], (RET_HEADS, RET_TILE, HEAD_DIM))
    kd_t = np.broadcast_to(np.tile(k_decay, (1, n_seq))[:, :, None], (RET_HEADS, RET_TILE, HEAD_DIM))
    as_f32 = lambda a: jnp.asarray(np.ascontiguousarray(a), dtype=F32)
    return as_f32(mask_t), as_f32(qd_t), as_f32(kd_t), as_f32(chunk_decay)


def _retention(proj, r0, gn_w, n_seq, seq_len, pos0, chunks=4):
    t = proj.shape[0]
    const3 = lambda *_: (0, 0, 0)
    if r0 is None:
        tile, tile_seqs = chunks * RET_TILE, 1
        steps = seq_len // tile
        cos2, sin2 = _rotary_tables(pos0, seq_len, 1)
        grid = (n_seq, steps)
        row = lambda n, c: n * steps + c
        tab_map = lambda n, c: (c, 0)
        state_map = lambda n, c: (n, 0, 0, 0)
        state_block = (1, RET_HEADS, HEAD_DIM, HEAD_DIM)
        sem = ("arbitrary", "arbitrary")
        body = functools.partial(_ret_seq_kernel, chunks=chunks)
        scratch = [pltpu.VMEM((RET_HEADS, HEAD_DIM, HEAD_DIM), F32)]
        name = "ret_seq"
    else:
        tile, tile_seqs = RET_TILE, RET_TILE // seq_len
        cos2, sin2 = _rotary_tables(pos0, seq_len, tile_seqs)
        grid = (t // tile,)
        row = lambda i: i
        tab_map = lambda i: (0, 0)
        state_map = lambda i: (i, 0, 0, 0)
        state_block = (tile_seqs, RET_HEADS, HEAD_DIM, HEAD_DIM)
        sem = ("arbitrary",)
        body = functools.partial(_ret_step_kernel, n_seq=tile_seqs)
        scratch = [pltpu.VMEM((RET_TILE, RET_WIDTH), F32),
                   pltpu.VMEM((RET_HEADS, HEAD_DIM, RET_TILE), F32),
                   pltpu.VMEM((RET_TILE, RET_WIDTH), F32)]
        name = "ret_step"
    mask_t, qd_t, kd_t, cd = _decay_tables(RET_TILE // tile_seqs, tile_seqs)

    def col(cb):
        return pl.BlockSpec((tile, RET_WIDTH), lambda *a: (row(*a), cb))

    in_specs = [
        col(1), col(2), col(3), col(4),
        pl.BlockSpec((tile, HEAD_DIM), tab_map),
        pl.BlockSpec((tile, HEAD_DIM), tab_map),
        pl.BlockSpec((RET_HEADS, RET_TILE, RET_TILE), const3),
        pl.BlockSpec((RET_HEADS, RET_TILE, HEAD_DIM), const3),
        pl.BlockSpec((RET_HEADS, RET_TILE, HEAD_DIM), const3),
        pl.BlockSpec(memory_space=pltpu.SMEM),
        pl.BlockSpec((1, RET_WIDTH), lambda *_: (0, 0)),
    ]
    args = [proj, proj, proj, proj, cos2, sin2, mask_t, qd_t, kd_t, cd, gn_w]
    if r0 is not None:
        in_specs.append(pl.BlockSpec(state_block, state_map))
        args.append(r0)
    return pl.pallas_call(
        body,
        grid=grid,
        in_specs=in_specs,
        out_specs=(pl.BlockSpec((tile, RET_WIDTH), lambda *a: (row(*a), 0)),
                   pl.BlockSpec(state_block, state_map)),
        out_shape=(jax.ShapeDtypeStruct((t, RET_WIDTH), BF16),
                   jax.ShapeDtypeStruct((n_seq, RET_HEADS, HEAD_DIM, HEAD_DIM), F32)),
        scratch_shapes=scratch,
        compiler_params=_params(*sem),
        name=name,
    )(*args)


def _outproj_kernel(x_ref, a_ref, b_ref, wa_ref, wb_ref, o_ref, *w_bf16, cast):
    if cast:
        (wo_ref,) = w_bf16

        @pl.when(pl.program_id(0) == 0)
        def _():
            wo_ref[:S5_WIDTH, :] = wa_ref[...].astype(BF16)
            wo_ref[S5_WIDTH:, :] = wb_ref[...].astype(BF16)

        wa, wb = wo_ref[:S5_WIDTH, :], wo_ref[S5_WIDTH:, :]
    else:
        wa, wb = wa_ref[...], wb_ref[...]
    o_ref[...] = (x_ref[...]
                  + jnp.dot(a_ref[...], wa, preferred_element_type=F32)
                  + jnp.dot(b_ref[...], wb, preferred_element_type=F32))


def _outproj(x2d, a, b, w, tm=512):
    t = x2d.shape[0]
    cast = w.dtype == F32
    resident = pl.Buffered(1)
    out_specs = [pl.BlockSpec((tm, D_MODEL), lambda i: (i, 0))]
    out_shape = [jax.ShapeDtypeStruct((t, D_MODEL), F32)]
    if cast:
        out_specs.append(pl.BlockSpec((S5_WIDTH + RET_WIDTH, D_MODEL), lambda i: (0, 0)))
        out_shape.append(jax.ShapeDtypeStruct((S5_WIDTH + RET_WIDTH, D_MODEL), BF16))
    return pl.pallas_call(
        functools.partial(_outproj_kernel, cast=cast),
        grid=(t // tm,),
        in_specs=[
            pl.BlockSpec((tm, D_MODEL), lambda i: (i, 0)),
            pl.BlockSpec((tm, S5_WIDTH), lambda i: (i, 0)),
            pl.BlockSpec((tm, RET_WIDTH), lambda i: (i, 0)),
            pl.BlockSpec((S5_WIDTH, D_MODEL), lambda i: (0, 0), pipeline_mode=resident),
            pl.BlockSpec((RET_WIDTH, D_MODEL), lambda i: (1, 0), pipeline_mode=resident),
        ],
        out_specs=out_specs,
        out_shape=out_shape,
        compiler_params=_params("arbitrary"),
        name="out_proj_cast" if cast else "out_proj",
    )(x2d, a, b, w, w)


def _ffn_kernel(x_ref, gn_ref, wg_ref, wu_ref, wo_ref, gf_ref, o_ref, *rest, cast):
    if cast:
        wgb_ref, wub_ref, wob_ref, h_scr = rest
        wgb_ref[...] = wg_ref[...].astype(BF16)
        wub_ref[...] = wu_ref[...].astype(BF16)
        wob_ref[...] = wo_ref[...].astype(BF16)
        wg_ref, wu_ref, wo_ref = wgb_ref, wub_ref, wob_ref
    else:
        (h_scr,) = rest
    j = pl.program_id(1)
    last_j = pl.num_programs(1) - 1

    def step(first, last):
        for r in range(x_ref.shape[0] // FFN_ROWS):
            rows = slice(r * FFN_ROWS, (r + 1) * FFN_ROWS)
            if first:
                h_scr[rows, :] = _rms(x_ref[rows, :], gn_ref[...]).astype(BF16)
            h = h_scr[rows, :]
            gate = jnp.dot(h, wg_ref[...], preferred_element_type=F32)
            up = jnp.dot(h, wu_ref[...], preferred_element_type=F32)
            act = (jax.nn.silu(gate) * up).astype(BF16)
            acc = (x_ref if first else o_ref)[rows, :] + jnp.dot(act, wo_ref[...],
                                                                  preferred_element_type=F32)
            o_ref[rows, :] = _rms(acc, gf_ref[...]) if last else acc

    pl.when(j == 0)(lambda: step(True, False))
    pl.when((j > 0) & (j < last_j))(lambda: step(False, False))
    pl.when(j == last_j)(lambda: step(False, True))


def _ffn(x2d, g_ffn, w_gate, w_up, w_down, g_final, tm=1024, tf=512):
    t = x2d.shape[0]
    nf = D_FF // tf
    cast = w_down.dtype == F32
    (wg, g0), (wu, u0) = w_gate, w_up
    g0, u0 = g0 * nf, u0 * nf
    x_mode = {}
    out_specs = [pl.BlockSpec((tm, D_MODEL), lambda i, j: (i, 0))]
    out_shape = [jax.ShapeDtypeStruct((t, D_MODEL), F32)]
    if cast:
        assert t == tm, "the bf16 weight outputs are written once per row tile"
        x_mode = dict(pipeline_mode=pl.Buffered(1))
        out_specs += [pl.BlockSpec((D_MODEL, tf), lambda i, j: (0, j)),
                      pl.BlockSpec((D_MODEL, tf), lambda i, j: (0, j)),
                      pl.BlockSpec((tf, D_MODEL), lambda i, j: (j, 0))]
        out_shape += [jax.ShapeDtypeStruct((D_MODEL, D_FF), BF16),
                      jax.ShapeDtypeStruct((D_MODEL, D_FF), BF16),
                      jax.ShapeDtypeStruct((D_FF, D_MODEL), BF16)]
    return pl.pallas_call(
        functools.partial(_ffn_kernel, cast=cast),
        grid=(t // tm, nf),
        in_specs=[
            pl.BlockSpec((tm, D_MODEL), lambda i, j: (i, 0), **x_mode),
            pl.BlockSpec((1, D_MODEL), lambda i, j: (0, 0)),
            pl.BlockSpec((D_MODEL, tf), lambda i, j: (0, j + g0)),
            pl.BlockSpec((D_MODEL, tf), lambda i, j: (0, j + u0)),
            pl.BlockSpec((tf, D_MODEL), lambda i, j: (j, 0)),
            pl.BlockSpec((1, D_MODEL), lambda i, j: (0, 0)),
        ],
        out_specs=out_specs,
        out_shape=out_shape,
        scratch_shapes=[pltpu.VMEM((tm, D_MODEL), BF16)],
        compiler_params=_params("arbitrary", "arbitrary"),
        name="ffn_cast" if cast else "ffn",
    )(x2d, g_ffn, wg, wu, w_down, g_final)


def _finish(x2d, proj, s5_out, ret_state, pos0, w, w_out, ffn_w, ffn_tf, n, l):
    ret_out, ret_new = _retention(proj, ret_state, w["gn_w"], n, l, pos0)
    x1, *w_out_b = _outproj(x2d, s5_out, ret_out, w_out)
    y, *ffn_b = _ffn(x1, w["norm_ffn"], *ffn_w, w["norm_final"], tf=ffn_tf)
    return y.reshape(n, l, D_MODEL), ret_new, w_out_b, ffn_b


def _as_groups(re, im, n):
    return re.reshape(n, S5_GROUPS, S5_STATE), im.reshape(n, S5_GROUPS, S5_STATE)


def _sample_layer(x, s5_state, ret_state, w, big):
    n, l, _ = x.shape
    x2d = x.reshape(n * l, D_MODEL)
    proj, w_in_b, u_t = _inproj_cast(x2d, w["norm_mix"], big["w_in"], l)
    s5_out, s5_re, s5_im, w_glu_b = _s5_step(u_t, s5_state, w["pairs"], w["wb"], w["cd"], w["d"],
                                             big["w_glu"], w["b_glu"])
    ffn_f32 = ((big["w_ffn_in"], 0), (big["w_ffn_in"], 1), big["w_ffn_out"])
    y, ret_new, (w_out_b,), (wg_b, wu_b, wd_b) = _finish(
        x2d, proj, s5_out, ret_state, float(PAST_LEN), w, big["w_out"], ffn_f32, 256, n, l)
    big_b = dict(w_in=w_in_b, w_glu=w_glu_b, w_out=w_out_b, ffn=((wg_b, 0), (wu_b, 0), wd_b))
    return (y, *_as_groups(s5_re, s5_im, n), ret_new), big_b


def _prompt_layer(x, w, big_b):
    n, l, _ = x.shape
    x2d = x.reshape(n * l, D_MODEL)
    proj, u_t = _inproj_seq(x2d, w["norm_mix"], big_b["w_in"], n)
    s5_out, s5_re, s5_im = _s5_seq(u_t, w["pairs"], w["wb"], w["cd"], w["d"], big_b["w_glu"],
                                   w["b_glu"], n, l)
    y, ret_new, _, _ = _finish(x2d, proj, s5_out, None, 0.0, w, big_b["w_out"], big_b["ffn"],
                               512, n, l)
    return (y, *_as_groups(s5_re, s5_im, n), ret_new)


def kernel(x_prompt, x_sample, state_s5_re, state_s5_im, state_ret, norm_mix, w_in, s5_lambda_re, s5_lambda_im, s5_log_step, s5_b_re, s5_b_im, s5_c_re, s5_c_im, s5_d, s5_w_glu, s5_b_glu, ret_gn_w, w_out, norm_ffn, w_ffn_in, w_ffn_out, norm_final):
    assert norm_mix.shape[0] == 1, "single-layer stack"
    pairs, wb, cd = _s5_params(s5_lambda_re[0], s5_lambda_im[0], s5_log_step[0],
                               s5_b_re[0], s5_b_im[0], s5_c_re[0], s5_c_im[0])
    w = dict(norm_mix=norm_mix, pairs=pairs, wb=wb, cd=cd, d=s5_d, b_glu=s5_b_glu,
             gn_w=ret_gn_w, norm_ffn=norm_ffn, norm_final=norm_final.reshape(1, D_MODEL))
    big = dict(w_in=w_in[0], w_glu=s5_w_glu[0], w_out=w_out[0],
               w_ffn_in=w_ffn_in[0], w_ffn_out=w_ffn_out[0])
    n_s = x_sample.shape[0]
    (ys, s_re, s_im, s_ret), big_b = _sample_layer(
        x_sample,
        (state_s5_re[0].reshape(n_s, N_STATE), state_s5_im[0].reshape(n_s, N_STATE)),
        state_ret[0], w, big)
    yp, p_re, p_im, p_ret = _prompt_layer(x_prompt, w, big_b)
    return (yp, ys, p_re[None], p_im[None], p_ret[None], s_re[None], s_im[None], s_ret[None])
```

```python
import functools
import math

import jax
import jax.numpy as jnp
import numpy as np
from jax import lax
from jax.experimental import pallas as pl
from jax.experimental.pallas import tpu as pltpu

F32 = jnp.float32
BF16 = jnp.bfloat16

D_MODEL = 2048
S5_WIDTH = 1024
S5_GROUP = 16
S5_GROUPS = 64
S5_STATE = 64
N_STATE = S5_GROUPS * S5_STATE
RET_WIDTH = 1024
RET_HEADS = 8
HEAD_DIM = 128
ROPE_BASE = 10000.0
D_FF = 5632
IN_WIDTH = S5_WIDTH + 4 * RET_WIDTH
NORM_EPS = 1e-6
PAST_LEN = 16384

SUBLANES = 8
SLAB_GROUPS = 8
N_SLABS = S5_GROUPS // SLAB_GROUPS
SLAB_IN = SLAB_GROUPS * S5_GROUP
SLAB_STATE = SLAB_GROUPS * S5_STATE
RET_TILE = 128
ROTARY_TILES = (1, 2)
FFN_ROWS = 512

VMEM_LIMIT_BYTES = 60 * 1024 * 1024


def _params(*sem):
    return pltpu.CompilerParams(dimension_semantics=sem, vmem_limit_bytes=VMEM_LIMIT_BYTES)


def _rms(xf, g):
    ms = jnp.mean(xf * xf, axis=-1, keepdims=True)
    return xf * lax.rsqrt(ms + NORM_EPS) * g


def _rotary(x, cos2, sin2):
    return x * cos2 + pltpu.roll(x, HEAD_DIM // 2, 1) * sin2


def _rotary_heads(x, cos2, sin2):
    return jnp.concatenate(
        [_rotary(x[:, h * HEAD_DIM:(h + 1) * HEAD_DIM], cos2, sin2) for h in range(RET_HEADS)],
        axis=1)


def _rotary_tables(pos0, rows, reps):
    half = HEAD_DIM // 2
    inv_freq = ROPE_BASE ** (-np.arange(half, dtype=np.float64) / half)
    pos = pos0 + np.arange(rows, dtype=np.float64)
    ang = pos[:, None] * inv_freq[None, :]
    cos, sin = np.cos(ang), np.sin(ang)
    cos2 = np.concatenate([cos, cos], axis=-1)
    sin2 = np.concatenate([-sin, sin], axis=-1)
    return (jnp.asarray(np.tile(cos2, (reps, 1)), dtype=F32),
            jnp.asarray(np.tile(sin2, (reps, 1)), dtype=F32))


def _inproj_seq_kernel(x_ref, g_ref, cos_ref, sin_ref, w_ref, o_ref, ou_ref, h_scr, *, tn, n_seq):
    seq = pl.program_id(1)
    tm = x_ref.shape[0]
    h_scr[...] = _rms(x_ref[...], g_ref[...]).astype(BF16)
    for j in range(IN_WIDTH // tn):
        cols = slice(j * tn, (j + 1) * tn)
        res = jnp.dot(h_scr[...], w_ref[:, cols], preferred_element_type=F32)
        if j * tn < S5_WIDTH:
            dst = pl.ds(seq, tm, stride=n_seq)
            for s in range(tn // SLAB_IN):
                ou_ref[j * (tn // SLAB_IN) + s, dst, :] = res[:, s * SLAB_IN:(s + 1) * SLAB_IN]
        if j in ROTARY_TILES:
            res = _rotary_heads(res, cos_ref[...], sin_ref[...])
        o_ref[:, cols] = res.astype(o_ref.dtype)


def _inproj_seq(x2d, g, w, n_seq, pos0, tm=256, tn=RET_WIDTH):
    t = x2d.shape[0]
    tiles = t // n_seq // tm
    row = lambda i, n: (n * tiles + i, 0)
    const = lambda i, n: (0, 0)
    cos2, sin2 = _rotary_tables(pos0, t // n_seq, 1)
    return pl.pallas_call(
        functools.partial(_inproj_seq_kernel, tn=tn, n_seq=n_seq),
        grid=(tiles, n_seq),
        in_specs=[
            pl.BlockSpec((tm, D_MODEL), row),
            pl.BlockSpec((1, D_MODEL), const),
            pl.BlockSpec((tm, HEAD_DIM), lambda i, n: (i, 0)),
            pl.BlockSpec((tm, HEAD_DIM), lambda i, n: (i, 0)),
            pl.BlockSpec((D_MODEL, IN_WIDTH), const, pipeline_mode=pl.Buffered(1)),
        ],
        out_specs=[pl.BlockSpec((tm, IN_WIDTH), row),
                   pl.BlockSpec((N_SLABS, n_seq * tm, SLAB_IN), lambda i, n: (0, i, 0))],
        out_shape=[jax.ShapeDtypeStruct((t, IN_WIDTH), BF16),
                   jax.ShapeDtypeStruct((N_SLABS, t, SLAB_IN), F32)],
        scratch_shapes=[pltpu.VMEM((tm, D_MODEL), BF16)],
        compiler_params=_params("arbitrary", "arbitrary"),
        name="in_proj_seq",
    )(x2d, g, cos2, sin2, w)


def _inproj_cast_kernel(x_ref, g_ref, cos_ref, sin_ref, w_ref, o_ref, wb_ref, ou_ref, h_scr, ru_s):
    j = pl.program_id(0)

    @pl.when(j == 0)
    def _():
        h_scr[...] = _rms(x_ref[...], g_ref[...]).astype(BF16)

    wb_ref[...] = w_ref[...].astype(BF16)
    res = jnp.dot(h_scr[...], wb_ref[...], preferred_element_type=F32)
    is_rotary = functools.reduce(jnp.logical_or, [j == r for r in ROTARY_TILES])

    @pl.when(is_rotary)
    def _():
        o_ref[...] = _rotary_heads(res, cos_ref[...], sin_ref[...]).astype(o_ref.dtype)

    @pl.when(jnp.logical_not(is_rotary))
    def _():
        o_ref[...] = res.astype(o_ref.dtype)

    @pl.when(j == 0)
    def _():
        seq_len, n_seq = ou_ref.shape[1], ou_ref.shape[2]
        for s in range(N_SLABS):
            ru_s[s] = res[:, s * SLAB_IN:(s + 1) * SLAB_IN]
        for s in range(N_SLABS):
            for t in range(seq_len):
                ou_ref[s, t] = ru_s[s, pl.ds(t, n_seq, stride=seq_len), :]


def _inproj_cast(x2d, g, w_f32, seq_len, pos0):
    t = x2d.shape[0]
    tn = RET_WIDTH
    cos2, sin2 = _rotary_tables(pos0, seq_len, t // seq_len)
    return pl.pallas_call(
        _inproj_cast_kernel,
        grid=(IN_WIDTH // tn,),
        in_specs=[
            pl.BlockSpec((t, D_MODEL), lambda j: (0, 0), pipeline_mode=pl.Buffered(1)),
            pl.BlockSpec((1, D_MODEL), lambda j: (0, 0)),
            pl.BlockSpec((t, HEAD_DIM), lambda j: (0, 0)),
            pl.BlockSpec((t, HEAD_DIM), lambda j: (0, 0)),
            pl.BlockSpec((D_MODEL, tn), lambda j: (0, j)),
        ],
        out_specs=[pl.BlockSpec((t, tn), lambda j: (0, j)),
                   pl.BlockSpec((D_MODEL, tn), lambda j: (0, j)),
                   pl.BlockSpec((N_SLABS, seq_len, t // seq_len, SLAB_IN), lambda j: (0, 0, 0, 0))],
        out_shape=[jax.ShapeDtypeStruct((t, IN_WIDTH), BF16),
                   jax.ShapeDtypeStruct((D_MODEL, IN_WIDTH), BF16),
                   jax.ShapeDtypeStruct((N_SLABS, seq_len, t // seq_len, SLAB_IN), F32)],
        scratch_shapes=[pltpu.VMEM((t, D_MODEL), BF16), pltpu.VMEM((N_SLABS, t, SLAB_IN), F32)],
        compiler_params=_params("arbitrary"),
        name="in_proj_cast",
    )(x2d, g, cos2, sin2, w_f32)


def _s5_param_kernel(lr_ref, li_ref, ls_ref, br_ref, bi_ref, ctr_ref, cti_ref,
                     pairr_ref, pairi_ref, wbr_ref, wbi_ref, cdr_ref, cdi_ref):
    lr = lr_ref[...]
    li = li_ref[...]
    dt = jnp.exp(ls_ref[...])
    mag = jnp.exp(lr * dt)
    ar = mag * jnp.cos(li * dt)
    ai = mag * jnp.sin(li * dt)
    den = lr * lr + li * li
    nr = ar - 1.0
    cr = (nr * lr + ai * li) / den
    ci = (ai * lr - nr * li) / den
    br = br_ref[...]
    bi = bi_ref[...]
    bbr = cr * br - ci * bi
    bbi = cr * bi + ci * br
    abr = ar * bbr - ai * bbi
    abi = ar * bbi + ai * bbr

    in_shape = (SLAB_IN, SLAB_STATE)
    same_in = (lax.broadcasted_iota(jnp.int32, in_shape, 0) // S5_GROUP
               == lax.broadcasted_iota(jnp.int32, in_shape, 1) // S5_STATE)
    out_shape = (SLAB_STATE, SLAB_IN)
    same_out = (lax.broadcasted_iota(jnp.int32, out_shape, 0) // S5_STATE
                == lax.broadcasted_iota(jnp.int32, out_shape, 1) // S5_GROUP)
    for s in range(N_SLABS):
        cs = slice(s * SLAB_STATE, (s + 1) * SLAB_STATE)
        for dst, top, bot in ((wbr_ref, bbr, abr), (wbi_ref, bbi, abi)):
            for k, part in enumerate((top, bot)):
                blk = jnp.tile(part[:, cs], (SLAB_GROUPS, 1))
                dst[s, k * SLAB_IN:(k + 1) * SLAB_IN, :] = jnp.where(same_in, blk, 0.0).astype(BF16)
        cdr_ref[s] = jnp.where(same_out, ctr_ref[cs, :], 0.0).astype(BF16)
        cdi_ref[s] = jnp.where(same_out, cti_ref[cs, :], 0.0).astype(BF16)

    rows = lax.broadcasted_iota(jnp.int32, (SUBLANES, N_STATE), 0)
    second = rows >= SUBLANES // 2
    full = lambda v: jnp.broadcast_to(v, (SUBLANES, N_STATE))
    pairr_ref[...] = jnp.where(second, full(ar * ar - ai * ai), full(ar))
    pairi_ref[...] = jnp.where(second, full(ar * ai + ai * ar), full(ai))


def _s5_params(lam_re, lam_im, log_step, b_re, b_im, c_re, c_im):
    lr = lam_re.reshape(1, N_STATE)
    li = lam_im.reshape(1, N_STATE)
    ls = jnp.repeat(log_step, S5_STATE).reshape(1, N_STATE)
    b_rows = lambda b: jnp.transpose(b, (2, 0, 1)).reshape(S5_GROUP, N_STATE)
    c_cols = lambda c: jnp.tile(jnp.transpose(c, (0, 2, 1)).reshape(N_STATE, S5_GROUP),
                                (1, SLAB_GROUPS))
    pair = jax.ShapeDtypeStruct((SUBLANES, N_STATE), F32)
    wb = jax.ShapeDtypeStruct((N_SLABS, 2 * SLAB_IN, SLAB_STATE), BF16)
    cd = jax.ShapeDtypeStruct((N_SLABS, SLAB_STATE, SLAB_IN), BF16)
    pairr, pairi, wbr, wbi, cdr, cdi = pl.pallas_call(
        _s5_param_kernel,
        out_shape=(pair, pair, wb, wb, cd, cd),
        name="s5_params",
    )(lr, li, ls, b_rows(b_re), b_rows(b_im), c_cols(c_re), c_cols(c_im))
    return (pairr, pairi), (wbr, wbi), (cdr, cdi)


def _cmul_add(xr, xi, pr, pi, vr, vi):
    return xr + (pr * vr - pi * vi), xi + (pr * vi + pi * vr)


def _s5_weight_specs(k_in):
    const3 = lambda *_: (0, 0, 0)
    const2 = lambda *_: (0, 0)
    return [
        pl.BlockSpec((SUBLANES, N_STATE), const2),
        pl.BlockSpec((SUBLANES, N_STATE), const2),
        pl.BlockSpec((N_SLABS, k_in, SLAB_STATE), const3),
        pl.BlockSpec((N_SLABS, k_in, SLAB_STATE), const3),
        pl.BlockSpec((N_SLABS, SLAB_STATE, SLAB_IN), const3),
        pl.BlockSpec((N_SLABS, SLAB_STATE, SLAB_IN), const3),
        pl.BlockSpec((1, S5_WIDTH), const2),
        pl.BlockSpec((S5_WIDTH, S5_WIDTH), const2),
        pl.BlockSpec((1, S5_WIDTH), const2),
    ]


def _s5_step_kernel(u_ref, x0r_ref, x0i_ref, pairr_ref, pairi_ref, wbr_ref, wbi_ref, cdr_ref, cdi_ref,
                    d_ref, wg_ref, bg_ref, o_ref, xlr_ref, xli_ref, wgb_ref, xr_s, xi_s, y_s, o_s):
    seq_len, n_seq = u_ref.shape[1], u_ref.shape[2]

    @pl.when(pl.program_id(0) == 0)
    def _():
        wgb_ref[...] = wg_ref[...].astype(BF16)

    for s in range(N_SLABS):
        cs = slice(s * SLAB_STATE, (s + 1) * SLAB_STATE)
        us = slice(s * SLAB_IN, (s + 1) * SLAB_IN)
        u = u_ref[s].reshape(seq_len * n_seq, SLAB_IN)
        ub = u.astype(BF16)
        xr_s[:, cs] = jnp.dot(ub, wbr_ref[s], preferred_element_type=F32)
        xi_s[:, cs] = jnp.dot(ub, wbi_ref[s], preferred_element_type=F32)

        a_r = jnp.broadcast_to(pairr_ref[0:1, cs], (SUBLANES, SLAB_STATE))
        a_i = jnp.broadcast_to(pairi_ref[0:1, cs], (SUBLANES, SLAB_STATE))
        for b in range(n_seq // SUBLANES):
            seqs = slice(b * SUBLANES, (b + 1) * SUBLANES)
            xr, xi = x0r_ref[seqs, cs], x0i_ref[seqs, cs]
            for t in range(seq_len):
                rows = slice(t * n_seq + b * SUBLANES, t * n_seq + (b + 1) * SUBLANES)
                xr, xi = _cmul_add(xr_s[rows, cs], xi_s[rows, cs], a_r, a_i, xr, xi)
                xr_s[rows, cs] = xr
                xi_s[rows, cs] = xi
            xlr_ref[seqs, cs] = xr
            xli_ref[seqs, cs] = xi

        y = (jnp.dot(xr_s[:, cs].astype(BF16), cdr_ref[s], preferred_element_type=F32)
             - jnp.dot(xi_s[:, cs].astype(BF16), cdi_ref[s], preferred_element_type=F32))
        y_s[:, us] = jax.nn.gelu(y + d_ref[:, us] * u)

    y = y_s[...]
    z = jnp.dot(y.astype(BF16), wgb_ref[...], preferred_element_type=F32) + bg_ref[...]
    o = y * jax.nn.sigmoid(z)
    for s in range(N_SLABS):
        for t in range(seq_len):
            o_s[s, pl.ds(t, n_seq, stride=seq_len), :] = (
                o[t * n_seq:(t + 1) * n_seq, s * SLAB_IN:(s + 1) * SLAB_IN])
    for s in range(N_SLABS):
        o_ref[:, s * SLAB_IN:(s + 1) * SLAB_IN] = o_s[s].astype(o_ref.dtype)


def _s5_step(u_t, x0, pairs, wb, cd, d, w_glu, b_glu, n_tile=32):
    _, seq_len, n_seq, _ = u_t.shape
    rows = n_tile * seq_len
    row = lambda i: (i, 0)
    st = jax.ShapeDtypeStruct((n_seq, N_STATE), F32)
    return pl.pallas_call(
        _s5_step_kernel,
        grid=(n_seq // n_tile,),
        in_specs=[pl.BlockSpec((N_SLABS, seq_len, n_tile, SLAB_IN), lambda i: (0, 0, i, 0)),
                  pl.BlockSpec((n_tile, N_STATE), row), pl.BlockSpec((n_tile, N_STATE), row)]
        + _s5_weight_specs(SLAB_IN),
        out_specs=(pl.BlockSpec((rows, S5_WIDTH), row), pl.BlockSpec((n_tile, N_STATE), row),
                   pl.BlockSpec((n_tile, N_STATE), row),
                   pl.BlockSpec((S5_WIDTH, S5_WIDTH), lambda i: (0, 0))),
        out_shape=(jax.ShapeDtypeStruct((n_seq * seq_len, S5_WIDTH), BF16), st, st,
                   jax.ShapeDtypeStruct((S5_WIDTH, S5_WIDTH), BF16)),
        scratch_shapes=[pltpu.VMEM((rows, N_STATE), F32), pltpu.VMEM((rows, N_STATE), F32),
                        pltpu.VMEM((rows, S5_WIDTH), F32),
                        pltpu.VMEM((N_SLABS, rows, SLAB_IN), F32)],
        compiler_params=_params("arbitrary"),
        name="s5_step",
    )(u_t, x0[0], x0[1], pairs[0], pairs[1], wb[0], wb[1], cd[0], cd[1], d, w_glu, b_glu)


def _s5_seq_kernel(u_ref, pairr_ref, pairi_ref, wbr_ref, wbi_ref, cdr_ref, cdi_ref, d_ref, wg_ref,
                   bg_ref, o_ref, xlr_ref, xli_ref, xr_s, xi_s, y_s, o_s, cbr_s, cbi_s, ul_s):
    n_rows = u_ref.shape[1]
    half = SUBLANES // 2
    tt = n_rows // half

    @pl.when(pl.program_id(0) == 0)
    def _():
        cbr_s[...] = jnp.zeros_like(cbr_s)
        cbi_s[...] = jnp.zeros_like(cbi_s)
        ul_s[...] = jnp.zeros_like(ul_s)

    first = lax.broadcasted_iota(jnp.int32, (SUBLANES, SLAB_IN), 0) < half
    for s in range(N_SLABS):
        cs = slice(s * SLAB_STATE, (s + 1) * SLAB_STATE)
        us = slice(s * SLAB_IN, (s + 1) * SLAB_IN)
        u = u_ref[s]
        shifted = pltpu.roll(u, half, 0)
        head = jnp.where(first, pltpu.roll(ul_s[s], half, 0), shifted[:SUBLANES])
        u_prev = jnp.concatenate([head, shifted[SUBLANES:]], axis=0)
        ul_s[s] = u[n_rows - SUBLANES:]
        ub = jnp.concatenate([u, u_prev], axis=1).astype(BF16)
        xr_s[:, cs] = jnp.dot(ub, wbr_ref[s], preferred_element_type=F32)
        xi_s[:, cs] = jnp.dot(ub, wbi_ref[s], preferred_element_type=F32)

        pr = jnp.broadcast_to(pairr_ref[half:half + 1, cs], (SUBLANES, SLAB_STATE))
        pi = jnp.broadcast_to(pairi_ref[half:half + 1, cs], (SUBLANES, SLAB_STATE))
        cr, ci = cbr_s[:, cs], cbi_s[:, cs]
        for b in range(n_rows // SUBLANES):
            rows = slice(b * SUBLANES, (b + 1) * SUBLANES)
            cr, ci = _cmul_add(xr_s[rows, cs], xi_s[rows, cs], pr, pi, cr, ci)
            xr_s[rows, cs] = cr
            xi_s[rows, cs] = ci
        cbr_s[:, cs] = cr
        cbi_s[:, cs] = ci

        y = (jnp.dot(xr_s[:, cs].astype(BF16), cdr_ref[s], preferred_element_type=F32)
             - jnp.dot(xi_s[:, cs].astype(BF16), cdi_ref[s], preferred_element_type=F32))
        y_s[:, us] = jax.nn.gelu(y + d_ref[:, us] * u)

    xlr_ref[...] = cbr_s[...]
    xli_ref[...] = cbi_s[...]

    y = y_s[...]
    z = jnp.dot(y.astype(BF16), wg_ref[...], preferred_element_type=F32) + bg_ref[...]
    o = y * jax.nn.sigmoid(z)
    for s in range(N_SLABS):
        o_s[s] = o[:, s * SLAB_IN:(s + 1) * SLAB_IN]
    for n in range(half):
        for s in range(N_SLABS):
            o_ref[n, :, s * SLAB_IN:(s + 1) * SLAB_IN] = (
                o_s[s, pl.ds(n, tt, stride=half), :].astype(o_ref.dtype))


def _s5_seq(u_t, pairs, wb, cd, d, w_glu, b_glu, n_seq, seq_len, tt=64):
    assert 2 * n_seq == SUBLANES
    n_rows = n_seq * tt
    st = jax.ShapeDtypeStruct((SUBLANES, N_STATE), F32)
    y, xlr, xli = pl.pallas_call(
        _s5_seq_kernel,
        grid=(seq_len // tt,),
        in_specs=[pl.BlockSpec((N_SLABS, n_rows, SLAB_IN), lambda i: (0, i, 0))]
        + _s5_weight_specs(2 * SLAB_IN),
        out_specs=(pl.BlockSpec((n_seq, tt, S5_WIDTH), lambda i: (0, i, 0)),
                   pl.BlockSpec((SUBLANES, N_STATE), lambda i: (0, 0)),
                   pl.BlockSpec((SUBLANES, N_STATE), lambda i: (0, 0))),
        out_shape=(jax.ShapeDtypeStruct((n_seq, seq_len, S5_WIDTH), BF16), st, st),
        scratch_shapes=[pltpu.VMEM((n_rows, N_STATE), F32), pltpu.VMEM((n_rows, N_STATE), F32),
                        pltpu.VMEM((n_rows, S5_WIDTH), F32),
                        pltpu.VMEM((N_SLABS, n_rows, SLAB_IN), F32),
                        pltpu.VMEM((SUBLANES, N_STATE), F32), pltpu.VMEM((SUBLANES, N_STATE), F32),
                        pltpu.VMEM((N_SLABS, SUBLANES, SLAB_IN), F32)],
        compiler_params=_params("arbitrary"),
        name="s5_seq",
    )(u_t, pairs[0], pairs[1], wb[0], wb[1], cd[0], cd[1], d, w_glu, b_glu)
    return y.reshape(n_seq * seq_len, S5_WIDTH), xlr[n_seq:], xli[n_seq:]


_NT = (((1,), (1,)), ((), ()))
_TN = (((0,), (0,)), ((), ()))


def _norm_gate(o, g, gn_w):
    mu = jnp.mean(o, axis=-1, keepdims=True)
    oc = o - mu
    var = jnp.mean(oc * oc, axis=-1, keepdims=True)
    return jax.nn.silu(g) * (oc * lax.rsqrt(var + NORM_EPS) * gn_w)


def _ret_seq_kernel(q_ref, k_ref, v_ref, g_ref, mask_ref, qd_ref, kd_ref, cd_ref,
                    gnw_ref, o_ref, rn_ref, r_s, *, chunks):
    @pl.when(pl.program_id(1) == 0)
    def _():
        r_s[...] = jnp.zeros_like(r_s)

    for c in range(chunks):
        rows = slice(c * RET_TILE, (c + 1) * RET_TILE)
        for h in range(RET_HEADS):
            hs = slice(h * HEAD_DIM, (h + 1) * HEAD_DIM)
            qb, kb, vb = q_ref[rows, hs], k_ref[rows, hs], v_ref[rows, hs]
            sc = lax.dot_general(qb, kb, _NT, preferred_element_type=F32) * mask_ref[h]
            r_old = r_s[h]
            q_dec = (qb.astype(F32) * qd_ref[h]).astype(BF16)
            k_dec = (kb.astype(F32) * kd_ref[h]).astype(BF16)
            lhs = jnp.concatenate([sc.astype(BF16), q_dec], axis=1)
            rhs = jnp.concatenate([vb, r_old.astype(BF16)], axis=0)
            o = jnp.dot(lhs, rhs, preferred_element_type=F32)
            r_s[h] = r_old * cd_ref[h] + lax.dot_general(k_dec, vb, _TN,
                                                          preferred_element_type=F32)
            o_ref[rows, hs] = _norm_gate(o, g_ref[rows, hs].astype(F32),
                                         gnw_ref[:, hs]).astype(o_ref.dtype)
    rn_ref[0] = r_s[...]


def _ret_step_kernel(q_ref, k_ref, v_ref, g_ref, mask_ref, qd_ref, kd_ref, cd_ref,
                     gnw_ref, r0_ref, o_ref, rn_ref, qr_s, kdt_s, ob_s, *, n_seq):
    rows_per_seq = RET_TILE // n_seq
    for h in range(RET_HEADS):
        hs = slice(h * HEAD_DIM, (h + 1) * HEAD_DIM)
        qb, kb = q_ref[:, hs], k_ref[:, hs]
        sc = lax.dot_general(qb, kb, _NT, preferred_element_type=F32) * mask_ref[h]
        ob_s[:, hs] = jnp.dot(sc.astype(BF16), v_ref[:, hs], preferred_element_type=F32)
        qr_s[:, hs] = qb.astype(F32)
        kdt_s[h] = (kb.astype(F32) * kd_ref[h]).T

    lane = lax.broadcasted_iota(jnp.int32, (HEAD_DIM, RET_TILE), 1)

    def per_seq(s, c):
        r0 = pl.multiple_of(s * rows_per_seq, rows_per_seq)
        rows = pl.ds(r0, rows_per_seq)
        in_seq = (lane >= r0) & (lane < r0 + rows_per_seq)
        for h in range(RET_HEADS):
            hs = slice(h * HEAD_DIM, (h + 1) * HEAD_DIM)
            r_old = r0_ref[s, h]
            cross = jnp.dot(qr_s[rows, hs].astype(BF16), r_old.astype(BF16), preferred_element_type=F32)
            ob_s[rows, hs] = ob_s[rows, hs] + cross * qd_ref[h, rows, :]
            kdt = jnp.where(in_seq, kdt_s[h], 0.0).astype(BF16)
            rn_ref[s, h] = r_old * cd_ref[h] + jnp.dot(kdt, v_ref[:, hs], preferred_element_type=F32)
        return c

    lax.fori_loop(0, n_seq, per_seq, 0)

    for h in range(RET_HEADS):
        hs = slice(h * HEAD_DIM, (h + 1) * HEAD_DIM)
        o_ref[:, hs] = _norm_gate(ob_s[:, hs], g_ref[:, hs].astype(F32),
                                  gnw_ref[:, hs]).astype(o_ref.dtype)


def _decay_tables(chunk, n_seq):
    scale = HEAD_DIM ** -0.5
    log_gamma = np.log(1.0 - 2.0 ** (-5.0 - np.arange(RET_HEADS, dtype=np.float64)))
    idx = np.arange(chunk, dtype=np.float64)
    diff = idx[:, None] - idx[None, :]
    mask = np.where(diff >= 0, np.exp(log_gamma[:, None, None] * np.maximum(diff, 0.0)), 0.0)
    q_decay = np.exp(log_gamma[:, None] * (idx + 1.0))
    k_decay = np.exp(log_gamma[:, None] * (chunk - 1.0 - idx)) * scale
    chunk_decay = np.exp(log_gamma * chunk)
    mask_t = scale * np.einsum("hab,st->hsatb", mask, np.eye(n_seq)).reshape(
        RET_HEADS, RET_TILE, RET_TILE)
    qd_t = np.broadcast_to(np.tile(q_decay, (1, n_seq))[:, :, None], (RET_HEADS, RET_TILE, HEAD_DIM))
    kd_t = np.broadcast_to(np.tile(k_decay, (1, n_seq))[:, :, None], (RET_HEADS, RET_TILE, HEAD_DIM))
    const = lambda a: jnp.asarray(np.ascontiguousarray(a), dtype=F32)
    return const(mask_t), const(qd_t), const(kd_t), const(chunk_decay)


def _retention(proj, r0, gn_w, n_seq, seq_len, chunks=4):
    t = proj.shape[0]
    const3 = lambda *_: (0, 0, 0)
    if r0 is None:
        tile, tile_seqs = chunks * RET_TILE, 1
        steps = seq_len // tile
        grid = (n_seq, steps)
        row = lambda n, c: n * steps + c
        state_map = lambda n, c: (n, 0, 0, 0)
        state_block = (1, RET_HEADS, HEAD_DIM, HEAD_DIM)
        sem = ("arbitrary", "arbitrary")
        body = functools.partial(_ret_seq_kernel, chunks=chunks)
        scratch = [pltpu.VMEM((RET_HEADS, HEAD_DIM, HEAD_DIM), F32)]
        name = "ret_seq"
    else:
        tile, tile_seqs = RET_TILE, RET_TILE // seq_len
        grid = (t // tile,)
        row = lambda i: i
        state_map = lambda i: (i, 0, 0, 0)
        state_block = (tile_seqs, RET_HEADS, HEAD_DIM, HEAD_DIM)
        sem = ("arbitrary",)
        body = functools.partial(_ret_step_kernel, n_seq=tile_seqs)
        scratch = [pltpu.VMEM((RET_TILE, RET_WIDTH), F32),
                   pltpu.VMEM((RET_HEADS, HEAD_DIM, RET_TILE), F32),
                   pltpu.VMEM((RET_TILE, RET_WIDTH), F32)]
        name = "ret_step"
    mask_t, qd_t, kd_t, cd = _decay_tables(RET_TILE // tile_seqs, tile_seqs)

    def col(cb):
        return pl.BlockSpec((tile, RET_WIDTH), lambda *a: (row(*a), cb))

    in_specs = [
        col(1), col(2), col(3), col(4),
        pl.BlockSpec((RET_HEADS, RET_TILE, RET_TILE), const3),
        pl.BlockSpec((RET_HEADS, RET_TILE, HEAD_DIM), const3),
        pl.BlockSpec((RET_HEADS, RET_TILE, HEAD_DIM), const3),
        pl.BlockSpec(memory_space=pltpu.SMEM),
        pl.BlockSpec((1, RET_WIDTH), lambda *_: (0, 0)),
    ]
    args = [proj, proj, proj, proj, mask_t, qd_t, kd_t, cd, gn_w]
    if r0 is not None:
        in_specs.append(pl.BlockSpec(state_block, state_map))
        args.append(r0)
    return pl.pallas_call(
        body,
        grid=grid,
        in_specs=in_specs,
        out_specs=(pl.BlockSpec((tile, RET_WIDTH), lambda *a: (row(*a), 0)),
                   pl.BlockSpec(state_block, state_map)),
        out_shape=(jax.ShapeDtypeStruct((t, RET_WIDTH), BF16),
                   jax.ShapeDtypeStruct((n_seq, RET_HEADS, HEAD_DIM, HEAD_DIM), F32)),
        scratch_shapes=scratch,
        compiler_params=_params(*sem),
        name=name,
    )(*args)


def _outproj_kernel(x_ref, a_ref, b_ref, wa_ref, wb_ref, o_ref, *w_bf16, cast):
    if cast:
        (wo_ref,) = w_bf16

        @pl.when(pl.program_id(0) == 0)
        def _():
            wo_ref[:S5_WIDTH, :] = wa_ref[...].astype(BF16)
            wo_ref[S5_WIDTH:, :] = wb_ref[...].astype(BF16)

        wa, wb = wo_ref[:S5_WIDTH, :], wo_ref[S5_WIDTH:, :]
    else:
        wa, wb = wa_ref[...], wb_ref[...]
    o_ref[...] = (x_ref[...]
                  + jnp.dot(a_ref[...], wa, preferred_element_type=F32)
                  + jnp.dot(b_ref[...], wb, preferred_element_type=F32))


def _outproj(x2d, a, b, w, tm=512):
    t = x2d.shape[0]
    cast = w.dtype == F32
    resident = pl.Buffered(1)
    out_specs = [pl.BlockSpec((tm, D_MODEL), lambda i: (i, 0))]
    out_shape = [jax.ShapeDtypeStruct((t, D_MODEL), F32)]
    if cast:
        out_specs.append(pl.BlockSpec((S5_WIDTH + RET_WIDTH, D_MODEL), lambda i: (0, 0)))
        out_shape.append(jax.ShapeDtypeStruct((S5_WIDTH + RET_WIDTH, D_MODEL), BF16))
    return pl.pallas_call(
        functools.partial(_outproj_kernel, cast=cast),
        grid=(t // tm,),
        in_specs=[
            pl.BlockSpec((tm, D_MODEL), lambda i: (i, 0)),
            pl.BlockSpec((tm, S5_WIDTH), lambda i: (i, 0)),
            pl.BlockSpec((tm, RET_WIDTH), lambda i: (i, 0)),
            pl.BlockSpec((S5_WIDTH, D_MODEL), lambda i: (0, 0), pipeline_mode=resident),
            pl.BlockSpec((RET_WIDTH, D_MODEL), lambda i: (1, 0), pipeline_mode=resident),
        ],
        out_specs=out_specs,
        out_shape=out_shape,
        compiler_params=_params("arbitrary"),
        name="out_proj_cast" if cast else "out_proj",
    )(x2d, a, b, w, w)


def _ffn_kernel(x_ref, gn_ref, wg_ref, wu_ref, wo_ref, gf_ref, o_ref, *rest, cast):
    if cast:
        wgb_ref, wub_ref, wob_ref, h_scr = rest
        wgb_ref[...] = wg_ref[...].astype(BF16)
        wub_ref[...] = wu_ref[...].astype(BF16)
        wob_ref[...] = wo_ref[...].astype(BF16)
        wg_ref, wu_ref, wo_ref = wgb_ref, wub_ref, wob_ref
    else:
        (h_scr,) = rest
    j = pl.program_id(1)
    last_j = pl.num_programs(1) - 1

    def step(first, last):
        for r in range(x_ref.shape[0] // FFN_ROWS):
            rows = slice(r * FFN_ROWS, (r + 1) * FFN_ROWS)
            if first:
                h_scr[rows, :] = _rms(x_ref[rows, :], gn_ref[...]).astype(BF16)
            h = h_scr[rows, :]
            gate = jnp.dot(h, wg_ref[...], preferred_element_type=F32)
            up = jnp.dot(h, wu_ref[...], preferred_element_type=F32)
            act = (jax.nn.silu(gate) * up).astype(BF16)
            acc = (x_ref if first else o_ref)[rows, :] + jnp.dot(act, wo_ref[...],
                                                                  preferred_element_type=F32)
            o_ref[rows, :] = _rms(acc, gf_ref[...]) if last else acc

    pl.when(j == 0)(lambda: step(True, False))
    pl.when((j > 0) & (j < last_j))(lambda: step(False, False))
    pl.when(j == last_j)(lambda: step(False, True))


def _ffn(x2d, g_ffn, w_gate, w_up, w_down, g_final, tm=1024, tf=512):
    t = x2d.shape[0]
    nf = D_FF // tf
    cast = w_down.dtype == F32
    (wg, g0), (wu, u0) = w_gate, w_up
    g0, u0 = g0 * nf, u0 * nf
    x_mode = {}
    out_specs = [pl.BlockSpec((tm, D_MODEL), lambda i, j: (i, 0))]
    out_shape = [jax.ShapeDtypeStruct((t, D_MODEL), F32)]
    if cast:
        assert t == tm, "the bf16 weight outputs are written once per row tile"
        x_mode = dict(pipeline_mode=pl.Buffered(1))
        out_specs += [pl.BlockSpec((D_MODEL, tf), lambda i, j: (0, j)),
                      pl.BlockSpec((D_MODEL, tf), lambda i, j: (0, j)),
                      pl.BlockSpec((tf, D_MODEL), lambda i, j: (j, 0))]
        out_shape += [jax.ShapeDtypeStruct((D_MODEL, D_FF), BF16),
                      jax.ShapeDtypeStruct((D_MODEL, D_FF), BF16),
                      jax.ShapeDtypeStruct((D_FF, D_MODEL), BF16)]
    return pl.pallas_call(
        functools.partial(_ffn_kernel, cast=cast),
        grid=(t // tm, nf),
        in_specs=[
            pl.BlockSpec((tm, D_MODEL), lambda i, j: (i, 0), **x_mode),
            pl.BlockSpec((1, D_MODEL), lambda i, j: (0, 0)),
            pl.BlockSpec((D_MODEL, tf), lambda i, j: (0, j + g0)),
            pl.BlockSpec((D_MODEL, tf), lambda i, j: (0, j + u0)),
            pl.BlockSpec((tf, D_MODEL), lambda i, j: (j, 0)),
            pl.BlockSpec((1, D_MODEL), lambda i, j: (0, 0)),
        ],
        out_specs=out_specs,
        out_shape=out_shape,
        scratch_shapes=[pltpu.VMEM((tm, D_MODEL), BF16)],
        compiler_params=_params("arbitrary", "arbitrary"),
        name="ffn_cast" if cast else "ffn",
    )(x2d, g_ffn, wg, wu, w_down, g_final)


def _finish(x2d, proj, s5_out, ret_state, w, w_out, ffn_w, ffn_tf, n, l):
    ret_out, ret_new = _retention(proj, ret_state, w["gn_w"], n, l)
    x1, *w_out_b = _outproj(x2d, s5_out, ret_out, w_out)
    y, *ffn_b = _ffn(x1, w["norm_ffn"], *ffn_w, w["norm_final"], tf=ffn_tf)
    return y.reshape(n, l, D_MODEL), ret_new, w_out_b, ffn_b


def _as_groups(re, im, n):
    return re.reshape(n, S5_GROUPS, S5_STATE), im.reshape(n, S5_GROUPS, S5_STATE)


def _sample_layer(x, s5_state, ret_state, w, big):
    n, l, _ = x.shape
    x2d = x.reshape(n * l, D_MODEL)
    proj, w_in_b, u_t = _inproj_cast(x2d, w["norm_mix"], big["w_in"], l, float(PAST_LEN))
    s5_out, s5_re, s5_im, w_glu_b = _s5_step(u_t, s5_state, w["pairs"], w["wb"], w["cd"], w["d"],
                                             big["w_glu"], w["b_glu"])
    ffn_f32 = ((big["w_ffn_in"], 0), (big["w_ffn_in"], 1), big["w_ffn_out"])
    y, ret_new, (w_out_b,), (wg_b, wu_b, wd_b) = _finish(
        x2d, proj, s5_out, ret_state, w, big["w_out"], ffn_f32, 256, n, l)
    big_b = dict(w_in=w_in_b, w_glu=w_glu_b, w_out=w_out_b, ffn=((wg_b, 0), (wu_b, 0), wd_b))
    return (y, *_as_groups(s5_re, s5_im, n), ret_new), big_b


def _prompt_layer(x, w, big_b):
    n, l, _ = x.shape
    x2d = x.reshape(n * l, D_MODEL)
    proj, u_t = _inproj_seq(x2d, w["norm_mix"], big_b["w_in"], n, 0.0)
    s5_out, s5_re, s5_im = _s5_seq(u_t, w["pairs"], w["wb"], w["cd"], w["d"], big_b["w_glu"],
                                   w["b_glu"], n, l)
    y, ret_new, _, _ = _finish(x2d, proj, s5_out, None, w, big_b["w_out"], big_b["ffn"],
                               512, n, l)
    return (y, *_as_groups(s5_re, s5_im, n), ret_new)


def kernel(x_prompt, x_sample, state_s5_re, state_s5_im, state_ret, norm_mix, w_in, s5_lambda_re, s5_lambda_im, s5_log_step, s5_b_re, s5_b_im, s5_c_re, s5_c_im, s5_d, s5_w_glu, s5_b_glu, ret_gn_w, w_out, norm_ffn, w_ffn_in, w_ffn_out, norm_final):
    assert norm_mix.shape[0] == 1, "single-layer stack"
    pairs, wb, cd = _s5_params(s5_lambda_re[0], s5_lambda_im[0], s5_log_step[0],
                               s5_b_re[0], s5_b_im[0], s5_c_re[0], s5_c_im[0])
    w = dict(norm_mix=norm_mix, pairs=pairs, wb=wb, cd=cd, d=s5_d, b_glu=s5_b_glu,
             gn_w=ret_gn_w, norm_ffn=norm_ffn, norm_final=norm_final.reshape(1, D_MODEL))
    big = dict(w_in=w_in[0], w_glu=s5_w_glu[0], w_out=w_out[0],
               w_ffn_in=w_ffn_in[0], w_ffn_out=w_ffn_out[0])
    n_s = x_sample.shape[0]
    (ys, s_re, s_im, s_ret), big_b = _sample_layer(
        x_sample,
        (state_s5_re[0].reshape(n_s, N_STATE), state_s5_im[0].reshape(n_s, N_STATE)),
        state_ret[0], w, big)
    yp, p_re, p_im, p_ret = _prompt_layer(x_prompt, w, big_b)
    return (yp, ys, p_re[None], p_im[None], p_ret[None], s_re[None], s_im[None], s_ret[None])
```

```python
import functools
import math

import jax
import jax.numpy as jnp
import numpy as np
from jax import lax
from jax.experimental import pallas as pl
from jax.experimental.pallas import tpu as pltpu

F32 = jnp.float32
BF16 = jnp.bfloat16

D_MODEL = 2048
S5_WIDTH = 1024
S5_GROUP = 16
S5_GROUPS = 64
S5_STATE = 64
N_STATE = S5_GROUPS * S5_STATE
RET_WIDTH = 1024
RET_HEADS = 8
HEAD_DIM = 128
ROPE_BASE = 10000.0
D_FF = 5632
IN_WIDTH = S5_WIDTH + 4 * RET_WIDTH
NORM_EPS = 1e-6
PAST_LEN = 16384

SUBLANES = 8
SLAB_GROUPS = 8
N_SLABS = S5_GROUPS // SLAB_GROUPS
SLAB_IN = SLAB_GROUPS * S5_GROUP
SLAB_STATE = SLAB_GROUPS * S5_STATE
RET_TILE = 128
ROTARY_TILES = (1, 2)
FFN_ROWS = 512

VMEM_LIMIT_BYTES = 60 * 1024 * 1024


def _params(*sem):
    return pltpu.CompilerParams(dimension_semantics=sem, vmem_limit_bytes=VMEM_LIMIT_BYTES)


def _rms(xf, g):
    ms = jnp.mean(xf * xf, axis=-1, keepdims=True)
    return xf * lax.rsqrt(ms + NORM_EPS) * g


def _rotary(x, cos2, sin2):
    return x * cos2 + pltpu.roll(x, HEAD_DIM // 2, 1) * sin2


def _rotary_heads(x, cos2, sin2):
    return jnp.concatenate(
        [_rotary(x[:, h * HEAD_DIM:(h + 1) * HEAD_DIM], cos2, sin2) for h in range(RET_HEADS)],
        axis=1)


def _rotary_tables(pos0, rows, reps):
    half = HEAD_DIM // 2
    inv_freq = ROPE_BASE ** (-np.arange(half, dtype=np.float64) / half)
    pos = pos0 + np.arange(rows, dtype=np.float64)
    ang = pos[:, None] * inv_freq[None, :]
    cos, sin = np.cos(ang), np.sin(ang)
    cos2 = np.concatenate([cos, cos], axis=-1)
    sin2 = np.concatenate([-sin, sin], axis=-1)
    return (jnp.asarray(np.tile(cos2, (reps, 1)), dtype=F32),
            jnp.asarray(np.tile(sin2, (reps, 1)), dtype=F32))


def _inproj_seq_kernel(x_ref, g_ref, cos_ref, sin_ref, w_ref, o_ref, ou_ref, h_scr, *, tn, n_seq):
    seq = pl.program_id(1)
    tm = x_ref.shape[0]
    h_scr[...] = _rms(x_ref[...], g_ref[...]).astype(BF16)
    for j in range(IN_WIDTH // tn):
        cols = slice(j * tn, (j + 1) * tn)
        res = jnp.dot(h_scr[...], w_ref[:, cols], preferred_element_type=F32)
        if j * tn < S5_WIDTH:
            dst = pl.ds(seq, tm, stride=n_seq)
            for s in range(tn // SLAB_IN):
                ou_ref[j * (tn // SLAB_IN) + s, dst, :] = res[:, s * SLAB_IN:(s + 1) * SLAB_IN]
        if j in ROTARY_TILES:
            res = _rotary_heads(res, cos_ref[...], sin_ref[...])
        o_ref[:, cols] = res.astype(o_ref.dtype)


def _inproj_seq(x2d, g, w, n_seq, pos0, tm=256, tn=RET_WIDTH):
    t = x2d.shape[0]
    tiles = t // n_seq // tm
    row = lambda i, n: (n * tiles + i, 0)
    const = lambda i, n: (0, 0)
    cos2, sin2 = _rotary_tables(pos0, t // n_seq, 1)
    return pl.pallas_call(
        functools.partial(_inproj_seq_kernel, tn=tn, n_seq=n_seq),
        grid=(tiles, n_seq),
        in_specs=[
            pl.BlockSpec((tm, D_MODEL), row),
            pl.BlockSpec((1, D_MODEL), const),
            pl.BlockSpec((tm, HEAD_DIM), lambda i, n: (i, 0)),
            pl.BlockSpec((tm, HEAD_DIM), lambda i, n: (i, 0)),
            pl.BlockSpec((D_MODEL, IN_WIDTH), const, pipeline_mode=pl.Buffered(1)),
        ],
        out_specs=[pl.BlockSpec((tm, IN_WIDTH), row),
                   pl.BlockSpec((N_SLABS, n_seq * tm, SLAB_IN), lambda i, n: (0, i, 0))],
        out_shape=[jax.ShapeDtypeStruct((t, IN_WIDTH), BF16),
                   jax.ShapeDtypeStruct((N_SLABS, t, SLAB_IN), F32)],
        scratch_shapes=[pltpu.VMEM((tm, D_MODEL), BF16)],
        compiler_params=_params("arbitrary", "arbitrary"),
        name="in_proj_seq",
    )(x2d, g, cos2, sin2, w)


def _inproj_cast_kernel(x_ref, g_ref, cos_ref, sin_ref, w_ref, o_ref, wb_ref, ou_ref, h_scr, ru_s):
    j = pl.program_id(0)

    @pl.when(j == 0)
    def _():
        h_scr[...] = _rms(x_ref[...], g_ref[...]).astype(BF16)

    wb_ref[...] = w_ref[...].astype(BF16)
    res = jnp.dot(h_scr[...], wb_ref[...], preferred_element_type=F32)
    is_rotary = functools.reduce(jnp.logical_or, [j == r for r in ROTARY_TILES])

    @pl.when(is_rotary)
    def _():
        o_ref[...] = _rotary_heads(res, cos_ref[...], sin_ref[...]).astype(o_ref.dtype)

    @pl.when(jnp.logical_not(is_rotary))
    def _():
        o_ref[...] = res.astype(o_ref.dtype)

    @pl.when(j == 0)
    def _():
        seq_len, n_seq = ou_ref.shape[1], ou_ref.shape[2]
        for s in range(N_SLABS):
            ru_s[s] = res[:, s * SLAB_IN:(s + 1) * SLAB_IN]
        for s in range(N_SLABS):
            for t in range(seq_len):
                ou_ref[s, t] = ru_s[s, pl.ds(t, n_seq, stride=seq_len), :]


def _inproj_cast(x2d, g, w_f32, seq_len, pos0):
    t = x2d.shape[0]
    tn = RET_WIDTH
    cos2, sin2 = _rotary_tables(pos0, seq_len, t // seq_len)
    return pl.pallas_call(
        _inproj_cast_kernel,
        grid=(IN_WIDTH // tn,),
        in_specs=[
            pl.BlockSpec((t, D_MODEL), lambda j: (0, 0), pipeline_mode=pl.Buffered(1)),
            pl.BlockSpec((1, D_MODEL), lambda j: (0, 0)),
            pl.BlockSpec((t, HEAD_DIM), lambda j: (0, 0)),
            pl.BlockSpec((t, HEAD_DIM), lambda j: (0, 0)),
            pl.BlockSpec((D_MODEL, tn), lambda j: (0, j)),
        ],
        out_specs=[pl.BlockSpec((t, tn), lambda j: (0, j)),
                   pl.BlockSpec((D_MODEL, tn), lambda j: (0, j)),
                   pl.BlockSpec((N_SLABS, seq_len, t // seq_len, SLAB_IN), lambda j: (0, 0, 0, 0))],
        out_shape=[jax.ShapeDtypeStruct((t, IN_WIDTH), BF16),
                   jax.ShapeDtypeStruct((D_MODEL, IN_WIDTH), BF16),
                   jax.ShapeDtypeStruct((N_SLABS, seq_len, t // seq_len, SLAB_IN), F32)],
        scratch_shapes=[pltpu.VMEM((t, D_MODEL), BF16), pltpu.VMEM((N_SLABS, t, SLAB_IN), F32)],
        compiler_params=_params("arbitrary"),
        name="in_proj_cast",
    )(x2d, g, cos2, sin2, w_f32)


def _s5_param_kernel(lr_ref, li_ref, ls_ref, br_ref, bi_ref, ctr_ref, cti_ref,
                     pairr_ref, pairi_ref, wbr_ref, wbi_ref, cdr_ref, cdi_ref):
    lr = lr_ref[...]
    li = li_ref[...]
    dt = jnp.exp(ls_ref[...])
    mag = jnp.exp(lr * dt)
    ar = mag * jnp.cos(li * dt)
    ai = mag * jnp.sin(li * dt)
    den = lr * lr + li * li
    nr = ar - 1.0
    cr = (nr * lr + ai * li) / den
    ci = (ai * lr - nr * li) / den
    br = br_ref[...]
    bi = bi_ref[...]
    bbr = cr * br - ci * bi
    bbi = cr * bi + ci * br
    abr = ar * bbr - ai * bbi
    abi = ar * bbi + ai * bbr

    in_shape = (SLAB_IN, SLAB_STATE)
    same_in = (lax.broadcasted_iota(jnp.int32, in_shape, 0) // S5_GROUP
               == lax.broadcasted_iota(jnp.int32, in_shape, 1) // S5_STATE)
    out_shape = (SLAB_STATE, SLAB_IN)
    same_out = (lax.broadcasted_iota(jnp.int32, out_shape, 0) // S5_STATE
                == lax.broadcasted_iota(jnp.int32, out_shape, 1) // S5_GROUP)
    for s in range(N_SLABS):
        cs = slice(s * SLAB_STATE, (s + 1) * SLAB_STATE)
        for dst, top, bot in ((wbr_ref, bbr, abr), (wbi_ref, bbi, abi)):
            for k, part in enumerate((top, bot)):
                blk = jnp.tile(part[:, cs], (SLAB_GROUPS, 1))
                dst[s, k * SLAB_IN:(k + 1) * SLAB_IN, :] = jnp.where(same_in, blk, 0.0).astype(BF16)
        cdr_ref[s] = jnp.where(same_out, ctr_ref[cs, :], 0.0).astype(BF16)
        cdi_ref[s] = jnp.where(same_out, cti_ref[cs, :], 0.0).astype(BF16)

    rows = lax.broadcasted_iota(jnp.int32, (SUBLANES, N_STATE), 0)
    second = rows >= SUBLANES // 2
    full = lambda v: jnp.broadcast_to(v, (SUBLANES, N_STATE))
    pairr_ref[...] = jnp.where(second, full(ar * ar - ai * ai), full(ar))
    pairi_ref[...] = jnp.where(second, full(ar * ai + ai * ar), full(ai))


def _s5_params(lam_re, lam_im, log_step, b_re, b_im, c_re, c_im):
    lr = lam_re.reshape(1, N_STATE)
    li = lam_im.reshape(1, N_STATE)
    ls = jnp.repeat(log_step, S5_STATE).reshape(1, N_STATE)
    b_rows = lambda b: jnp.transpose(b, (2, 0, 1)).reshape(S5_GROUP, N_STATE)
    c_cols = lambda c: jnp.tile(jnp.transpose(c, (0, 2, 1)).reshape(N_STATE, S5_GROUP),
                                (1, SLAB_GROUPS))
    pair = jax.ShapeDtypeStruct((SUBLANES, N_STATE), F32)
    wb = jax.ShapeDtypeStruct((N_SLABS, 2 * SLAB_IN, SLAB_STATE), BF16)
    cd = jax.ShapeDtypeStruct((N_SLABS, SLAB_STATE, SLAB_IN), BF16)
    pairr, pairi, wbr, wbi, cdr, cdi = pl.pallas_call(
        _s5_param_kernel,
        out_shape=(pair, pair, wb, wb, cd, cd),
        name="s5_params",
    )(lr, li, ls, b_rows(b_re), b_rows(b_im), c_cols(c_re), c_cols(c_im))
    return (pairr, pairi), (wbr, wbi), (cdr, cdi)


def _cmul_add(xr, xi, pr, pi, vr, vi):
    return xr + (pr * vr - pi * vi), xi + (pr * vi + pi * vr)


def _s5_weight_specs(k_in):
    const3 = lambda *_: (0, 0, 0)
    const2 = lambda *_: (0, 0)
    return [
        pl.BlockSpec((SUBLANES, N_STATE), const2),
        pl.BlockSpec((SUBLANES, N_STATE), const2),
        pl.BlockSpec((N_SLABS, k_in, SLAB_STATE), const3),
        pl.BlockSpec((N_SLABS, k_in, SLAB_STATE), const3),
        pl.BlockSpec((N_SLABS, SLAB_STATE, SLAB_IN), const3),
        pl.BlockSpec((N_SLABS, SLAB_STATE, SLAB_IN), const3),
        pl.BlockSpec((1, S5_WIDTH), const2),
        pl.BlockSpec((S5_WIDTH, S5_WIDTH), const2),
        pl.BlockSpec((1, S5_WIDTH), const2),
    ]


def _s5_step_kernel(u_ref, x0r_ref, x0i_ref, pairr_ref, pairi_ref, wbr_ref, wbi_ref, cdr_ref, cdi_ref,
                    d_ref, wg_ref, bg_ref, o_ref, xlr_ref, xli_ref, wgb_ref, xr_s, xi_s, y_s, o_s):
    seq_len, n_seq = u_ref.shape[1], u_ref.shape[2]

    @pl.when(pl.program_id(0) == 0)
    def _():
        wgb_ref[...] = wg_ref[...].astype(BF16)

    for s in range(N_SLABS):
        cs = slice(s * SLAB_STATE, (s + 1) * SLAB_STATE)
        us = slice(s * SLAB_IN, (s + 1) * SLAB_IN)
        u = u_ref[s].reshape(seq_len * n_seq, SLAB_IN)
        ub = u.astype(BF16)
        xr_s[:, cs] = jnp.dot(ub, wbr_ref[s], preferred_element_type=F32)
        xi_s[:, cs] = jnp.dot(ub, wbi_ref[s], preferred_element_type=F32)

        a_r = jnp.broadcast_to(pairr_ref[0:1, cs], (SUBLANES, SLAB_STATE))
        a_i = jnp.broadcast_to(pairi_ref[0:1, cs], (SUBLANES, SLAB_STATE))
        for b in range(n_seq // SUBLANES):
            seqs = slice(b * SUBLANES, (b + 1) * SUBLANES)
            xr, xi = x0r_ref[seqs, cs], x0i_ref[seqs, cs]
            for t in range(seq_len):
                rows = slice(t * n_seq + b * SUBLANES, t * n_seq + (b + 1) * SUBLANES)
                xr, xi = _cmul_add(xr_s[rows, cs], xi_s[rows, cs], a_r, a_i, xr, xi)
                xr_s[rows, cs] = xr
                xi_s[rows, cs] = xi
            xlr_ref[seqs, cs] = xr
            xli_ref[seqs, cs] = xi

        y = (jnp.dot(xr_s[:, cs].astype(BF16), cdr_ref[s], preferred_element_type=F32)
             - jnp.dot(xi_s[:, cs].astype(BF16), cdi_ref[s], preferred_element_type=F32))
        y_s[:, us] = jax.nn.gelu(y + d_ref[:, us] * u)

    y = y_s[...]
    z = jnp.dot(y.astype(BF16), wgb_ref[...], preferred_element_type=F32) + bg_ref[...]
    o = y * jax.nn.sigmoid(z)
    for s in range(N_SLABS):
        for t in range(seq_len):
            o_s[s, pl.ds(t, n_seq, stride=seq_len), :] = (
                o[t * n_seq:(t + 1) * n_seq, s * SLAB_IN:(s + 1) * SLAB_IN])
    for s in range(N_SLABS):
        o_ref[:, s * SLAB_IN:(s + 1) * SLAB_IN] = o_s[s].astype(o_ref.dtype)


def _s5_step(u_t, x0, pairs, wb, cd, d, w_glu, b_glu, n_tile=32):
    _, seq_len, n_seq, _ = u_t.shape
    rows = n_tile * seq_len
    row = lambda i: (i, 0)
    st = jax.ShapeDtypeStruct((n_seq, N_STATE), F32)
    return pl.pallas_call(
        _s5_step_kernel,
        grid=(n_seq // n_tile,),
        in_specs=[pl.BlockSpec((N_SLABS, seq_len, n_tile, SLAB_IN), lambda i: (0, 0, i, 0)),
                  pl.BlockSpec((n_tile, N_STATE), row), pl.BlockSpec((n_tile, N_STATE), row)]
        + _s5_weight_specs(SLAB_IN),
        out_specs=(pl.BlockSpec((rows, S5_WIDTH), row), pl.BlockSpec((n_tile, N_STATE), row),
                   pl.BlockSpec((n_tile, N_STATE), row),
                   pl.BlockSpec((S5_WIDTH, S5_WIDTH), lambda i: (0, 0))),
        out_shape=(jax.ShapeDtypeStruct((n_seq * seq_len, S5_WIDTH), BF16), st, st,
                   jax.ShapeDtypeStruct((S5_WIDTH, S5_WIDTH), BF16)),
        scratch_shapes=[pltpu.VMEM((rows, N_STATE), F32), pltpu.VMEM((rows, N_STATE), F32),
                        pltpu.VMEM((rows, S5_WIDTH), F32),
                        pltpu.VMEM((N_SLABS, rows, SLAB_IN), F32)],
        compiler_params=_params("arbitrary"),
        name="s5_step",
    )(u_t, x0[0], x0[1], pairs[0], pairs[1], wb[0], wb[1], cd[0], cd[1], d, w_glu, b_glu)


def _s5_seq_kernel(u_ref, pairr_ref, pairi_ref, wbr_ref, wbi_ref, cdr_ref, cdi_ref, d_ref, wg_ref,
                   bg_ref, o_ref, xlr_ref, xli_ref, xr_s, xi_s, y_s, o_s, cbr_s, cbi_s, ul_s):
    n_rows = u_ref.shape[1]
    half = SUBLANES // 2
    tt = n_rows // half

    @pl.when(pl.program_id(0) == 0)
    def _():
        cbr_s[...] = jnp.zeros_like(cbr_s)
        cbi_s[...] = jnp.zeros_like(cbi_s)
        ul_s[...] = jnp.zeros_like(ul_s)

    first = lax.broadcasted_iota(jnp.int32, (SUBLANES, SLAB_IN), 0) < half

    def b_projection(s):
        cs = slice(s * SLAB_STATE, (s + 1) * SLAB_STATE)
        u = u_ref[s]
        shifted = pltpu.roll(u, half, 0)
        head = jnp.where(first, pltpu.roll(ul_s[s], half, 0), shifted[:SUBLANES])
        u_prev = jnp.concatenate([head, shifted[SUBLANES:]], axis=0)
        ul_s[s] = u[n_rows - SUBLANES:]
        ub = jnp.concatenate([u, u_prev], axis=1).astype(BF16)
        xr_s[:, cs] = jnp.dot(ub, wbr_ref[s], preferred_element_type=F32)
        xi_s[:, cs] = jnp.dot(ub, wbi_ref[s], preferred_element_type=F32)

    def scan(s):
        cs = slice(s * SLAB_STATE, (s + 1) * SLAB_STATE)
        pr = jnp.broadcast_to(pairr_ref[half:half + 1, cs], (SUBLANES, SLAB_STATE))
        pi = jnp.broadcast_to(pairi_ref[half:half + 1, cs], (SUBLANES, SLAB_STATE))
        cr, ci = cbr_s[:, cs], cbi_s[:, cs]
        for b in range(n_rows // SUBLANES):
            rows = slice(b * SUBLANES, (b + 1) * SUBLANES)
            cr, ci = _cmul_add(xr_s[rows, cs], xi_s[rows, cs], pr, pi, cr, ci)
            xr_s[rows, cs] = cr
            xi_s[rows, cs] = ci
        cbr_s[:, cs] = cr
        cbi_s[:, cs] = ci

    def c_projection(s):
        cs = slice(s * SLAB_STATE, (s + 1) * SLAB_STATE)
        us = slice(s * SLAB_IN, (s + 1) * SLAB_IN)
        y = (jnp.dot(xr_s[:, cs].astype(BF16), cdr_ref[s], preferred_element_type=F32)
             - jnp.dot(xi_s[:, cs].astype(BF16), cdi_ref[s], preferred_element_type=F32))
        y_s[:, us] = jax.nn.gelu(y + d_ref[:, us] * u_ref[s])

    stages = (b_projection, scan, c_projection)
    for step in range(N_SLABS + len(stages) - 1):
        for lag, stage in enumerate(stages):
            if 0 <= step - lag < N_SLABS:
                stage(step - lag)

    xlr_ref[...] = cbr_s[...]
    xli_ref[...] = cbi_s[...]

    y = y_s[...]
    z = jnp.dot(y.astype(BF16), wg_ref[...], preferred_element_type=F32) + bg_ref[...]
    o = y * jax.nn.sigmoid(z)
    for s in range(N_SLABS):
        o_s[s] = o[:, s * SLAB_IN:(s + 1) * SLAB_IN]
    for n in range(half):
        for s in range(N_SLABS):
            o_ref[n, :, s * SLAB_IN:(s + 1) * SLAB_IN] = (
                o_s[s, pl.ds(n, tt, stride=half), :].astype(o_ref.dtype))


def _s5_seq(u_t, pairs, wb, cd, d, w_glu, b_glu, n_seq, seq_len, tt=64):
    assert 2 * n_seq == SUBLANES
    n_rows = n_seq * tt
    st = jax.ShapeDtypeStruct((SUBLANES, N_STATE), F32)
    y, xlr, xli = pl.pallas_call(
        _s5_seq_kernel,
        grid=(seq_len // tt,),
        in_specs=[pl.BlockSpec((N_SLABS, n_rows, SLAB_IN), lambda i: (0, i, 0))]
        + _s5_weight_specs(2 * SLAB_IN),
        out_specs=(pl.BlockSpec((n_seq, tt, S5_WIDTH), lambda i: (0, i, 0)),
                   pl.BlockSpec((SUBLANES, N_STATE), lambda i: (0, 0)),
                   pl.BlockSpec((SUBLANES, N_STATE), lambda i: (0, 0))),
        out_shape=(jax.ShapeDtypeStruct((n_seq, seq_len, S5_WIDTH), BF16), st, st),
        scratch_shapes=[pltpu.VMEM((n_rows, N_STATE), F32), pltpu.VMEM((n_rows, N_STATE), F32),
                        pltpu.VMEM((n_rows, S5_WIDTH), F32),
                        pltpu.VMEM((N_SLABS, n_rows, SLAB_IN), F32),
                        pltpu.VMEM((SUBLANES, N_STATE), F32), pltpu.VMEM((SUBLANES, N_STATE), F32),
                        pltpu.VMEM((N_SLABS, SUBLANES, SLAB_IN), F32)],
        compiler_params=_params("arbitrary"),
        name="s5_seq",
    )(u_t, pairs[0], pairs[1], wb[0], wb[1], cd[0], cd[1], d, w_glu, b_glu)
    return y.reshape(n_seq * seq_len, S5_WIDTH), xlr[n_seq:], xli[n_seq:]


_NT = (((1,), (1,)), ((), ()))
_TN = (((0,), (0,)), ((), ()))


def _norm_gate(o, g, gn_w):
    mu = jnp.mean(o, axis=-1, keepdims=True)
    oc = o - mu
    var = jnp.mean(oc * oc, axis=-1, keepdims=True)
    return jax.nn.silu(g) * (oc * lax.rsqrt(var + NORM_EPS) * gn_w)


def _ret_seq_kernel(q_ref, k_ref, v_ref, g_ref, mask_ref, qd_ref, kd_ref, cd_ref,
                    gnw_ref, o_ref, rn_ref, r_s, sc_s, kv_s, *, chunks):
    @pl.when(pl.program_id(1) == 0)
    def _():
        r_s[...] = jnp.zeros_like(r_s)

    for c in range(chunks):
        rows = slice(c * RET_TILE, (c + 1) * RET_TILE)
        for h in range(RET_HEADS):
            hs = slice(h * HEAD_DIM, (h + 1) * HEAD_DIM)
            qb, kb, vb = q_ref[rows, hs], k_ref[rows, hs], v_ref[rows, hs]
            sc = lax.dot_general(qb, kb, _NT, preferred_element_type=F32) * mask_ref[h]
            sc_s[c, h] = sc.astype(BF16)
            k_dec = (kb.astype(F32) * kd_ref[h]).astype(BF16)
            kv_s[c, h] = lax.dot_general(k_dec, vb, _TN, preferred_element_type=F32)

    for c in range(chunks):
        rows = slice(c * RET_TILE, (c + 1) * RET_TILE)
        for h in range(RET_HEADS):
            hs = slice(h * HEAD_DIM, (h + 1) * HEAD_DIM)
            qb, vb = q_ref[rows, hs], v_ref[rows, hs]
            r_old = r_s[h]
            q_dec = (qb.astype(F32) * qd_ref[h]).astype(BF16)
            lhs = jnp.concatenate([sc_s[c, h], q_dec], axis=1)
            rhs = jnp.concatenate([vb, r_old.astype(BF16)], axis=0)
            o = jnp.dot(lhs, rhs, preferred_element_type=F32)
            r_s[h] = r_old * cd_ref[h] + kv_s[c, h]
            o_ref[rows, hs] = _norm_gate(o, g_ref[rows, hs].astype(F32),
                                         gnw_ref[:, hs]).astype(o_ref.dtype)
    rn_ref[0] = r_s[...]


def _ret_step_kernel(q_ref, k_ref, v_ref, g_ref, mask_ref, qd_ref, kd_ref, cd_ref,
                     gnw_ref, r0_ref, o_ref, rn_ref, qr_s, kdt_s, ob_s, *, n_seq):
    rows_per_seq = RET_TILE // n_seq
    for h in range(RET_HEADS):
        hs = slice(h * HEAD_DIM, (h + 1) * HEAD_DIM)
        qb, kb = q_ref[:, hs], k_ref[:, hs]
        sc = lax.dot_general(qb, kb, _NT, preferred_element_type=F32) * mask_ref[h]
        ob_s[:, hs] = jnp.dot(sc.astype(BF16), v_ref[:, hs], preferred_element_type=F32)
        qr_s[:, hs] = qb.astype(F32)
        kdt_s[h] = (kb.astype(F32) * kd_ref[h]).T

    lane = lax.broadcasted_iota(jnp.int32, (HEAD_DIM, RET_TILE), 1)

    def per_seq(s, c):
        r0 = pl.multiple_of(s * rows_per_seq, rows_per_seq)
        rows = pl.ds(r0, rows_per_seq)
        in_seq = (lane >= r0) & (lane < r0 + rows_per_seq)
        for h in range(RET_HEADS):
            hs = slice(h * HEAD_DIM, (h + 1) * HEAD_DIM)
            r_old = r0_ref[s, h]
            cross = jnp.dot(qr_s[rows, hs].astype(BF16), r_old.astype(BF16), preferred_element_type=F32)
            ob_s[rows, hs] = ob_s[rows, hs] + cross * qd_ref[h, rows, :]
            kdt = jnp.where(in_seq, kdt_s[h], 0.0).astype(BF16)
            rn_ref[s, h] = r_old * cd_ref[h] + jnp.dot(kdt, v_ref[:, hs], preferred_element_type=F32)
        return c

    lax.fori_loop(0, n_seq, per_seq, 0)

    for h in range(RET_HEADS):
        hs = slice(h * HEAD_DIM, (h + 1) * HEAD_DIM)
        o_ref[:, hs] = _norm_gate(ob_s[:, hs], g_ref[:, hs].astype(F32),
                                  gnw_ref[:, hs]).astype(o_ref.dtype)


def _decay_tables(chunk, n_seq):
    scale = HEAD_DIM ** -0.5
    log_gamma = np.log(1.0 - 2.0 ** (-5.0 - np.arange(RET_HEADS, dtype=np.float64)))
    idx = np.arange(chunk, dtype=np.float64)
    diff = idx[:, None] - idx[None, :]
    mask = np.where(diff >= 0, np.exp(log_gamma[:, None, None] * np.maximum(diff, 0.0)), 0.0)
    q_decay = np.exp(log_gamma[:, None] * (idx + 1.0))
    k_decay = np.exp(log_gamma[:, None] * (chunk - 1.0 - idx)) * scale
    chunk_decay = np.exp(log_gamma * chunk)
    mask_t = scale * np.einsum("hab,st->hsatb", mask, np.eye(n_seq)).reshape(
        RET_HEADS, RET_TILE, RET_TILE)
    qd_t = np.broadcast_to(np.tile(q_decay, (1, n_seq))[:, :, None], (RET_HEADS, RET_TILE, HEAD_DIM))
    kd_t = np.broadcast_to(np.tile(k_decay, (1, n_seq))[:, :, None], (RET_HEADS, RET_TILE, HEAD_DIM))
    const = lambda a: jnp.asarray(np.ascontiguousarray(a), dtype=F32)
    return const(mask_t), const(qd_t), const(kd_t), const(chunk_decay)


def _retention(proj, r0, gn_w, n_seq, seq_len, chunks=4):
    t = proj.shape[0]
    const3 = lambda *_: (0, 0, 0)
    if r0 is None:
        tile, tile_seqs = chunks * RET_TILE, 1
        steps = seq_len // tile
        grid = (n_seq, steps)
        row = lambda n, c: n * steps + c
        state_map = lambda n, c: (n, 0, 0, 0)
        state_block = (1, RET_HEADS, HEAD_DIM, HEAD_DIM)
        sem = ("arbitrary", "arbitrary")
        body = functools.partial(_ret_seq_kernel, chunks=chunks)
        scratch = [pltpu.VMEM((RET_HEADS, HEAD_DIM, HEAD_DIM), F32),
                   pltpu.VMEM((chunks, RET_HEADS, RET_TILE, RET_TILE), BF16),
                   pltpu.VMEM((chunks, RET_HEADS, HEAD_DIM, HEAD_DIM), F32)]
        name = "ret_seq"
    else:
        tile, tile_seqs = RET_TILE, RET_TILE // seq_len
        grid = (t // tile,)
        row = lambda i: i
        state_map = lambda i: (i, 0, 0, 0)
        state_block = (tile_seqs, RET_HEADS, HEAD_DIM, HEAD_DIM)
        sem = ("arbitrary",)
        body = functools.partial(_ret_step_kernel, n_seq=tile_seqs)
        scratch = [pltpu.VMEM((RET_TILE, RET_WIDTH), F32),
                   pltpu.VMEM((RET_HEADS, HEAD_DIM, RET_TILE), F32),
                   pltpu.VMEM((RET_TILE, RET_WIDTH), F32)]
        name = "ret_step"
    mask_t, qd_t, kd_t, cd = _decay_tables(RET_TILE // tile_seqs, tile_seqs)

    def col(cb):
        return pl.BlockSpec((tile, RET_WIDTH), lambda *a: (row(*a), cb))

    in_specs = [
        col(1), col(2), col(3), col(4),
        pl.BlockSpec((RET_HEADS, RET_TILE, RET_TILE), const3),
        pl.BlockSpec((RET_HEADS, RET_TILE, HEAD_DIM), const3),
        pl.BlockSpec((RET_HEADS, RET_TILE, HEAD_DIM), const3),
        pl.BlockSpec(memory_space=pltpu.SMEM),
        pl.BlockSpec((1, RET_WIDTH), lambda *_: (0, 0)),
    ]
    args = [proj, proj, proj, proj, mask_t, qd_t, kd_t, cd, gn_w]
    if r0 is not None:
        in_specs.append(pl.BlockSpec(state_block, state_map))
        args.append(r0)
    return pl.pallas_call(
        body,
        grid=grid,
        in_specs=in_specs,
        out_specs=(pl.BlockSpec((tile, RET_WIDTH), lambda *a: (row(*a), 0)),
                   pl.BlockSpec(state_block, state_map)),
        out_shape=(jax.ShapeDtypeStruct((t, RET_WIDTH), BF16),
                   jax.ShapeDtypeStruct((n_seq, RET_HEADS, HEAD_DIM, HEAD_DIM), F32)),
        scratch_shapes=scratch,
        compiler_params=_params(*sem),
        name=name,
    )(*args)


def _outproj_kernel(x_ref, a_ref, b_ref, wa_ref, wb_ref, o_ref, *w_bf16, cast):
    if cast:
        (wo_ref,) = w_bf16

        @pl.when(pl.program_id(0) == 0)
        def _():
            wo_ref[:S5_WIDTH, :] = wa_ref[...].astype(BF16)
            wo_ref[S5_WIDTH:, :] = wb_ref[...].astype(BF16)

        wa, wb = wo_ref[:S5_WIDTH, :], wo_ref[S5_WIDTH:, :]
    else:
        wa, wb = wa_ref[...], wb_ref[...]
    o_ref[...] = (x_ref[...]
                  + jnp.dot(a_ref[...], wa, preferred_element_type=F32)
                  + jnp.dot(b_ref[...], wb, preferred_element_type=F32))


def _outproj(x2d, a, b, w, tm=512):
    t = x2d.shape[0]
    cast = w.dtype == F32
    resident = pl.Buffered(1)
    out_specs = [pl.BlockSpec((tm, D_MODEL), lambda i: (i, 0))]
    out_shape = [jax.ShapeDtypeStruct((t, D_MODEL), F32)]
    if cast:
        out_specs.append(pl.BlockSpec((S5_WIDTH + RET_WIDTH, D_MODEL), lambda i: (0, 0)))
        out_shape.append(jax.ShapeDtypeStruct((S5_WIDTH + RET_WIDTH, D_MODEL), BF16))
    return pl.pallas_call(
        functools.partial(_outproj_kernel, cast=cast),
        grid=(t // tm,),
        in_specs=[
            pl.BlockSpec((tm, D_MODEL), lambda i: (i, 0)),
            pl.BlockSpec((tm, S5_WIDTH), lambda i: (i, 0)),
            pl.BlockSpec((tm, RET_WIDTH), lambda i: (i, 0)),
            pl.BlockSpec((S5_WIDTH, D_MODEL), lambda i: (0, 0), pipeline_mode=resident),
            pl.BlockSpec((RET_WIDTH, D_MODEL), lambda i: (1, 0), pipeline_mode=resident),
        ],
        out_specs=out_specs,
        out_shape=out_shape,
        compiler_params=_params("arbitrary"),
        name="out_proj_cast" if cast else "out_proj",
    )(x2d, a, b, w, w)


def _ffn_kernel(x_ref, gn_ref, wg_ref, wu_ref, wo_ref, gf_ref, o_ref, *rest, cast):
    if cast:
        wgb_ref, wub_ref, wob_ref, h_scr = rest
        wgb_ref[...] = wg_ref[...].astype(BF16)
        wub_ref[...] = wu_ref[...].astype(BF16)
        wob_ref[...] = wo_ref[...].astype(BF16)
        wg_ref, wu_ref, wo_ref = wgb_ref, wub_ref, wob_ref
    else:
        (h_scr,) = rest
    j = pl.program_id(1)
    last_j = pl.num_programs(1) - 1

    def step(first, last):
        for r in range(x_ref.shape[0] // FFN_ROWS):
            rows = slice(r * FFN_ROWS, (r + 1) * FFN_ROWS)
            if first:
                h_scr[rows, :] = _rms(x_ref[rows, :], gn_ref[...]).astype(BF16)
            h = h_scr[rows, :]
            gate = jnp.dot(h, wg_ref[...], preferred_element_type=F32)
            up = jnp.dot(h, wu_ref[...], preferred_element_type=F32)
            act = (jax.nn.silu(gate) * up).astype(BF16)
            acc = (x_ref if first else o_ref)[rows, :] + jnp.dot(act, wo_ref[...],
                                                                  preferred_element_type=F32)
            o_ref[rows, :] = _rms(acc, gf_ref[...]) if last else acc

    pl.when(j == 0)(lambda: step(True, False))
    pl.when((j > 0) & (j < last_j))(lambda: step(False, False))
    pl.when(j == last_j)(lambda: step(False, True))


def _ffn(x2d, g_ffn, w_gate, w_up, w_down, g_final, tm=1024, tf=512):
    t = x2d.shape[0]
    nf = D_FF // tf
    cast = w_down.dtype == F32
    (wg, g0), (wu, u0) = w_gate, w_up
    g0, u0 = g0 * nf, u0 * nf
    x_mode = {}
    out_specs = [pl.BlockSpec((tm, D_MODEL), lambda i, j: (i, 0))]
    out_shape = [jax.ShapeDtypeStruct((t, D_MODEL), F32)]
    if cast:
        assert t == tm, "the bf16 weight outputs are written once per row tile"
        x_mode = dict(pipeline_mode=pl.Buffered(1))
        out_specs += [pl.BlockSpec((D_MODEL, tf), lambda i, j: (0, j)),
                      pl.BlockSpec((D_MODEL, tf), lambda i, j: (0, j)),
                      pl.BlockSpec((tf, D_MODEL), lambda i, j: (j, 0))]
        out_shape += [jax.ShapeDtypeStruct((D_MODEL, D_FF), BF16),
                      jax.ShapeDtypeStruct((D_MODEL, D_FF), BF16),
                      jax.ShapeDtypeStruct((D_FF, D_MODEL), BF16)]
    return pl.pallas_call(
        functools.partial(_ffn_kernel, cast=cast),
        grid=(t // tm, nf),
        in_specs=[
            pl.BlockSpec((tm, D_MODEL), lambda i, j: (i, 0), **x_mode),
            pl.BlockSpec((1, D_MODEL), lambda i, j: (0, 0)),
            pl.BlockSpec((D_MODEL, tf), lambda i, j: (0, j + g0)),
            pl.BlockSpec((D_MODEL, tf), lambda i, j: (0, j + u0)),
            pl.BlockSpec((tf, D_MODEL), lambda i, j: (j, 0)),
            pl.BlockSpec((1, D_MODEL), lambda i, j: (0, 0)),
        ],
        out_specs=out_specs,
        out_shape=out_shape,
        scratch_shapes=[pltpu.VMEM((tm, D_MODEL), BF16)],
        compiler_params=_params("arbitrary", "arbitrary"),
        name="ffn_cast" if cast else "ffn",
    )(x2d, g_ffn, wg, wu, w_down, g_final)


def _finish(x2d, proj, s5_out, ret_state, w, w_out, ffn_w, ffn_tf, n, l):
    ret_out, ret_new = _retention(proj, ret_state, w["gn_w"], n, l)
    x1, *w_out_b = _outproj(x2d, s5_out, ret_out, w_out)
    y, *ffn_b = _ffn(x1, w["norm_ffn"], *ffn_w, w["norm_final"], tf=ffn_tf)
    return y.reshape(n, l, D_MODEL), ret_new, w_out_b, ffn_b


def _as_groups(re, im, n):
    return re.reshape(n, S5_GROUPS, S5_STATE), im.reshape(n, S5_GROUPS, S5_STATE)


def _sample_layer(x, s5_state, ret_state, w, big):
    n, l, _ = x.shape
    x2d = x.reshape(n * l, D_MODEL)
    proj, w_in_b, u_t = _inproj_cast(x2d, w["norm_mix"], big["w_in"], l, float(PAST_LEN))
    s5_out, s5_re, s5_im, w_glu_b = _s5_step(u_t, s5_state, w["pairs"], w["wb"], w["cd"], w["d"],
                                             big["w_glu"], w["b_glu"])
    ffn_f32 = ((big["w_ffn_in"], 0), (big["w_ffn_in"], 1), big["w_ffn_out"])
    y, ret_new, (w_out_b,), (wg_b, wu_b, wd_b) = _finish(
        x2d, proj, s5_out, ret_state, w, big["w_out"], ffn_f32, 256, n, l)
    big_b = dict(w_in=w_in_b, w_glu=w_glu_b, w_out=w_out_b, ffn=((wg_b, 0), (wu_b, 0), wd_b))
    return (y, *_as_groups(s5_re, s5_im, n), ret_new), big_b


def _prompt_layer(x, w, big_b):
    n, l, _ = x.shape
    x2d = x.reshape(n * l, D_MODEL)
    proj, u_t = _inproj_seq(x2d, w["norm_mix"], big_b["w_in"], n, 0.0)
    s5_out, s5_re, s5_im = _s5_seq(u_t, w["pairs"], w["wb"], w["cd"], w["d"], big_b["w_glu"],
                                   w["b_glu"], n, l)
    y, ret_new, _, _ = _finish(x2d, proj, s5_out, None, w, big_b["w_out"], big_b["ffn"],
                               512, n, l)
    return (y, *_as_groups(s5_re, s5_im, n), ret_new)


def kernel(x_prompt, x_sample, state_s5_re, state_s5_im, state_ret, norm_mix, w_in, s5_lambda_re, s5_lambda_im, s5_log_step, s5_b_re, s5_b_im, s5_c_re, s5_c_im, s5_d, s5_w_glu, s5_b_glu, ret_gn_w, w_out, norm_ffn, w_ffn_in, w_ffn_out, norm_final):
    assert norm_mix.shape[0] == 1, "single-layer stack"
    pairs, wb, cd = _s5_params(s5_lambda_re[0], s5_lambda_im[0], s5_log_step[0],
                               s5_b_re[0], s5_b_im[0], s5_c_re[0], s5_c_im[0])
    w = dict(norm_mix=norm_mix, pairs=pairs, wb=wb, cd=cd, d=s5_d, b_glu=s5_b_glu,
             gn_w=ret_gn_w, norm_ffn=norm_ffn, norm_final=norm_final.reshape(1, D_MODEL))
    big = dict(w_in=w_in[0], w_glu=s5_w_glu[0], w_out=w_out[0],
               w_ffn_in=w_ffn_in[0], w_ffn_out=w_ffn_out[0])
    n_s = x_sample.shape[0]
    (ys, s_re, s_im, s_ret), big_b = _sample_layer(
        x_sample,
        (state_s5_re[0].reshape(n_s, N_STATE), state_s5_im[0].reshape(n_s, N_STATE)),
        state_ret[0], w, big)
    yp, p_re, p_im, p_ret = _prompt_layer(x_prompt, w, big_b)
    return (yp, ys, p_re[None], p_im[None], p_ret[None], s_re[None], s_im[None], s_ret[None])
```

```python
import functools
import math

import jax
import jax.numpy as jnp
import numpy as np
from jax import lax
from jax.experimental import pallas as pl
from jax.experimental.pallas import tpu as pltpu

F32 = jnp.float32
BF16 = jnp.bfloat16

D_MODEL = 2048
S5_WIDTH = 1024
S5_GROUP = 16
S5_GROUPS = 64
S5_STATE = 64
N_STATE = S5_GROUPS * S5_STATE
RET_WIDTH = 1024
RET_HEADS = 8
HEAD_DIM = 128
ROPE_BASE = 10000.0
D_FF = 5632
IN_WIDTH = S5_WIDTH + 4 * RET_WIDTH
NORM_EPS = 1e-6
PAST_LEN = 16384

SUBLANES = 8
SLAB_GROUPS = 8
N_SLABS = S5_GROUPS // SLAB_GROUPS
SLAB_IN = SLAB_GROUPS * S5_GROUP
SLAB_STATE = SLAB_GROUPS * S5_STATE
RET_TILE = 128
ROTARY_TILES = (1, 2)
FFN_ROWS = 512

VMEM_LIMIT_BYTES = 60 * 1024 * 1024


def _params(*sem):
    return pltpu.CompilerParams(dimension_semantics=sem, vmem_limit_bytes=VMEM_LIMIT_BYTES)


def _rms(xf, g):
    ms = jnp.mean(xf * xf, axis=-1, keepdims=True)
    return xf * lax.rsqrt(ms + NORM_EPS) * g


def _rotary(x, cos2, sin2):
    return x * cos2 + pltpu.roll(x, HEAD_DIM // 2, 1) * sin2


def _rotary_heads(x, cos2, sin2):
    return jnp.concatenate(
        [_rotary(x[:, h * HEAD_DIM:(h + 1) * HEAD_DIM], cos2, sin2) for h in range(RET_HEADS)],
        axis=1)


def _rotary_tables(pos0, rows, reps):
    half = HEAD_DIM // 2
    inv_freq = ROPE_BASE ** (-np.arange(half, dtype=np.float64) / half)
    pos = pos0 + np.arange(rows, dtype=np.float64)
    ang = pos[:, None] * inv_freq[None, :]
    cos, sin = np.cos(ang), np.sin(ang)
    cos2 = np.concatenate([cos, cos], axis=-1)
    sin2 = np.concatenate([-sin, sin], axis=-1)
    return (jnp.asarray(np.tile(cos2, (reps, 1)), dtype=F32),
            jnp.asarray(np.tile(sin2, (reps, 1)), dtype=F32))


def _inproj_seq_kernel(x_ref, g_ref, cos_ref, sin_ref, w_ref, o_ref, ou_ref, h_scr, *, tn, n_seq):
    seq = pl.program_id(1)
    tm = x_ref.shape[0]
    h_scr[...] = _rms(x_ref[...], g_ref[...]).astype(BF16)
    for j in range(IN_WIDTH // tn):
        cols = slice(j * tn, (j + 1) * tn)
        res = jnp.dot(h_scr[...], w_ref[:, cols], preferred_element_type=F32)
        if j * tn < S5_WIDTH:
            dst = pl.ds(seq, tm, stride=n_seq)
            for s in range(tn // SLAB_IN):
                ou_ref[j * (tn // SLAB_IN) + s, dst, :] = res[:, s * SLAB_IN:(s + 1) * SLAB_IN]
        if j in ROTARY_TILES:
            res = _rotary_heads(res, cos_ref[...], sin_ref[...])
        o_ref[:, cols] = res.astype(o_ref.dtype)


def _inproj_seq(x2d, g, w, n_seq, pos0, tm=256, tn=RET_WIDTH):
    t = x2d.shape[0]
    tiles = t // n_seq // tm
    row = lambda i, n: (n * tiles + i, 0)
    const = lambda i, n: (0, 0)
    cos2, sin2 = _rotary_tables(pos0, t // n_seq, 1)
    return pl.pallas_call(
        functools.partial(_inproj_seq_kernel, tn=tn, n_seq=n_seq),
        grid=(tiles, n_seq),
        in_specs=[
            pl.BlockSpec((tm, D_MODEL), row),
            pl.BlockSpec((1, D_MODEL), const),
            pl.BlockSpec((tm, HEAD_DIM), lambda i, n: (i, 0)),
            pl.BlockSpec((tm, HEAD_DIM), lambda i, n: (i, 0)),
            pl.BlockSpec((D_MODEL, IN_WIDTH), const, pipeline_mode=pl.Buffered(1)),
        ],
        out_specs=[pl.BlockSpec((tm, IN_WIDTH), row),
                   pl.BlockSpec((N_SLABS, n_seq * tm, SLAB_IN), lambda i, n: (0, i, 0))],
        out_shape=[jax.ShapeDtypeStruct((t, IN_WIDTH), BF16),
                   jax.ShapeDtypeStruct((N_SLABS, t, SLAB_IN), F32)],
        scratch_shapes=[pltpu.VMEM((tm, D_MODEL), BF16)],
        compiler_params=_params("arbitrary", "arbitrary"),
        name="in_proj_seq",
    )(x2d, g, cos2, sin2, w)


def _inproj_cast_kernel(x_ref, g_ref, cos_ref, sin_ref, w_ref, o_ref, wb_ref, ou_ref, h_scr, ru_s):
    j = pl.program_id(0)

    @pl.when(j == 0)
    def _():
        h_scr[...] = _rms(x_ref[...], g_ref[...]).astype(BF16)

    wb_ref[...] = w_ref[...].astype(BF16)
    res = jnp.dot(h_scr[...], wb_ref[...], preferred_element_type=F32)
    is_rotary = functools.reduce(jnp.logical_or, [j == r for r in ROTARY_TILES])

    @pl.when(is_rotary)
    def _():
        o_ref[...] = _rotary_heads(res, cos_ref[...], sin_ref[...]).astype(o_ref.dtype)

    @pl.when(jnp.logical_not(is_rotary))
    def _():
        o_ref[...] = res.astype(o_ref.dtype)

    @pl.when(j == 0)
    def _():
        seq_len, n_seq = ou_ref.shape[1], ou_ref.shape[2]
        for s in range(N_SLABS):
            ru_s[s] = res[:, s * SLAB_IN:(s + 1) * SLAB_IN]
        for s in range(N_SLABS):
            for t in range(seq_len):
                ou_ref[s, t] = ru_s[s, pl.ds(t, n_seq, stride=seq_len), :]


def _inproj_cast(x2d, g, w_f32, seq_len, pos0):
    t = x2d.shape[0]
    tn = RET_WIDTH
    cos2, sin2 = _rotary_tables(pos0, seq_len, t // seq_len)
    return pl.pallas_call(
        _inproj_cast_kernel,
        grid=(IN_WIDTH // tn,),
        in_specs=[
            pl.BlockSpec((t, D_MODEL), lambda j: (0, 0), pipeline_mode=pl.Buffered(1)),
            pl.BlockSpec((1, D_MODEL), lambda j: (0, 0)),
            pl.BlockSpec((t, HEAD_DIM), lambda j: (0, 0)),
            pl.BlockSpec((t, HEAD_DIM), lambda j: (0, 0)),
            pl.BlockSpec((D_MODEL, tn), lambda j: (0, j)),
        ],
        out_specs=[pl.BlockSpec((t, tn), lambda j: (0, j)),
                   pl.BlockSpec((D_MODEL, tn), lambda j: (0, j)),
                   pl.BlockSpec((N_SLABS, seq_len, t // seq_len, SLAB_IN), lambda j: (0, 0, 0, 0))],
        out_shape=[jax.ShapeDtypeStruct((t, IN_WIDTH), BF16),
                   jax.ShapeDtypeStruct((D_MODEL, IN_WIDTH), BF16),
                   jax.ShapeDtypeStruct((N_SLABS, seq_len, t // seq_len, SLAB_IN), F32)],
        scratch_shapes=[pltpu.VMEM((t, D_MODEL), BF16), pltpu.VMEM((N_SLABS, t, SLAB_IN), F32)],
        compiler_params=_params("arbitrary"),
        name="in_proj_cast",
    )(x2d, g, cos2, sin2, w_f32)


def _s5_param_kernel(lr_ref, li_ref, ls_ref, br_ref, bi_ref, ctr_ref, cti_ref,
                     pairr_ref, pairi_ref, wbr_ref, wbi_ref, cdr_ref, cdi_ref):
    lr = lr_ref[...]
    li = li_ref[...]
    dt = jnp.exp(ls_ref[...])
    mag = jnp.exp(lr * dt)
    ar = mag * jnp.cos(li * dt)
    ai = mag * jnp.sin(li * dt)
    den = lr * lr + li * li
    nr = ar - 1.0
    cr = (nr * lr + ai * li) / den
    ci = (ai * lr - nr * li) / den
    br = br_ref[...]
    bi = bi_ref[...]
    bbr = cr * br - ci * bi
    bbi = cr * bi + ci * br
    abr = ar * bbr - ai * bbi
    abi = ar * bbi + ai * bbr

    in_shape = (SLAB_IN, SLAB_STATE)
    same_in = (lax.broadcasted_iota(jnp.int32, in_shape, 0) // S5_GROUP
               == lax.broadcasted_iota(jnp.int32, in_shape, 1) // S5_STATE)
    out_shape = (SLAB_STATE, SLAB_IN)
    same_out = (lax.broadcasted_iota(jnp.int32, out_shape, 0) // S5_STATE
                == lax.broadcasted_iota(jnp.int32, out_shape, 1) // S5_GROUP)
    for s in range(N_SLABS):
        cs = slice(s * SLAB_STATE, (s + 1) * SLAB_STATE)
        for dst, top, bot in ((wbr_ref, bbr, abr), (wbi_ref, bbi, abi)):
            for k, part in enumerate((top, bot)):
                blk = jnp.tile(part[:, cs], (SLAB_GROUPS, 1))
                dst[s, k * SLAB_IN:(k + 1) * SLAB_IN, :] = jnp.where(same_in, blk, 0.0).astype(BF16)
        cdr_ref[s] = jnp.where(same_out, ctr_ref[cs, :], 0.0).astype(BF16)
        cdi_ref[s] = jnp.where(same_out, cti_ref[cs, :], 0.0).astype(BF16)

    rows = lax.broadcasted_iota(jnp.int32, (SUBLANES, N_STATE), 0)
    second = rows >= SUBLANES // 2
    full = lambda v: jnp.broadcast_to(v, (SUBLANES, N_STATE))
    pairr_ref[...] = jnp.where(second, full(ar * ar - ai * ai), full(ar))
    pairi_ref[...] = jnp.where(second, full(ar * ai + ai * ar), full(ai))


def _s5_params(lam_re, lam_im, log_step, b_re, b_im, c_re, c_im):
    lr = lam_re.reshape(1, N_STATE)
    li = lam_im.reshape(1, N_STATE)
    ls = jnp.repeat(log_step, S5_STATE).reshape(1, N_STATE)
    b_rows = lambda b: jnp.transpose(b, (2, 0, 1)).reshape(S5_GROUP, N_STATE)
    c_cols = lambda c: jnp.tile(jnp.transpose(c, (0, 2, 1)).reshape(N_STATE, S5_GROUP),
                                (1, SLAB_GROUPS))
    pair = jax.ShapeDtypeStruct((SUBLANES, N_STATE), F32)
    wb = jax.ShapeDtypeStruct((N_SLABS, 2 * SLAB_IN, SLAB_STATE), BF16)
    cd = jax.ShapeDtypeStruct((N_SLABS, SLAB_STATE, SLAB_IN), BF16)
    pairr, pairi, wbr, wbi, cdr, cdi = pl.pallas_call(
        _s5_param_kernel,
        out_shape=(pair, pair, wb, wb, cd, cd),
        name="s5_params",
    )(lr, li, ls, b_rows(b_re), b_rows(b_im), c_cols(c_re), c_cols(c_im))
    return (pairr, pairi), (wbr, wbi), (cdr, cdi)


def _cmul_add(xr, xi, pr, pi, vr, vi):
    return xr + (pr * vr - pi * vi), xi + (pr * vi + pi * vr)


def _s5_weight_specs(k_in):
    const3 = lambda *_: (0, 0, 0)
    const2 = lambda *_: (0, 0)
    return [
        pl.BlockSpec((SUBLANES, N_STATE), const2),
        pl.BlockSpec((SUBLANES, N_STATE), const2),
        pl.BlockSpec((N_SLABS, k_in, SLAB_STATE), const3),
        pl.BlockSpec((N_SLABS, k_in, SLAB_STATE), const3),
        pl.BlockSpec((N_SLABS, SLAB_STATE, SLAB_IN), const3),
        pl.BlockSpec((N_SLABS, SLAB_STATE, SLAB_IN), const3),
        pl.BlockSpec((1, S5_WIDTH), const2),
        pl.BlockSpec((S5_WIDTH, S5_WIDTH), const2),
        pl.BlockSpec((1, S5_WIDTH), const2),
    ]


def _s5_step_kernel(u_ref, x0r_ref, x0i_ref, pairr_ref, pairi_ref, wbr_ref, wbi_ref, cdr_ref, cdi_ref,
                    d_ref, wg_ref, bg_ref, o_ref, xlr_ref, xli_ref, wgb_ref, xr_s, xi_s, y_s, o_s):
    seq_len, n_seq = u_ref.shape[1], u_ref.shape[2]

    @pl.when(pl.program_id(0) == 0)
    def _():
        wgb_ref[...] = wg_ref[...].astype(BF16)

    for s in range(N_SLABS):
        cs = slice(s * SLAB_STATE, (s + 1) * SLAB_STATE)
        us = slice(s * SLAB_IN, (s + 1) * SLAB_IN)
        u = u_ref[s].reshape(seq_len * n_seq, SLAB_IN)
        ub = u.astype(BF16)
        xr_s[:, cs] = jnp.dot(ub, wbr_ref[s], preferred_element_type=F32)
        xi_s[:, cs] = jnp.dot(ub, wbi_ref[s], preferred_element_type=F32)

        a_r = jnp.broadcast_to(pairr_ref[0:1, cs], (SUBLANES, SLAB_STATE))
        a_i = jnp.broadcast_to(pairi_ref[0:1, cs], (SUBLANES, SLAB_STATE))
        for b in range(n_seq // SUBLANES):
            seqs = slice(b * SUBLANES, (b + 1) * SUBLANES)
            xr, xi = x0r_ref[seqs, cs], x0i_ref[seqs, cs]
            for t in range(seq_len):
                rows = slice(t * n_seq + b * SUBLANES, t * n_seq + (b + 1) * SUBLANES)
                xr, xi = _cmul_add(xr_s[rows, cs], xi_s[rows, cs], a_r, a_i, xr, xi)
                xr_s[rows, cs] = xr
                xi_s[rows, cs] = xi
            xlr_ref[seqs, cs] = xr
            xli_ref[seqs, cs] = xi

        y = (jnp.dot(xr_s[:, cs].astype(BF16), cdr_ref[s], preferred_element_type=F32)
             - jnp.dot(xi_s[:, cs].astype(BF16), cdi_ref[s], preferred_element_type=F32))
        y_s[:, us] = jax.nn.gelu(y + d_ref[:, us] * u)

    y = y_s[...]
    z = jnp.dot(y.astype(BF16), wgb_ref[...], preferred_element_type=F32) + bg_ref[...]
    o = y * jax.nn.sigmoid(z)
    for s in range(N_SLABS):
        for t in range(seq_len):
            o_s[s, pl.ds(t, n_seq, stride=seq_len), :] = (
                o[t * n_seq:(t + 1) * n_seq, s * SLAB_IN:(s + 1) * SLAB_IN])
    for s in range(N_SLABS):
        o_ref[:, s * SLAB_IN:(s + 1) * SLAB_IN] = o_s[s].astype(o_ref.dtype)


def _s5_step(u_t, x0, pairs, wb, cd, d, w_glu, b_glu, n_tile=32):
    _, seq_len, n_seq, _ = u_t.shape
    rows = n_tile * seq_len
    row = lambda i: (i, 0)
    st = jax.ShapeDtypeStruct((n_seq, N_STATE), F32)
    return pl.pallas_call(
        _s5_step_kernel,
        grid=(n_seq // n_tile,),
        in_specs=[pl.BlockSpec((N_SLABS, seq_len, n_tile, SLAB_IN), lambda i: (0, 0, i, 0)),
                  pl.BlockSpec((n_tile, N_STATE), row), pl.BlockSpec((n_tile, N_STATE), row)]
        + _s5_weight_specs(SLAB_IN),
        out_specs=(pl.BlockSpec((rows, S5_WIDTH), row), pl.BlockSpec((n_tile, N_STATE), row),
                   pl.BlockSpec((n_tile, N_STATE), row),
                   pl.BlockSpec((S5_WIDTH, S5_WIDTH), lambda i: (0, 0))),
        out_shape=(jax.ShapeDtypeStruct((n_seq * seq_len, S5_WIDTH), BF16), st, st,
                   jax.ShapeDtypeStruct((S5_WIDTH, S5_WIDTH), BF16)),
        scratch_shapes=[pltpu.VMEM((rows, N_STATE), F32), pltpu.VMEM((rows, N_STATE), F32),
                        pltpu.VMEM((rows, S5_WIDTH), F32),
                        pltpu.VMEM((N_SLABS, rows, SLAB_IN), F32)],
        compiler_params=_params("arbitrary"),
        name="s5_step",
    )(u_t, x0[0], x0[1], pairs[0], pairs[1], wb[0], wb[1], cd[0], cd[1], d, w_glu, b_glu)


def _s5_seq_kernel(u_ref, pairr_ref, pairi_ref, wbr_ref, wbi_ref, cdr_ref, cdi_ref, d_ref, wg_ref,
                   bg_ref, o_ref, xlr_ref, xli_ref, xr_s, xi_s, y_s, o_s, cbr_s, cbi_s, ul_s):
    n_rows = u_ref.shape[1]
    half = SUBLANES // 2
    tt = n_rows // half

    @pl.when(pl.program_id(0) == 0)
    def _():
        cbr_s[...] = jnp.zeros_like(cbr_s)
        cbi_s[...] = jnp.zeros_like(cbi_s)
        ul_s[...] = jnp.zeros_like(ul_s)

    first = lax.broadcasted_iota(jnp.int32, (SUBLANES, SLAB_IN), 0) < half

    def b_projection(s):
        cs = slice(s * SLAB_STATE, (s + 1) * SLAB_STATE)
        u = u_ref[s]
        shifted = pltpu.roll(u, half, 0)
        head = jnp.where(first, pltpu.roll(ul_s[s], half, 0), shifted[:SUBLANES])
        u_prev = jnp.concatenate([head, shifted[SUBLANES:]], axis=0)
        ul_s[s] = u[n_rows - SUBLANES:]
        ub = jnp.concatenate([u, u_prev], axis=1).astype(BF16)
        xr_s[:, cs] = jnp.dot(ub, wbr_ref[s], preferred_element_type=F32)
        xi_s[:, cs] = jnp.dot(ub, wbi_ref[s], preferred_element_type=F32)

    def scan(s):
        cs = slice(s * SLAB_STATE, (s + 1) * SLAB_STATE)
        pr = jnp.broadcast_to(pairr_ref[half:half + 1, cs], (SUBLANES, SLAB_STATE))
        pi = jnp.broadcast_to(pairi_ref[half:half + 1, cs], (SUBLANES, SLAB_STATE))
        cr, ci = cbr_s[:, cs], cbi_s[:, cs]
        for b in range(n_rows // SUBLANES):
            rows = slice(b * SUBLANES, (b + 1) * SUBLANES)
            cr, ci = _cmul_add(xr_s[rows, cs], xi_s[rows, cs], pr, pi, cr, ci)
            xr_s[rows, cs] = cr
            xi_s[rows, cs] = ci
        cbr_s[:, cs] = cr
        cbi_s[:, cs] = ci

    def c_projection(s):
        cs = slice(s * SLAB_STATE, (s + 1) * SLAB_STATE)
        us = slice(s * SLAB_IN, (s + 1) * SLAB_IN)
        y = (jnp.dot(xr_s[:, cs].astype(BF16), cdr_ref[s], preferred_element_type=F32)
             - jnp.dot(xi_s[:, cs].astype(BF16), cdi_ref[s], preferred_element_type=F32))
        y_s[:, us] = jax.nn.gelu(y + d_ref[:, us] * u_ref[s])

    stages = (b_projection, scan, c_projection)
    for step in range(N_SLABS + len(stages) - 1):
        for lag, stage in enumerate(stages):
            if 0 <= step - lag < N_SLABS:
                stage(step - lag)

    xlr_ref[...] = cbr_s[...]
    xli_ref[...] = cbi_s[...]

    y = y_s[...]
    z = jnp.dot(y.astype(BF16), wg_ref[...], preferred_element_type=F32) + bg_ref[...]
    o = y * jax.nn.sigmoid(z)
    for s in range(N_SLABS):
        o_s[s] = o[:, s * SLAB_IN:(s + 1) * SLAB_IN]
    for n in range(half):
        for s in range(N_SLABS):
            o_ref[n, :, s * SLAB_IN:(s + 1) * SLAB_IN] = (
                o_s[s, pl.ds(n, tt, stride=half), :].astype(o_ref.dtype))


def _s5_seq(u_t, pairs, wb, cd, d, w_glu, b_glu, n_seq, seq_len, tt=64):
    assert 2 * n_seq == SUBLANES
    n_rows = n_seq * tt
    st = jax.ShapeDtypeStruct((SUBLANES, N_STATE), F32)
    y, xlr, xli = pl.pallas_call(
        _s5_seq_kernel,
        grid=(seq_len // tt,),
        in_specs=[pl.BlockSpec((N_SLABS, n_rows, SLAB_IN), lambda i: (0, i, 0))]
        + _s5_weight_specs(2 * SLAB_IN),
        out_specs=(pl.BlockSpec((n_seq, tt, S5_WIDTH), lambda i: (0, i, 0)),
                   pl.BlockSpec((SUBLANES, N_STATE), lambda i: (0, 0)),
                   pl.BlockSpec((SUBLANES, N_STATE), lambda i: (0, 0))),
        out_shape=(jax.ShapeDtypeStruct((n_seq, seq_len, S5_WIDTH), BF16), st, st),
        scratch_shapes=[pltpu.VMEM((n_rows, N_STATE), F32), pltpu.VMEM((n_rows, N_STATE), F32),
                        pltpu.VMEM((n_rows, S5_WIDTH), F32),
                        pltpu.VMEM((N_SLABS, n_rows, SLAB_IN), F32),
                        pltpu.VMEM((SUBLANES, N_STATE), F32), pltpu.VMEM((SUBLANES, N_STATE), F32),
                        pltpu.VMEM((N_SLABS, SUBLANES, SLAB_IN), F32)],
        compiler_params=_params("arbitrary"),
        name="s5_seq",
    )(u_t, pairs[0], pairs[1], wb[0], wb[1], cd[0], cd[1], d, w_glu, b_glu)
    return y.reshape(n_seq * seq_len, S5_WIDTH), xlr[n_seq:], xli[n_seq:]


_NT = (((1,), (1,)), ((), ()))
_TN = (((0,), (0,)), ((), ()))


def _norm_gate(o, g, gn_w):
    mu = jnp.mean(o, axis=-1, keepdims=True)
    oc = o - mu
    var = jnp.mean(oc * oc, axis=-1, keepdims=True)
    return jax.nn.silu(g) * (oc * lax.rsqrt(var + NORM_EPS) * gn_w)


def _ret_seq_kernel(q_ref, k_ref, v_ref, g_ref, mask_ref, qd_ref, kd_ref, cd_ref,
                    gnw_ref, o_ref, rn_ref, r_s, sc_s, kv_s, *, chunks):
    @pl.when(pl.program_id(1) == 0)
    def _():
        r_s[...] = jnp.zeros_like(r_s)

    for c in range(chunks):
        rows = slice(c * RET_TILE, (c + 1) * RET_TILE)
        for h in range(RET_HEADS):
            hs = slice(h * HEAD_DIM, (h + 1) * HEAD_DIM)
            qb, kb, vb = q_ref[rows, hs], k_ref[rows, hs], v_ref[rows, hs]
            sc = lax.dot_general(qb, kb, _NT, preferred_element_type=F32) * mask_ref[h]
            sc_s[c, h] = sc.astype(BF16)
            k_dec = (kb.astype(F32) * kd_ref[h]).astype(BF16)
            kv_s[c, h] = lax.dot_general(k_dec, vb, _TN, preferred_element_type=F32)

    for c in range(chunks):
        rows = slice(c * RET_TILE, (c + 1) * RET_TILE)
        for h in range(RET_HEADS):
            hs = slice(h * HEAD_DIM, (h + 1) * HEAD_DIM)
            qb, vb = q_ref[rows, hs], v_ref[rows, hs]
            r_old = r_s[h]
            q_dec = (qb.astype(F32) * qd_ref[h]).astype(BF16)
            lhs = jnp.concatenate([sc_s[c, h], q_dec], axis=1)
            rhs = jnp.concatenate([vb, r_old.astype(BF16)], axis=0)
            o = jnp.dot(lhs, rhs, preferred_element_type=F32)
            r_s[h] = r_old * cd_ref[h] + kv_s[c, h]
            o_ref[rows, hs] = _norm_gate(o, g_ref[rows, hs].astype(F32),
                                         gnw_ref[:, hs]).astype(o_ref.dtype)
    rn_ref[0] = r_s[...]


def _ret_step_kernel(q_ref, k_ref, v_ref, g_ref, mask_ref, qd_ref, kd_ref, cd_ref,
                     gnw_ref, r0_ref, o_ref, rn_ref, qr_s, kdt_s, ob_s, *, n_seq):
    rows_per_seq = RET_TILE // n_seq
    for h in range(RET_HEADS):
        hs = slice(h * HEAD_DIM, (h + 1) * HEAD_DIM)
        qb, kb = q_ref[:, hs], k_ref[:, hs]
        sc = lax.dot_general(qb, kb, _NT, preferred_element_type=F32) * mask_ref[h]
        ob_s[:, hs] = jnp.dot(sc.astype(BF16), v_ref[:, hs], preferred_element_type=F32)
        qr_s[:, hs] = qb.astype(F32) * qd_ref[h]
        kdt_s[h] = (kb.astype(F32) * kd_ref[h]).T

    lane = lax.broadcasted_iota(jnp.int32, (HEAD_DIM, RET_TILE), 1)
    for s in range(n_seq):
        rows = slice(s * rows_per_seq, (s + 1) * rows_per_seq)
        in_seq = (lane >= s * rows_per_seq) & (lane < (s + 1) * rows_per_seq)
        for h in range(RET_HEADS):
            hs = slice(h * HEAD_DIM, (h + 1) * HEAD_DIM)
            r_old = r0_ref[s, h]
            ob_s[rows, hs] += jnp.dot(qr_s[rows, hs].astype(BF16), r_old.astype(BF16),
                                      preferred_element_type=F32)
            kdt = jnp.where(in_seq, kdt_s[h], 0.0).astype(BF16)
            rn_ref[s, h] = r_old * cd_ref[h] + jnp.dot(kdt, v_ref[:, hs], preferred_element_type=F32)

    for h in range(RET_HEADS):
        hs = slice(h * HEAD_DIM, (h + 1) * HEAD_DIM)
        o_ref[:, hs] = _norm_gate(ob_s[:, hs], g_ref[:, hs].astype(F32),
                                  gnw_ref[:, hs]).astype(o_ref.dtype)


def _decay_tables(chunk, n_seq):
    scale = HEAD_DIM ** -0.5
    log_gamma = np.log(1.0 - 2.0 ** (-5.0 - np.arange(RET_HEADS, dtype=np.float64)))
    idx = np.arange(chunk, dtype=np.float64)
    diff = idx[:, None] - idx[None, :]
    mask = np.where(diff >= 0, np.exp(log_gamma[:, None, None] * np.maximum(diff, 0.0)), 0.0)
    q_decay = np.exp(log_gamma[:, None] * (idx + 1.0))
    k_decay = np.exp(log_gamma[:, None] * (chunk - 1.0 - idx)) * scale
    chunk_decay = np.exp(log_gamma * chunk)
    mask_t = scale * np.einsum("hab,st->hsatb", mask, np.eye(n_seq)).reshape(
        RET_HEADS, RET_TILE, RET_TILE)
    qd_t = np.broadcast_to(np.tile(q_decay, (1, n_seq))[:, :, None], (RET_HEADS, RET_TILE, HEAD_DIM))
    kd_t = np.broadcast_to(np.tile(k_decay, (1, n_seq))[:, :, None], (RET_HEADS, RET_TILE, HEAD_DIM))
    const = lambda a: jnp.asarray(np.ascontiguousarray(a), dtype=F32)
    return const(mask_t), const(qd_t), const(kd_t), const(chunk_decay)


def _retention(proj, r0, gn_w, n_seq, seq_len, chunks=4):
    t = proj.shape[0]
    const3 = lambda *_: (0, 0, 0)
    if r0 is None:
        tile, tile_seqs = chunks * RET_TILE, 1
        steps = seq_len // tile
        grid = (n_seq, steps)
        row = lambda n, c: n * steps + c
        state_map = lambda n, c: (n, 0, 0, 0)
        state_block = (1, RET_HEADS, HEAD_DIM, HEAD_DIM)
        sem = ("arbitrary", "arbitrary")
        body = functools.partial(_ret_seq_kernel, chunks=chunks)
        scratch = [pltpu.VMEM((RET_HEADS, HEAD_DIM, HEAD_DIM), F32),
                   pltpu.VMEM((chunks, RET_HEADS, RET_TILE, RET_TILE), BF16),
                   pltpu.VMEM((chunks, RET_HEADS, HEAD_DIM, HEAD_DIM), F32)]
        name = "ret_seq"
    else:
        tile, tile_seqs = RET_TILE, RET_TILE // seq_len
        grid = (t // tile,)
        row = lambda i: i
        state_map = lambda i: (i, 0, 0, 0)
        state_block = (tile_seqs, RET_HEADS, HEAD_DIM, HEAD_DIM)
        sem = ("arbitrary",)
        body = functools.partial(_ret_step_kernel, n_seq=tile_seqs)
        scratch = [pltpu.VMEM((RET_TILE, RET_WIDTH), F32),
                   pltpu.VMEM((RET_HEADS, HEAD_DIM, RET_TILE), F32),
                   pltpu.VMEM((RET_TILE, RET_WIDTH), F32)]
        name = "ret_step"
    mask_t, qd_t, kd_t, cd = _decay_tables(RET_TILE // tile_seqs, tile_seqs)

    def col(cb):
        return pl.BlockSpec((tile, RET_WIDTH), lambda *a: (row(*a), cb))

    in_specs = [
        col(1), col(2), col(3), col(4),
        pl.BlockSpec((RET_HEADS, RET_TILE, RET_TILE), const3),
        pl.BlockSpec((RET_HEADS, RET_TILE, HEAD_DIM), const3),
        pl.BlockSpec((RET_HEADS, RET_TILE, HEAD_DIM), const3),
        pl.BlockSpec(memory_space=pltpu.SMEM),
        pl.BlockSpec((1, RET_WIDTH), lambda *_: (0, 0)),
    ]
    args = [proj, proj, proj, proj, mask_t, qd_t, kd_t, cd, gn_w]
    if r0 is not None:
        in_specs.append(pl.BlockSpec(state_block, state_map))
        args.append(r0)
    return pl.pallas_call(
        body,
        grid=grid,
        in_specs=in_specs,
        out_specs=(pl.BlockSpec((tile, RET_WIDTH), lambda *a: (row(*a), 0)),
                   pl.BlockSpec(state_block, state_map)),
        out_shape=(jax.ShapeDtypeStruct((t, RET_WIDTH), BF16),
                   jax.ShapeDtypeStruct((n_seq, RET_HEADS, HEAD_DIM, HEAD_DIM), F32)),
        scratch_shapes=scratch,
        compiler_params=_params(*sem),
        name=name,
    )(*args)


def _outproj_kernel(x_ref, a_ref, b_ref, wa_ref, wb_ref, o_ref, *w_bf16, cast):
    if cast:
        (wo_ref,) = w_bf16

        @pl.when(pl.program_id(0) == 0)
        def _():
            wo_ref[:S5_WIDTH, :] = wa_ref[...].astype(BF16)
            wo_ref[S5_WIDTH:, :] = wb_ref[...].astype(BF16)

        wa, wb = wo_ref[:S5_WIDTH, :], wo_ref[S5_WIDTH:, :]
    else:
        wa, wb = wa_ref[...], wb_ref[...]
    o_ref[...] = (x_ref[...]
                  + jnp.dot(a_ref[...], wa, preferred_element_type=F32)
                  + jnp.dot(b_ref[...], wb, preferred_element_type=F32))


def _outproj(x2d, a, b, w, tm=512):
    t = x2d.shape[0]
    cast = w.dtype == F32
    resident = pl.Buffered(1)
    out_specs = [pl.BlockSpec((tm, D_MODEL), lambda i: (i, 0))]
    out_shape = [jax.ShapeDtypeStruct((t, D_MODEL), F32)]
    if cast:
        out_specs.append(pl.BlockSpec((S5_WIDTH + RET_WIDTH, D_MODEL), lambda i: (0, 0)))
        out_shape.append(jax.ShapeDtypeStruct((S5_WIDTH + RET_WIDTH, D_MODEL), BF16))
    return pl.pallas_call(
        functools.partial(_outproj_kernel, cast=cast),
        grid=(t // tm,),
        in_specs=[
            pl.BlockSpec((tm, D_MODEL), lambda i: (i, 0)),
            pl.BlockSpec((tm, S5_WIDTH), lambda i: (i, 0)),
            pl.BlockSpec((tm, RET_WIDTH), lambda i: (i, 0)),
            pl.BlockSpec((S5_WIDTH, D_MODEL), lambda i: (0, 0), pipeline_mode=resident),
            pl.BlockSpec((RET_WIDTH, D_MODEL), lambda i: (1, 0), pipeline_mode=resident),
        ],
        out_specs=out_specs,
        out_shape=out_shape,
        compiler_params=_params("arbitrary"),
        name="out_proj_cast" if cast else "out_proj",
    )(x2d, a, b, w, w)


def _ffn_kernel(x_ref, gn_ref, wg_ref, wu_ref, wo_ref, gf_ref, o_ref, *rest, cast):
    if cast:
        wgb_ref, wub_ref, wob_ref, h_scr = rest
        wgb_ref[...] = wg_ref[...].astype(BF16)
        wub_ref[...] = wu_ref[...].astype(BF16)
        wob_ref[...] = wo_ref[...].astype(BF16)
        wg_ref, wu_ref, wo_ref = wgb_ref, wub_ref, wob_ref
    else:
        (h_scr,) = rest
    j = pl.program_id(1)
    last_j = pl.num_programs(1) - 1

    def step(first, last):
        for r in range(x_ref.shape[0] // FFN_ROWS):
            rows = slice(r * FFN_ROWS, (r + 1) * FFN_ROWS)
            if first:
                h_scr[rows, :] = _rms(x_ref[rows, :], gn_ref[...]).astype(BF16)
            h = h_scr[rows, :]
            gate = jnp.dot(h, wg_ref[...], preferred_element_type=F32)
            up = jnp.dot(h, wu_ref[...], preferred_element_type=F32)
            act = (jax.nn.silu(gate) * up).astype(BF16)
            acc = (x_ref if first else o_ref)[rows, :] + jnp.dot(act, wo_ref[...],
                                                                  preferred_element_type=F32)
            o_ref[rows, :] = _rms(acc, gf_ref[...]) if last else acc

    pl.when(j == 0)(lambda: step(True, False))
    pl.when((j > 0) & (j < last_j))(lambda: step(False, False))
    pl.when(j == last_j)(lambda: step(False, True))


def _ffn(x2d, g_ffn, w_gate, w_up, w_down, g_final, tm=1024, tf=512):
    t = x2d.shape[0]
    nf = D_FF // tf
    cast = w_down.dtype == F32
    (wg, g0), (wu, u0) = w_gate, w_up
    g0, u0 = g0 * nf, u0 * nf
    x_mode = {}
    out_specs = [pl.BlockSpec((tm, D_MODEL), lambda i, j: (i, 0))]
    out_shape = [jax.ShapeDtypeStruct((t, D_MODEL), F32)]
    if cast:
        assert t == tm, "the bf16 weight outputs are written once per row tile"
        x_mode = dict(pipeline_mode=pl.Buffered(1))
        out_specs += [pl.BlockSpec((D_MODEL, tf), lambda i, j: (0, j)),
                      pl.BlockSpec((D_MODEL, tf), lambda i, j: (0, j)),
                      pl.BlockSpec((tf, D_MODEL), lambda i, j: (j, 0))]
        out_shape += [jax.ShapeDtypeStruct((D_MODEL, D_FF), BF16),
                      jax.ShapeDtypeStruct((D_MODEL, D_FF), BF16),
                      jax.ShapeDtypeStruct((D_FF, D_MODEL), BF16)]
    return pl.pallas_call(
        functools.partial(_ffn_kernel, cast=cast),
        grid=(t // tm, nf),
        in_specs=[
            pl.BlockSpec((tm, D_MODEL), lambda i, j: (i, 0), **x_mode),
            pl.BlockSpec((1, D_MODEL), lambda i, j: (0, 0)),
            pl.BlockSpec((D_MODEL, tf), lambda i, j: (0, j + g0)),
            pl.BlockSpec((D_MODEL, tf), lambda i, j: (0, j + u0)),
            pl.BlockSpec((tf, D_MODEL), lambda i, j: (j, 0)),
            pl.BlockSpec((1, D_MODEL), lambda i, j: (0, 0)),
        ],
        out_specs=out_specs,
        out_shape=out_shape,
        scratch_shapes=[pltpu.VMEM((tm, D_MODEL), BF16)],
        compiler_params=_params("arbitrary", "arbitrary"),
        name="ffn_cast" if cast else "ffn",
    )(x2d, g_ffn, wg, wu, w_down, g_final)


def _finish(x2d, proj, s5_out, ret_state, w, w_out, ffn_w, ffn_tf, n, l):
    ret_out, ret_new = _retention(proj, ret_state, w["gn_w"], n, l)
    x1, *w_out_b = _outproj(x2d, s5_out, ret_out, w_out)
    y, *ffn_b = _ffn(x1, w["norm_ffn"], *ffn_w, w["norm_final"], tf=ffn_tf)
    return y.reshape(n, l, D_MODEL), ret_new, w_out_b, ffn_b


def _as_groups(re, im, n):
    return re.reshape(n, S5_GROUPS, S5_STATE), im.reshape(n, S5_GROUPS, S5_STATE)


def _sample_layer(x, s5_state, ret_state, w, big):
    n, l, _ = x.shape
    x2d = x.reshape(n * l, D_MODEL)
    proj, w_in_b, u_t = _inproj_cast(x2d, w["norm_mix"], big["w_in"], l, float(PAST_LEN))
    s5_out, s5_re, s5_im, w_glu_b = _s5_step(u_t, s5_state, w["pairs"], w["wb"], w["cd"], w["d"],
                                             big["w_glu"], w["b_glu"])
    ffn_f32 = ((big["w_ffn_in"], 0), (big["w_ffn_in"], 1), big["w_ffn_out"])
    y, ret_new, (w_out_b,), (wg_b, wu_b, wd_b) = _finish(
        x2d, proj, s5_out, ret_state, w, big["w_out"], ffn_f32, 256, n, l)
    big_b = dict(w_in=w_in_b, w_glu=w_glu_b, w_out=w_out_b, ffn=((wg_b, 0), (wu_b, 0), wd_b))
    return (y, *_as_groups(s5_re, s5_im, n), ret_new), big_b


def _prompt_layer(x, w, big_b):
    n, l, _ = x.shape
    x2d = x.reshape(n * l, D_MODEL)
    proj, u_t = _inproj_seq(x2d, w["norm_mix"], big_b["w_in"], n, 0.0)
    s5_out, s5_re, s5_im = _s5_seq(u_t, w["pairs"], w["wb"], w["cd"], w["d"], big_b["w_glu"],
                                   w["b_glu"], n, l)
    y, ret_new, _, _ = _finish(x2d, proj, s5_out, None, w, big_b["w_out"], big_b["ffn"],
                               512, n, l)
    return (y, *_as_groups(s5_re, s5_im, n), ret_new)


def kernel(x_prompt, x_sample, state_s5_re, state_s5_im, state_ret, norm_mix, w_in, s5_lambda_re, s5_lambda_im, s5_log_step, s5_b_re, s5_b_im, s5_c_re, s5_c_im, s5_d, s5_w_glu, s5_b_glu, ret_gn_w, w_out, norm_ffn, w_ffn_in, w_ffn_out, norm_final):
    assert norm_mix.shape[0] == 1, "single-layer stack"
    pairs, wb, cd = _s5_params(s5_lambda_re[0], s5_lambda_im[0], s5_log_step[0],
                               s5_b_re[0], s5_b_im[0], s5_c_re[0], s5_c_im[0])
    w = dict(norm_mix=norm_mix, pairs=pairs, wb=wb, cd=cd, d=s5_d, b_glu=s5_b_glu,
             gn_w=ret_gn_w, norm_ffn=norm_ffn, norm_final=norm_final.reshape(1, D_MODEL))
    big = dict(w_in=w_in[0], w_glu=s5_w_glu[0], w_out=w_out[0],
               w_ffn_in=w_ffn_in[0], w_ffn_out=w_ffn_out[0])
    n_s = x_sample.shape[0]
    (ys, s_re, s_im, s_ret), big_b = _sample_layer(
        x_sample,
        (state_s5_re[0].reshape(n_s, N_STATE), state_s5_im[0].reshape(n_s, N_STATE)),
        state_ret[0], w, big)
    yp, p_re, p_im, p_ret = _prompt_layer(x_prompt, w, big_b)
    return (yp, ys, p_re[None], p_im[None], p_ret[None], s_re[None], s_im[None], s_ret[None])
```

```python
import functools
import math

import jax
import jax.numpy as jnp
import numpy as np
from jax import lax
from jax.experimental import pallas as pl
from jax.experimental.pallas import tpu as pltpu

F32 = jnp.float32
BF16 = jnp.bfloat16

D_MODEL = 2048
S5_WIDTH = 1024
S5_GROUP = 16
S5_GROUPS = 64
S5_STATE = 64
N_STATE = S5_GROUPS * S5_STATE
RET_WIDTH = 1024
RET_HEADS = 8
HEAD_DIM = 128
ROPE_BASE = 10000.0
D_FF = 5632
IN_WIDTH = S5_WIDTH + 4 * RET_WIDTH
NORM_EPS = 1e-6
PAST_LEN = 16384

SUBLANES = 8
SLAB_GROUPS = 8
N_SLABS = S5_GROUPS // SLAB_GROUPS
SLAB_IN = SLAB_GROUPS * S5_GROUP
SLAB_STATE = SLAB_GROUPS * S5_STATE
RET_TILE = 128
ROTARY_TILES = (1, 2)
FFN_ROWS = 512

VMEM_LIMIT_BYTES = 60 * 1024 * 1024


def _params(*sem):
    return pltpu.CompilerParams(dimension_semantics=sem, vmem_limit_bytes=VMEM_LIMIT_BYTES)


def _rms(xf, g):
    ms = jnp.mean(xf * xf, axis=-1, keepdims=True)
    return xf * lax.rsqrt(ms + NORM_EPS) * g


def _rotary(x, cos2, sin2):
    return x * cos2 + pltpu.roll(x, HEAD_DIM // 2, 1) * sin2


def _rotary_heads(x, cos2, sin2):
    return jnp.concatenate(
        [_rotary(x[:, h * HEAD_DIM:(h + 1) * HEAD_DIM], cos2, sin2) for h in range(RET_HEADS)],
        axis=1)


def _rotary_tables(pos0, rows, reps):
    half = HEAD_DIM // 2
    inv_freq = ROPE_BASE ** (-np.arange(half, dtype=np.float64) / half)
    pos = pos0 + np.arange(rows, dtype=np.float64)
    ang = pos[:, None] * inv_freq[None, :]
    cos, sin = np.cos(ang), np.sin(ang)
    cos2 = np.concatenate([cos, cos], axis=-1)
    sin2 = np.concatenate([-sin, sin], axis=-1)
    return (jnp.asarray(np.tile(cos2, (reps, 1)), dtype=F32),
            jnp.asarray(np.tile(sin2, (reps, 1)), dtype=F32))


def _inproj_seq_kernel(x_ref, g_ref, cos_ref, sin_ref, w_ref, o_ref, ou_ref, h_scr, *, tn, n_seq):
    seq = pl.program_id(1)
    tm = x_ref.shape[0]
    h_scr[...] = _rms(x_ref[...], g_ref[...]).astype(BF16)
    for j in range(IN_WIDTH // tn):
        cols = slice(j * tn, (j + 1) * tn)
        res = jnp.dot(h_scr[...], w_ref[:, cols], preferred_element_type=F32)
        if j * tn < S5_WIDTH:
            dst = pl.ds(seq, tm, stride=n_seq)
            for s in range(tn // SLAB_IN):
                ou_ref[j * (tn // SLAB_IN) + s, dst, :] = res[:, s * SLAB_IN:(s + 1) * SLAB_IN]
        if j in ROTARY_TILES:
            res = _rotary_heads(res, cos_ref[...], sin_ref[...])
        o_ref[:, cols] = res.astype(o_ref.dtype)


def _inproj_seq(x2d, g, w, n_seq, pos0, tm=256, tn=RET_WIDTH):
    t = x2d.shape[0]
    tiles = t // n_seq // tm
    row = lambda i, n: (n * tiles + i, 0)
    const = lambda i, n: (0, 0)
    cos2, sin2 = _rotary_tables(pos0, t // n_seq, 1)
    return pl.pallas_call(
        functools.partial(_inproj_seq_kernel, tn=tn, n_seq=n_seq),
        grid=(tiles, n_seq),
        in_specs=[
            pl.BlockSpec((tm, D_MODEL), row),
            pl.BlockSpec((1, D_MODEL), const),
            pl.BlockSpec((tm, HEAD_DIM), lambda i, n: (i, 0)),
            pl.BlockSpec((tm, HEAD_DIM), lambda i, n: (i, 0)),
            pl.BlockSpec((D_MODEL, IN_WIDTH), const, pipeline_mode=pl.Buffered(1)),
        ],
        out_specs=[pl.BlockSpec((tm, IN_WIDTH), row),
                   pl.BlockSpec((N_SLABS, n_seq * tm, SLAB_IN), lambda i, n: (0, i, 0))],
        out_shape=[jax.ShapeDtypeStruct((t, IN_WIDTH), BF16),
                   jax.ShapeDtypeStruct((N_SLABS, t, SLAB_IN), F32)],
        scratch_shapes=[pltpu.VMEM((tm, D_MODEL), BF16)],
        compiler_params=_params("arbitrary", "arbitrary"),
        name="in_proj_seq",
    )(x2d, g, cos2, sin2, w)


def _inproj_cast_kernel(x_ref, g_ref, cos_ref, sin_ref, w_ref, o_ref, wb_ref, ou_ref, h_scr, ru_s):
    j = pl.program_id(0)

    @pl.when(j == 0)
    def _():
        h_scr[...] = _rms(x_ref[...], g_ref[...]).astype(BF16)

    wb_ref[...] = w_ref[...].astype(BF16)
    res = jnp.dot(h_scr[...], wb_ref[...], preferred_element_type=F32)
    is_rotary = functools.reduce(jnp.logical_or, [j == r for r in ROTARY_TILES])

    @pl.when(is_rotary)
    def _():
        o_ref[...] = _rotary_heads(res, cos_ref[...], sin_ref[...]).astype(o_ref.dtype)

    @pl.when(jnp.logical_not(is_rotary))
    def _():
        o_ref[...] = res.astype(o_ref.dtype)

    @pl.when(j == 0)
    def _():
        seq_len, n_seq = ou_ref.shape[1], ou_ref.shape[2]
        for s in range(N_SLABS):
            ru_s[s] = res[:, s * SLAB_IN:(s + 1) * SLAB_IN]
        for s in range(N_SLABS):
            for t in range(seq_len):
                ou_ref[s, t] = ru_s[s, pl.ds(t, n_seq, stride=seq_len), :]


def _inproj_cast(x2d, g, w_f32, seq_len, pos0):
    t = x2d.shape[0]
    tn = RET_WIDTH
    cos2, sin2 = _rotary_tables(pos0, seq_len, t // seq_len)
    return pl.pallas_call(
        _inproj_cast_kernel,
        grid=(IN_WIDTH // tn,),
        in_specs=[
            pl.BlockSpec((t, D_MODEL), lambda j: (0, 0), pipeline_mode=pl.Buffered(1)),
            pl.BlockSpec((1, D_MODEL), lambda j: (0, 0)),
            pl.BlockSpec((t, HEAD_DIM), lambda j: (0, 0)),
            pl.BlockSpec((t, HEAD_DIM), lambda j: (0, 0)),
            pl.BlockSpec((D_MODEL, tn), lambda j: (0, j)),
        ],
        out_specs=[pl.BlockSpec((t, tn), lambda j: (0, j)),
                   pl.BlockSpec((D_MODEL, tn), lambda j: (0, j)),
                   pl.BlockSpec((N_SLABS, seq_len, t // seq_len, SLAB_IN), lambda j: (0, 0, 0, 0))],
        out_shape=[jax.ShapeDtypeStruct((t, IN_WIDTH), BF16),
                   jax.ShapeDtypeStruct((D_MODEL, IN_WIDTH), BF16),
                   jax.ShapeDtypeStruct((N_SLABS, seq_len, t // seq_len, SLAB_IN), F32)],
        scratch_shapes=[pltpu.VMEM((t, D_MODEL), BF16), pltpu.VMEM((N_SLABS, t, SLAB_IN), F32)],
        compiler_params=_params("arbitrary"),
        name="in_proj_cast",
    )(x2d, g, cos2, sin2, w_f32)


def _s5_param_kernel(lr_ref, li_ref, ls_ref, br_ref, bi_ref, ctr_ref, cti_ref,
                     pairr_ref, pairi_ref, wbr_ref, wbi_ref, cdr_ref, cdi_ref):
    lr = lr_ref[...]
    li = li_ref[...]
    dt = jnp.exp(ls_ref[...])
    mag = jnp.exp(lr * dt)
    ar = mag * jnp.cos(li * dt)
    ai = mag * jnp.sin(li * dt)
    den = lr * lr + li * li
    nr = ar - 1.0
    cr = (nr * lr + ai * li) / den
    ci = (ai * lr - nr * li) / den
    br = br_ref[...]
    bi = bi_ref[...]
    bbr = cr * br - ci * bi
    bbi = cr * bi + ci * br
    abr = ar * bbr - ai * bbi
    abi = ar * bbi + ai * bbr

    in_shape = (SLAB_IN, SLAB_STATE)
    same_in = (lax.broadcasted_iota(jnp.int32, in_shape, 0) // S5_GROUP
               == lax.broadcasted_iota(jnp.int32, in_shape, 1) // S5_STATE)
    out_shape = (SLAB_STATE, SLAB_IN)
    same_out = (lax.broadcasted_iota(jnp.int32, out_shape, 0) // S5_STATE
                == lax.broadcasted_iota(jnp.int32, out_shape, 1) // S5_GROUP)
    for s in range(N_SLABS):
        cs = slice(s * SLAB_STATE, (s + 1) * SLAB_STATE)
        for dst, top, bot in ((wbr_ref, bbr, abr), (wbi_ref, bbi, abi)):
            for k, part in enumerate((top, bot)):
                blk = jnp.tile(part[:, cs], (SLAB_GROUPS, 1))
                dst[s, k * SLAB_IN:(k + 1) * SLAB_IN, :] = jnp.where(same_in, blk, 0.0).astype(BF16)
        cdr_ref[s] = jnp.where(same_out, ctr_ref[cs, :], 0.0).astype(BF16)
        cdi_ref[s] = jnp.where(same_out, cti_ref[cs, :], 0.0).astype(BF16)

    rows = lax.broadcasted_iota(jnp.int32, (SUBLANES, N_STATE), 0)
    second = rows >= SUBLANES // 2
    full = lambda v: jnp.broadcast_to(v, (SUBLANES, N_STATE))
    pairr_ref[...] = jnp.where(second, full(ar * ar - ai * ai), full(ar))
    pairi_ref[...] = jnp.where(second, full(ar * ai + ai * ar), full(ai))


def _s5_params(lam_re, lam_im, log_step, b_re, b_im, c_re, c_im):
    lr = lam_re.reshape(1, N_STATE)
    li = lam_im.reshape(1, N_STATE)
    ls = jnp.repeat(log_step, S5_STATE).reshape(1, N_STATE)
    b_rows = lambda b: jnp.transpose(b, (2, 0, 1)).reshape(S5_GROUP, N_STATE)
    c_cols = lambda c: jnp.tile(jnp.transpose(c, (0, 2, 1)).reshape(N_STATE, S5_GROUP),
                                (1, SLAB_GROUPS))
    pair = jax.ShapeDtypeStruct((SUBLANES, N_STATE), F32)
    wb = jax.ShapeDtypeStruct((N_SLABS, 2 * SLAB_IN, SLAB_STATE), BF16)
    cd = jax.ShapeDtypeStruct((N_SLABS, SLAB_STATE, SLAB_IN), BF16)
    pairr, pairi, wbr, wbi, cdr, cdi = pl.pallas_call(
        _s5_param_kernel,
        out_shape=(pair, pair, wb, wb, cd, cd),
        name="s5_params",
    )(lr, li, ls, b_rows(b_re), b_rows(b_im), c_cols(c_re), c_cols(c_im))
    return (pairr, pairi), (wbr, wbi), (cdr, cdi)


def _emit_pipelined(stages, n):
    for step in range(n + len(stages) - 1):
        for lag, stage in enumerate(stages):
            if 0 <= step - lag < n:
                stage(step - lag)


def _cmul_add(xr, xi, pr, pi, vr, vi):
    return xr + (pr * vr - pi * vi), xi + (pr * vi + pi * vr)


def _s5_weight_specs(k_in):
    const3 = lambda *_: (0, 0, 0)
    const2 = lambda *_: (0, 0)
    return [
        pl.BlockSpec((SUBLANES, N_STATE), const2),
        pl.BlockSpec((SUBLANES, N_STATE), const2),
        pl.BlockSpec((N_SLABS, k_in, SLAB_STATE), const3),
        pl.BlockSpec((N_SLABS, k_in, SLAB_STATE), const3),
        pl.BlockSpec((N_SLABS, SLAB_STATE, SLAB_IN), const3),
        pl.BlockSpec((N_SLABS, SLAB_STATE, SLAB_IN), const3),
        pl.BlockSpec((1, S5_WIDTH), const2),
        pl.BlockSpec((S5_WIDTH, S5_WIDTH), const2),
        pl.BlockSpec((1, S5_WIDTH), const2),
    ]


def _s5_step_kernel(u_ref, x0r_ref, x0i_ref, pairr_ref, pairi_ref, wbr_ref, wbi_ref, cdr_ref, cdi_ref,
                    d_ref, wg_ref, bg_ref, o_ref, xlr_ref, xli_ref, wgb_ref, xr_s, xi_s, y_s, o_s):
    seq_len, n_seq = u_ref.shape[1], u_ref.shape[2]

    @pl.when(pl.program_id(0) == 0)
    def _():
        wgb_ref[...] = wg_ref[...].astype(BF16)

    def b_projection(s):
        cs = slice(s * SLAB_STATE, (s + 1) * SLAB_STATE)
        ub = u_ref[s].reshape(seq_len * n_seq, SLAB_IN).astype(BF16)
        xr_s[:, cs] = jnp.dot(ub, wbr_ref[s], preferred_element_type=F32)
        xi_s[:, cs] = jnp.dot(ub, wbi_ref[s], preferred_element_type=F32)

    def scan(s):
        cs = slice(s * SLAB_STATE, (s + 1) * SLAB_STATE)
        a_r = jnp.broadcast_to(pairr_ref[0:1, cs], (SUBLANES, SLAB_STATE))
        a_i = jnp.broadcast_to(pairi_ref[0:1, cs], (SUBLANES, SLAB_STATE))
        for b in range(n_seq // SUBLANES):
            seqs = slice(b * SUBLANES, (b + 1) * SUBLANES)
            xr, xi = x0r_ref[seqs, cs], x0i_ref[seqs, cs]
            for t in range(seq_len):
                rows = slice(t * n_seq + b * SUBLANES, t * n_seq + (b + 1) * SUBLANES)
                xr, xi = _cmul_add(xr_s[rows, cs], xi_s[rows, cs], a_r, a_i, xr, xi)
                xr_s[rows, cs] = xr
                xi_s[rows, cs] = xi
            xlr_ref[seqs, cs] = xr
            xli_ref[seqs, cs] = xi

    def c_projection(s):
        cs = slice(s * SLAB_STATE, (s + 1) * SLAB_STATE)
        us = slice(s * SLAB_IN, (s + 1) * SLAB_IN)
        u = u_ref[s].reshape(seq_len * n_seq, SLAB_IN)
        y = (jnp.dot(xr_s[:, cs].astype(BF16), cdr_ref[s], preferred_element_type=F32)
             - jnp.dot(xi_s[:, cs].astype(BF16), cdi_ref[s], preferred_element_type=F32))
        y_s[:, us] = jax.nn.gelu(y + d_ref[:, us] * u)

    _emit_pipelined((b_projection, scan, c_projection), N_SLABS)

    y = y_s[...]
    z = jnp.dot(y.astype(BF16), wgb_ref[...], preferred_element_type=F32) + bg_ref[...]
    o = y * jax.nn.sigmoid(z)
    for s in range(N_SLABS):
        for t in range(seq_len):
            o_s[s, pl.ds(t, n_seq, stride=seq_len), :] = (
                o[t * n_seq:(t + 1) * n_seq, s * SLAB_IN:(s + 1) * SLAB_IN])
    for s in range(N_SLABS):
        o_ref[:, s * SLAB_IN:(s + 1) * SLAB_IN] = o_s[s].astype(o_ref.dtype)


def _s5_step(u_t, x0, pairs, wb, cd, d, w_glu, b_glu, n_tile=32):
    _, seq_len, n_seq, _ = u_t.shape
    rows = n_tile * seq_len
    row = lambda i: (i, 0)
    st = jax.ShapeDtypeStruct((n_seq, N_STATE), F32)
    return pl.pallas_call(
        _s5_step_kernel,
        grid=(n_seq // n_tile,),
        in_specs=[pl.BlockSpec((N_SLABS, seq_len, n_tile, SLAB_IN), lambda i: (0, 0, i, 0)),
                  pl.BlockSpec((n_tile, N_STATE), row), pl.BlockSpec((n_tile, N_STATE), row)]
        + _s5_weight_specs(SLAB_IN),
        out_specs=(pl.BlockSpec((rows, S5_WIDTH), row), pl.BlockSpec((n_tile, N_STATE), row),
                   pl.BlockSpec((n_tile, N_STATE), row),
                   pl.BlockSpec((S5_WIDTH, S5_WIDTH), lambda i: (0, 0))),
        out_shape=(jax.ShapeDtypeStruct((n_seq * seq_len, S5_WIDTH), BF16), st, st,
                   jax.ShapeDtypeStruct((S5_WIDTH, S5_WIDTH), BF16)),
        scratch_shapes=[pltpu.VMEM((rows, N_STATE), F32), pltpu.VMEM((rows, N_STATE), F32),
                        pltpu.VMEM((rows, S5_WIDTH), F32),
                        pltpu.VMEM((N_SLABS, rows, SLAB_IN), F32)],
        compiler_params=_params("arbitrary"),
        name="s5_step",
    )(u_t, x0[0], x0[1], pairs[0], pairs[1], wb[0], wb[1], cd[0], cd[1], d, w_glu, b_glu)


def _s5_seq_kernel(u_ref, pairr_ref, pairi_ref, wbr_ref, wbi_ref, cdr_ref, cdi_ref, d_ref, wg_ref,
                   bg_ref, o_ref, xlr_ref, xli_ref, xr_s, xi_s, y_s, o_s, cbr_s, cbi_s, ul_s):
    n_rows = u_ref.shape[1]
    half = SUBLANES // 2
    tt = n_rows // half

    @pl.when(pl.program_id(0) == 0)
    def _():
        cbr_s[...] = jnp.zeros_like(cbr_s)
        cbi_s[...] = jnp.zeros_like(cbi_s)
        ul_s[...] = jnp.zeros_like(ul_s)

    first = lax.broadcasted_iota(jnp.int32, (SUBLANES, SLAB_IN), 0) < half

    def b_projection(s):
        cs = slice(s * SLAB_STATE, (s + 1) * SLAB_STATE)
        u = u_ref[s]
        shifted = pltpu.roll(u, half, 0)
        head = jnp.where(first, pltpu.roll(ul_s[s], half, 0), shifted[:SUBLANES])
        u_prev = jnp.concatenate([head, shifted[SUBLANES:]], axis=0)
        ul_s[s] = u[n_rows - SUBLANES:]
        ub = jnp.concatenate([u, u_prev], axis=1).astype(BF16)
        xr_s[:, cs] = jnp.dot(ub, wbr_ref[s], preferred_element_type=F32)
        xi_s[:, cs] = jnp.dot(ub, wbi_ref[s], preferred_element_type=F32)

    def scan(s):
        cs = slice(s * SLAB_STATE, (s + 1) * SLAB_STATE)
        pr = jnp.broadcast_to(pairr_ref[half:half + 1, cs], (SUBLANES, SLAB_STATE))
        pi = jnp.broadcast_to(pairi_ref[half:half + 1, cs], (SUBLANES, SLAB_STATE))
        cr, ci = cbr_s[:, cs], cbi_s[:, cs]
        for b in range(n_rows // SUBLANES):
            rows = slice(b * SUBLANES, (b + 1) * SUBLANES)
            cr, ci = _cmul_add(xr_s[rows, cs], xi_s[rows, cs], pr, pi, cr, ci)
            xr_s[rows, cs] = cr
            xi_s[rows, cs] = ci
        cbr_s[:, cs] = cr
        cbi_s[:, cs] = ci

    def c_projection(s):
        cs = slice(s * SLAB_STATE, (s + 1) * SLAB_STATE)
        us = slice(s * SLAB_IN, (s + 1) * SLAB_IN)
        y = (jnp.dot(xr_s[:, cs].astype(BF16), cdr_ref[s], preferred_element_type=F32)
             - jnp.dot(xi_s[:, cs].astype(BF16), cdi_ref[s], preferred_element_type=F32))
        y_s[:, us] = jax.nn.gelu(y + d_ref[:, us] * u_ref[s])

    _emit_pipelined((b_projection, scan, c_projection), N_SLABS)

    xlr_ref[...] = cbr_s[...]
    xli_ref[...] = cbi_s[...]

    y = y_s[...]
    z = jnp.dot(y.astype(BF16), wg_ref[...], preferred_element_type=F32) + bg_ref[...]
    o = y * jax.nn.sigmoid(z)
    for s in range(N_SLABS):
        o_s[s] = o[:, s * SLAB_IN:(s + 1) * SLAB_IN]
    for n in range(half):
        for s in range(N_SLABS):
            o_ref[n, :, s * SLAB_IN:(s + 1) * SLAB_IN] = (
                o_s[s, pl.ds(n, tt, stride=half), :].astype(o_ref.dtype))


def _s5_seq(u_t, pairs, wb, cd, d, w_glu, b_glu, n_seq, seq_len, tt=128):
    assert 2 * n_seq == SUBLANES
    n_rows = n_seq * tt
    st = jax.ShapeDtypeStruct((SUBLANES, N_STATE), F32)
    y, xlr, xli = pl.pallas_call(
        _s5_seq_kernel,
        grid=(seq_len // tt,),
        in_specs=[pl.BlockSpec((N_SLABS, n_rows, SLAB_IN), lambda i: (0, i, 0))]
        + _s5_weight_specs(2 * SLAB_IN),
        out_specs=(pl.BlockSpec((n_seq, tt, S5_WIDTH), lambda i: (0, i, 0)),
                   pl.BlockSpec((SUBLANES, N_STATE), lambda i: (0, 0)),
                   pl.BlockSpec((SUBLANES, N_STATE), lambda i: (0, 0))),
        out_shape=(jax.ShapeDtypeStruct((n_seq, seq_len, S5_WIDTH), BF16), st, st),
        scratch_shapes=[pltpu.VMEM((n_rows, N_STATE), F32), pltpu.VMEM((n_rows, N_STATE), F32),
                        pltpu.VMEM((n_rows, S5_WIDTH), F32),
                        pltpu.VMEM((N_SLABS, n_rows, SLAB_IN), F32),
                        pltpu.VMEM((SUBLANES, N_STATE), F32), pltpu.VMEM((SUBLANES, N_STATE), F32),
                        pltpu.VMEM((N_SLABS, SUBLANES, SLAB_IN), F32)],
        compiler_params=_params("arbitrary"),
        name="s5_seq",
    )(u_t, pairs[0], pairs[1], wb[0], wb[1], cd[0], cd[1], d, w_glu, b_glu)
    return y.reshape(n_seq * seq_len, S5_WIDTH), xlr[n_seq:], xli[n_seq:]


_NT = (((1,), (1,)), ((), ()))
_TN = (((0,), (0,)), ((), ()))


def _norm_gate(o, g, gn_w):
    mu = jnp.mean(o, axis=-1, keepdims=True)
    oc = o - mu
    var = jnp.mean(oc * oc, axis=-1, keepdims=True)
    return jax.nn.silu(g) * (oc * lax.rsqrt(var + NORM_EPS) * gn_w)


def _ret_seq_kernel(q_ref, k_ref, v_ref, g_ref, mask_ref, qd_ref, kd_ref, cd_ref,
                    gnw_ref, o_ref, rn_ref, r_s, sc_s, kv_s, *, chunks):
    @pl.when(pl.program_id(1) == 0)
    def _():
        r_s[...] = jnp.zeros_like(r_s)

    for c in range(chunks):
        rows = slice(c * RET_TILE, (c + 1) * RET_TILE)
        for h in range(RET_HEADS):
            hs = slice(h * HEAD_DIM, (h + 1) * HEAD_DIM)
            qb, kb, vb = q_ref[rows, hs], k_ref[rows, hs], v_ref[rows, hs]
            sc = lax.dot_general(qb, kb, _NT, preferred_element_type=F32) * mask_ref[h]
            sc_s[c, h] = sc.astype(BF16)
            k_dec = (kb.astype(F32) * kd_ref[h]).astype(BF16)
            kv_s[c, h] = lax.dot_general(k_dec, vb, _TN, preferred_element_type=F32)

    for c in range(chunks):
        rows = slice(c * RET_TILE, (c + 1) * RET_TILE)
        for h in range(RET_HEADS):
            hs = slice(h * HEAD_DIM, (h + 1) * HEAD_DIM)
            qb, vb = q_ref[rows, hs], v_ref[rows, hs]
            r_old = r_s[h]
            q_dec = (qb.astype(F32) * qd_ref[h]).astype(BF16)
            lhs = jnp.concatenate([sc_s[c, h], q_dec], axis=1)
            rhs = jnp.concatenate([vb, r_old.astype(BF16)], axis=0)
            o = jnp.dot(lhs, rhs, preferred_element_type=F32)
            r_s[h] = r_old * cd_ref[h] + kv_s[c, h]
            o_ref[rows, hs] = _norm_gate(o, g_ref[rows, hs].astype(F32),
                                         gnw_ref[:, hs]).astype(o_ref.dtype)
    rn_ref[0] = r_s[...]


def _ret_step_kernel(q_ref, k_ref, v_ref, g_ref, mask_ref, qd_ref, kd_ref, cd_ref,
                     gnw_ref, r0_ref, o_ref, rn_ref, qr_s, kdt_s, ob_s, *, n_seq):
    rows_per_seq = RET_TILE // n_seq
    for h in range(RET_HEADS):
        hs = slice(h * HEAD_DIM, (h + 1) * HEAD_DIM)
        qb, kb = q_ref[:, hs], k_ref[:, hs]
        sc = lax.dot_general(qb, kb, _NT, preferred_element_type=F32) * mask_ref[h]
        ob_s[:, hs] = jnp.dot(sc.astype(BF16), v_ref[:, hs], preferred_element_type=F32)
        qr_s[:, hs] = qb.astype(F32) * qd_ref[h]
        kdt_s[h] = (kb.astype(F32) * kd_ref[h]).T

    lane = lax.broadcasted_iota(jnp.int32, (HEAD_DIM, RET_TILE), 1)
    for s in range(n_seq):
        rows = slice(s * rows_per_seq, (s + 1) * rows_per_seq)
        in_seq = (lane >= s * rows_per_seq) & (lane < (s + 1) * rows_per_seq)
        for h in range(RET_HEADS):
            hs = slice(h * HEAD_DIM, (h + 1) * HEAD_DIM)
            r_old = r0_ref[s, h]
            ob_s[rows, hs] += jnp.dot(qr_s[rows, hs].astype(BF16), r_old.astype(BF16),
                                      preferred_element_type=F32)
            kdt = jnp.where(in_seq, kdt_s[h], 0.0).astype(BF16)
            rn_ref[s, h] = r_old * cd_ref[h] + jnp.dot(kdt, v_ref[:, hs], preferred_element_type=F32)

    for h in range(RET_HEADS):
        hs = slice(h * HEAD_DIM, (h + 1) * HEAD_DIM)
        o_ref[:, hs] = _norm_gate(ob_s[:, hs], g_ref[:, hs].astype(F32),
                                  gnw_ref[:, hs]).astype(o_ref.dtype)


def _decay_tables(chunk, n_seq):
    scale = HEAD_DIM ** -0.5
    log_gamma = np.log(1.0 - 2.0 ** (-5.0 - np.arange(RET_HEADS, dtype=np.float64)))
    idx = np.arange(chunk, dtype=np.float64)
    diff = idx[:, None] - idx[None, :]
    mask = np.where(diff >= 0, np.exp(log_gamma[:, None, None] * np.maximum(diff, 0.0)), 0.0)
    q_decay = np.exp(log_gamma[:, None] * (idx + 1.0))
    k_decay = np.exp(log_gamma[:, None] * (chunk - 1.0 - idx)) * scale
    chunk_decay = np.exp(log_gamma * chunk)
    mask_t = scale * np.einsum("hab,st->hsatb", mask, np.eye(n_seq)).reshape(
        RET_HEADS, RET_TILE, RET_TILE)
    qd_t = np.broadcast_to(np.tile(q_decay, (1, n_seq))[:, :, None], (RET_HEADS, RET_TILE, HEAD_DIM))
    kd_t = np.broadcast_to(np.tile(k_decay, (1, n_seq))[:, :, None], (RET_HEADS, RET_TILE, HEAD_DIM))
    const = lambda a: jnp.asarray(np.ascontiguousarray(a), dtype=F32)
    return const(mask_t), const(qd_t), const(kd_t), const(chunk_decay)


def _retention(proj, r0, gn_w, n_seq, seq_len, chunks=8):
    t = proj.shape[0]
    const3 = lambda *_: (0, 0, 0)
    if r0 is None:
        tile, tile_seqs = chunks * RET_TILE, 1
        steps = seq_len // tile
        grid = (n_seq, steps)
        row = lambda n, c: n * steps + c
        state_map = lambda n, c: (n, 0, 0, 0)
        state_block = (1, RET_HEADS, HEAD_DIM, HEAD_DIM)
        sem = ("arbitrary", "arbitrary")
        body = functools.partial(_ret_seq_kernel, chunks=chunks)
        scratch = [pltpu.VMEM((RET_HEADS, HEAD_DIM, HEAD_DIM), F32),
                   pltpu.VMEM((chunks, RET_HEADS, RET_TILE, RET_TILE), BF16),
                   pltpu.VMEM((chunks, RET_HEADS, HEAD_DIM, HEAD_DIM), F32)]
        name = "ret_seq"
    else:
        tile, tile_seqs = RET_TILE, RET_TILE // seq_len
        grid = (t // tile,)
        row = lambda i: i
        state_map = lambda i: (i, 0, 0, 0)
        state_block = (tile_seqs, RET_HEADS, HEAD_DIM, HEAD_DIM)
        sem = ("arbitrary",)
        body = functools.partial(_ret_step_kernel, n_seq=tile_seqs)
        scratch = [pltpu.VMEM((RET_TILE, RET_WIDTH), F32),
                   pltpu.VMEM((RET_HEADS, HEAD_DIM, RET_TILE), F32),
                   pltpu.VMEM((RET_TILE, RET_WIDTH), F32)]
        name = "ret_step"
    mask_t, qd_t, kd_t, cd = _decay_tables(RET_TILE // tile_seqs, tile_seqs)

    def col(cb):
        return pl.BlockSpec((tile, RET_WIDTH), lambda *a: (row(*a), cb))

    in_specs = [
        col(1), col(2), col(3), col(4),
        pl.BlockSpec((RET_HEADS, RET_TILE, RET_TILE), const3),
        pl.BlockSpec((RET_HEADS, RET_TILE, HEAD_DIM), const3),
        pl.BlockSpec((RET_HEADS, RET_TILE, HEAD_DIM), const3),
        pl.BlockSpec(memory_space=pltpu.SMEM),
        pl.BlockSpec((1, RET_WIDTH), lambda *_: (0, 0)),
    ]
    args = [proj, proj, proj, proj, mask_t, qd_t, kd_t, cd, gn_w]
    if r0 is not None:
        in_specs.append(pl.BlockSpec(state_block, state_map))
        args.append(r0)
    return pl.pallas_call(
        body,
        grid=grid,
        in_specs=in_specs,
        out_specs=(pl.BlockSpec((tile, RET_WIDTH), lambda *a: (row(*a), 0)),
                   pl.BlockSpec(state_block, state_map)),
        out_shape=(jax.ShapeDtypeStruct((t, RET_WIDTH), BF16),
                   jax.ShapeDtypeStruct((n_seq, RET_HEADS, HEAD_DIM, HEAD_DIM), F32)),
        scratch_shapes=scratch,
        compiler_params=_params(*sem),
        name=name,
    )(*args)


def _outproj_kernel(x_ref, a_ref, b_ref, wa_ref, wb_ref, o_ref, *w_bf16, cast):
    if cast:
        (wo_ref,) = w_bf16

        @pl.when(pl.program_id(0) == 0)
        def _():
            wo_ref[:S5_WIDTH, :] = wa_ref[...].astype(BF16)
            wo_ref[S5_WIDTH:, :] = wb_ref[...].astype(BF16)

        wa, wb = wo_ref[:S5_WIDTH, :], wo_ref[S5_WIDTH:, :]
    else:
        wa, wb = wa_ref[...], wb_ref[...]
    o_ref[...] = (x_ref[...]
                  + jnp.dot(a_ref[...], wa, preferred_element_type=F32)
                  + jnp.dot(b_ref[...], wb, preferred_element_type=F32))


def _outproj(x2d, a, b, w, tm=512):
    t = x2d.shape[0]
    cast = w.dtype == F32
    resident = pl.Buffered(1)
    out_specs = [pl.BlockSpec((tm, D_MODEL), lambda i: (i, 0))]
    out_shape = [jax.ShapeDtypeStruct((t, D_MODEL), F32)]
    if cast:
        out_specs.append(pl.BlockSpec((S5_WIDTH + RET_WIDTH, D_MODEL), lambda i: (0, 0)))
        out_shape.append(jax.ShapeDtypeStruct((S5_WIDTH + RET_WIDTH, D_MODEL), BF16))
    return pl.pallas_call(
        functools.partial(_outproj_kernel, cast=cast),
        grid=(t // tm,),
        in_specs=[
            pl.BlockSpec((tm, D_MODEL), lambda i: (i, 0)),
            pl.BlockSpec((tm, S5_WIDTH), lambda i: (i, 0)),
            pl.BlockSpec((tm, RET_WIDTH), lambda i: (i, 0)),
            pl.BlockSpec((S5_WIDTH, D_MODEL), lambda i: (0, 0), pipeline_mode=resident),
            pl.BlockSpec((RET_WIDTH, D_MODEL), lambda i: (1, 0), pipeline_mode=resident),
        ],
        out_specs=out_specs,
        out_shape=out_shape,
        compiler_params=_params("arbitrary"),
        name="out_proj_cast" if cast else "out_proj",
    )(x2d, a, b, w, w)


def _ffn_kernel(x_ref, gn_ref, wg_ref, wu_ref, wo_ref, gf_ref, o_ref, *rest, cast):
    if cast:
        wgb_ref, wub_ref, wob_ref, h_scr = rest
        wgb_ref[...] = wg_ref[...].astype(BF16)
        wub_ref[...] = wu_ref[...].astype(BF16)
        wob_ref[...] = wo_ref[...].astype(BF16)
        wg_ref, wu_ref, wo_ref = wgb_ref, wub_ref, wob_ref
    else:
        (h_scr,) = rest
    j = pl.program_id(1)
    last_j = pl.num_programs(1) - 1

    def step(first, last):
        for r in range(x_ref.shape[0] // FFN_ROWS):
            rows = slice(r * FFN_ROWS, (r + 1) * FFN_ROWS)
            if first:
                h_scr[rows, :] = _rms(x_ref[rows, :], gn_ref[...]).astype(BF16)
            h = h_scr[rows, :]
            gate = jnp.dot(h, wg_ref[...], preferred_element_type=F32)
            up = jnp.dot(h, wu_ref[...], preferred_element_type=F32)
            act = (jax.nn.silu(gate) * up).astype(BF16)
            acc = (x_ref if first else o_ref)[rows, :] + jnp.dot(act, wo_ref[...],
                                                                  preferred_element_type=F32)
            o_ref[rows, :] = _rms(acc, gf_ref[...]) if last else acc

    pl.when(j == 0)(lambda: step(True, False))
    pl.when((j > 0) & (j < last_j))(lambda: step(False, False))
    pl.when(j == last_j)(lambda: step(False, True))


def _ffn(x2d, g_ffn, w_gate, w_up, w_down, g_final, tm=1024, tf=512):
    t = x2d.shape[0]
    nf = D_FF // tf
    cast = w_down.dtype == F32
    (wg, g0), (wu, u0) = w_gate, w_up
    g0, u0 = g0 * nf, u0 * nf
    x_mode = {}
    out_specs = [pl.BlockSpec((tm, D_MODEL), lambda i, j: (i, 0))]
    out_shape = [jax.ShapeDtypeStruct((t, D_MODEL), F32)]
    if cast:
        assert t == tm, "the bf16 weight outputs are written once per row tile"
        x_mode = dict(pipeline_mode=pl.Buffered(1))
        out_specs += [pl.BlockSpec((D_MODEL, tf), lambda i, j: (0, j)),
                      pl.BlockSpec((D_MODEL, tf), lambda i, j: (0, j)),
                      pl.BlockSpec((tf, D_MODEL), lambda i, j: (j, 0))]
        out_shape += [jax.ShapeDtypeStruct((D_MODEL, D_FF), BF16),
                      jax.ShapeDtypeStruct((D_MODEL, D_FF), BF16),
                      jax.ShapeDtypeStruct((D_FF, D_MODEL), BF16)]
    return pl.pallas_call(
        functools.partial(_ffn_kernel, cast=cast),
        grid=(t // tm, nf),
        in_specs=[
            pl.BlockSpec((tm, D_MODEL), lambda i, j: (i, 0), **x_mode),
            pl.BlockSpec((1, D_MODEL), lambda i, j: (0, 0)),
            pl.BlockSpec((D_MODEL, tf), lambda i, j: (0, j + g0)),
            pl.BlockSpec((D_MODEL, tf), lambda i, j: (0, j + u0)),
            pl.BlockSpec((tf, D_MODEL), lambda i, j: (j, 0)),
            pl.BlockSpec((1, D_MODEL), lambda i, j: (0, 0)),
        ],
        out_specs=out_specs,
        out_shape=out_shape,
        scratch_shapes=[pltpu.VMEM((tm, D_MODEL), BF16)],
        compiler_params=_params("arbitrary", "arbitrary"),
        name="ffn_cast" if cast else "ffn",
    )(x2d, g_ffn, wg, wu, w_down, g_final)


def _finish(x2d, proj, s5_out, ret_state, w, w_out, ffn_w, ffn_tf, n, l):
    ret_out, ret_new = _retention(proj, ret_state, w["gn_w"], n, l)
    x1, *w_out_b = _outproj(x2d, s5_out, ret_out, w_out)
    y, *ffn_b = _ffn(x1, w["norm_ffn"], *ffn_w, w["norm_final"], tf=ffn_tf)
    return y.reshape(n, l, D_MODEL), ret_new, w_out_b, ffn_b


def _as_groups(re, im, n):
    return re.reshape(n, S5_GROUPS, S5_STATE), im.reshape(n, S5_GROUPS, S5_STATE)


def _sample_layer(x, s5_state, ret_state, w, big):
    n, l, _ = x.shape
    x2d = x.reshape(n * l, D_MODEL)
    proj, w_in_b, u_t = _inproj_cast(x2d, w["norm_mix"], big["w_in"], l, float(PAST_LEN))
    s5_out, s5_re, s5_im, w_glu_b = _s5_step(u_t, s5_state, w["pairs"], w["wb"], w["cd"], w["d"],
                                             big["w_glu"], w["b_glu"])
    ffn_f32 = ((big["w_ffn_in"], 0), (big["w_ffn_in"], 1), big["w_ffn_out"])
    y, ret_new, (w_out_b,), (wg_b, wu_b, wd_b) = _finish(
        x2d, proj, s5_out, ret_state, w, big["w_out"], ffn_f32, 256, n, l)
    big_b = dict(w_in=w_in_b, w_glu=w_glu_b, w_out=w_out_b, ffn=((wg_b, 0), (wu_b, 0), wd_b))
    return (y, *_as_groups(s5_re, s5_im, n), ret_new), big_b


def _prompt_layer(x, w, big_b):
    n, l, _ = x.shape
    x2d = x.reshape(n * l, D_MODEL)
    proj, u_t = _inproj_seq(x2d, w["norm_mix"], big_b["w_in"], n, 0.0)
    s5_out, s5_re, s5_im = _s5_seq(u_t, w["pairs"], w["wb"], w["cd"], w["d"], big_b["w_glu"],
                                   w["b_glu"], n, l)
    y, ret_new, _, _ = _finish(x2d, proj, s5_out, None, w, big_b["w_out"], big_b["ffn"],
                               512, n, l)
    return (y, *_as_groups(s5_re, s5_im, n), ret_new)


def kernel(x_prompt, x_sample, state_s5_re, state_s5_im, state_ret, norm_mix, w_in, s5_lambda_re, s5_lambda_im, s5_log_step, s5_b_re, s5_b_im, s5_c_re, s5_c_im, s5_d, s5_w_glu, s5_b_glu, ret_gn_w, w_out, norm_ffn, w_ffn_in, w_ffn_out, norm_final):
    assert norm_mix.shape[0] == 1, "single-layer stack"
    pairs, wb, cd = _s5_params(s5_lambda_re[0], s5_lambda_im[0], s5_log_step[0],
                               s5_b_re[0], s5_b_im[0], s5_c_re[0], s5_c_im[0])
    w = dict(norm_mix=norm_mix, pairs=pairs, wb=wb, cd=cd, d=s5_d, b_glu=s5_b_glu,
             gn_w=ret_gn_w, norm_ffn=norm_ffn, norm_final=norm_final.reshape(1, D_MODEL))
    big = dict(w_in=w_in[0], w_glu=s5_w_glu[0], w_out=w_out[0],
               w_ffn_in=w_ffn_in[0], w_ffn_out=w_ffn_out[0])
    n_s = x_sample.shape[0]
    (ys, s_re, s_im, s_ret), big_b = _sample_layer(
        x_sample,
        (state_s5_re[0].reshape(n_s, N_STATE), state_s5_im[0].reshape(n_s, N_STATE)),
        state_ret[0], w, big)
    yp, p_re, p_im, p_ret = _prompt_layer(x_prompt, w, big_b)
    return (yp, ys, p_re[None], p_im[None], p_ret[None], s_re[None], s_im[None], s_ret[None])
```

```python
import functools
import math

import jax
import jax.numpy as jnp
import numpy as np
from jax import lax
from jax.experimental import pallas as pl
from jax.experimental.pallas import tpu as pltpu

F32 = jnp.float32
BF16 = jnp.bfloat16

D_MODEL = 2048
S5_WIDTH = 1024
S5_GROUP = 16
S5_GROUPS = 64
S5_STATE = 64
N_STATE = S5_GROUPS * S5_STATE
RET_WIDTH = 1024
RET_HEADS = 8
HEAD_DIM = 128
ROPE_BASE = 10000.0
D_FF = 5632
IN_WIDTH = S5_WIDTH + 4 * RET_WIDTH
NORM_EPS = 1e-6
PAST_LEN = 16384

SUBLANES = 8
SLAB_GROUPS = 8
N_SLABS = S5_GROUPS // SLAB_GROUPS
SLAB_IN = SLAB_GROUPS * S5_GROUP
SLAB_STATE = SLAB_GROUPS * S5_STATE
RET_TILE = 128
ROTARY_TILES = (1, 2)
FFN_ROWS = 512

VMEM_LIMIT_BYTES = 60 * 1024 * 1024


def _params(*sem):
    return pltpu.CompilerParams(dimension_semantics=sem, vmem_limit_bytes=VMEM_LIMIT_BYTES)


def _rms(xf, g):
    ms = jnp.mean(xf * xf, axis=-1, keepdims=True)
    return xf * lax.rsqrt(ms + NORM_EPS) * g


def _rotary(x, cos2, sin2):
    return x * cos2 + pltpu.roll(x, HEAD_DIM // 2, 1) * sin2


def _rotary_heads(x, cos2, sin2):
    return jnp.concatenate(
        [_rotary(x[:, h * HEAD_DIM:(h + 1) * HEAD_DIM], cos2, sin2) for h in range(RET_HEADS)],
        axis=1)


def _rotary_tables(pos0, rows, reps):
    half = HEAD_DIM // 2
    inv_freq = ROPE_BASE ** (-np.arange(half, dtype=np.float64) / half)
    pos = pos0 + np.arange(rows, dtype=np.float64)
    ang = pos[:, None] * inv_freq[None, :]
    cos, sin = np.cos(ang), np.sin(ang)
    cos2 = np.concatenate([cos, cos], axis=-1)
    sin2 = np.concatenate([-sin, sin], axis=-1)
    return (jnp.asarray(np.tile(cos2, (reps, 1)), dtype=F32),
            jnp.asarray(np.tile(sin2, (reps, 1)), dtype=F32))


def _inproj_seq_kernel(x_ref, g_ref, cos_ref, sin_ref, w_ref, o_ref, ou_ref, h_scr, *, tn, n_seq):
    seq = pl.program_id(1)
    tm = x_ref.shape[0]
    h_scr[...] = _rms(x_ref[...], g_ref[...]).astype(BF16)
    for j in range(IN_WIDTH // tn):
        cols = slice(j * tn, (j + 1) * tn)
        res = jnp.dot(h_scr[...], w_ref[:, cols], preferred_element_type=F32)
        if j == 0:
            dst = pl.ds(seq, tm, stride=n_seq)
            for s in range(N_SLABS):
                ou_ref[s, dst, :] = res[:, s * SLAB_IN:(s + 1) * SLAB_IN]
            continue
        if j in ROTARY_TILES:
            res = _rotary_heads(res, cos_ref[...], sin_ref[...])
        o_ref[j - 1] = res.astype(o_ref.dtype)


def _inproj_seq(x2d, g, w, n_seq, pos0, tm=256, tn=RET_WIDTH):
    assert tn == S5_WIDTH == RET_WIDTH
    t = x2d.shape[0]
    tiles = t // n_seq // tm
    row = lambda i, n: (n * tiles + i, 0)
    const = lambda i, n: (0, 0)
    cos2, sin2 = _rotary_tables(pos0, t // n_seq, 1)
    return pl.pallas_call(
        functools.partial(_inproj_seq_kernel, tn=tn, n_seq=n_seq),
        grid=(tiles, n_seq),
        in_specs=[
            pl.BlockSpec((tm, D_MODEL), row),
            pl.BlockSpec((1, D_MODEL), const),
            pl.BlockSpec((tm, HEAD_DIM), lambda i, n: (i, 0)),
            pl.BlockSpec((tm, HEAD_DIM), lambda i, n: (i, 0)),
            pl.BlockSpec((D_MODEL, IN_WIDTH), const, pipeline_mode=pl.Buffered(1)),
        ],
        out_specs=[pl.BlockSpec((4, tm, RET_WIDTH), lambda i, n: (0, n * tiles + i, 0)),
                   pl.BlockSpec((N_SLABS, n_seq * tm, SLAB_IN), lambda i, n: (0, i, 0))],
        out_shape=[jax.ShapeDtypeStruct((4, t, RET_WIDTH), BF16),
                   jax.ShapeDtypeStruct((N_SLABS, t, SLAB_IN), F32)],
        scratch_shapes=[pltpu.VMEM((tm, D_MODEL), BF16)],
        compiler_params=_params("arbitrary", "arbitrary"),
        name="in_proj_seq",
    )(x2d, g, cos2, sin2, w)


def _inproj_cast_kernel(x_ref, g_ref, cos_ref, sin_ref, w_ref, o_ref, wb_ref, ou_ref, h_scr, ru_s):
    j = pl.program_id(0)

    @pl.when(j == 0)
    def _():
        h_scr[...] = _rms(x_ref[...], g_ref[...]).astype(BF16)

    wb_ref[...] = w_ref[...].astype(BF16)
    res = jnp.dot(h_scr[...], wb_ref[...], preferred_element_type=F32)
    is_rotary = functools.reduce(jnp.logical_or, [j == r for r in ROTARY_TILES])

    @pl.when(is_rotary)
    def _():
        o_ref[...] = _rotary_heads(res, cos_ref[...], sin_ref[...]).astype(o_ref.dtype)

    @pl.when(jnp.logical_not(is_rotary))
    def _():
        o_ref[...] = res.astype(o_ref.dtype)

    @pl.when(j == 0)
    def _():
        seq_len, n_seq = ou_ref.shape[1], ou_ref.shape[2]
        for s in range(N_SLABS):
            ru_s[s] = res[:, s * SLAB_IN:(s + 1) * SLAB_IN]
        for s in range(N_SLABS):
            for t in range(seq_len):
                ou_ref[s, t] = ru_s[s, pl.ds(t, n_seq, stride=seq_len), :]


def _inproj_cast(x2d, g, w_f32, seq_len, pos0):
    t = x2d.shape[0]
    tn = RET_WIDTH
    cos2, sin2 = _rotary_tables(pos0, seq_len, t // seq_len)
    return pl.pallas_call(
        _inproj_cast_kernel,
        grid=(IN_WIDTH // tn,),
        in_specs=[
            pl.BlockSpec((t, D_MODEL), lambda j: (0, 0), pipeline_mode=pl.Buffered(1)),
            pl.BlockSpec((1, D_MODEL), lambda j: (0, 0)),
            pl.BlockSpec((t, HEAD_DIM), lambda j: (0, 0)),
            pl.BlockSpec((t, HEAD_DIM), lambda j: (0, 0)),
            pl.BlockSpec((D_MODEL, tn), lambda j: (0, j)),
        ],
        out_specs=[pl.BlockSpec((None, t, tn), lambda j: (j, 0, 0)),
                   pl.BlockSpec((D_MODEL, tn), lambda j: (0, j)),
                   pl.BlockSpec((N_SLABS, seq_len, t // seq_len, SLAB_IN), lambda j: (0, 0, 0, 0))],
        out_shape=[jax.ShapeDtypeStruct((IN_WIDTH // tn, t, tn), BF16),
                   jax.ShapeDtypeStruct((D_MODEL, IN_WIDTH), BF16),
                   jax.ShapeDtypeStruct((N_SLABS, seq_len, t // seq_len, SLAB_IN), F32)],
        scratch_shapes=[pltpu.VMEM((t, D_MODEL), BF16), pltpu.VMEM((N_SLABS, t, SLAB_IN), F32)],
        compiler_params=_params("arbitrary"),
        name="in_proj_cast",
    )(x2d, g, cos2, sin2, w_f32)


def _s5_param_kernel(lr_ref, li_ref, ls_ref, br_ref, bi_ref, ctr_ref, cti_ref,
                     pairr_ref, pairi_ref, wbr_ref, wbi_ref, cdr_ref, cdi_ref):
    lr = lr_ref[...]
    li = li_ref[...]
    dt = jnp.exp(ls_ref[...])
    mag = jnp.exp(lr * dt)
    ar = mag * jnp.cos(li * dt)
    ai = mag * jnp.sin(li * dt)
    den = lr * lr + li * li
    nr = ar - 1.0
    cr = (nr * lr + ai * li) / den
    ci = (ai * lr - nr * li) / den
    br = br_ref[...]
    bi = bi_ref[...]
    bbr = cr * br - ci * bi
    bbi = cr * bi + ci * br
    abr = ar * bbr - ai * bbi
    abi = ar * bbi + ai * bbr

    in_shape = (SLAB_IN, SLAB_STATE)
    same_in = (lax.broadcasted_iota(jnp.int32, in_shape, 0) // S5_GROUP
               == lax.broadcasted_iota(jnp.int32, in_shape, 1) // S5_STATE)
    out_shape = (SLAB_STATE, SLAB_IN)
    same_out = (lax.broadcasted_iota(jnp.int32, out_shape, 0) // S5_STATE
                == lax.broadcasted_iota(jnp.int32, out_shape, 1) // S5_GROUP)
    for s in range(N_SLABS):
        cs = slice(s * SLAB_STATE, (s + 1) * SLAB_STATE)
        for dst, top, bot in ((wbr_ref, bbr, abr), (wbi_ref, bbi, abi)):
            for k, part in enumerate((top, bot)):
                blk = jnp.tile(part[:, cs], (SLAB_GROUPS, 1))
                dst[s, k * SLAB_IN:(k + 1) * SLAB_IN, :] = jnp.where(same_in, blk, 0.0).astype(BF16)
        cdr_ref[s] = jnp.where(same_out, ctr_ref[cs, :], 0.0).astype(BF16)
        cdi_ref[s] = jnp.where(same_out, cti_ref[cs, :], 0.0).astype(BF16)

    rows = lax.broadcasted_iota(jnp.int32, (SUBLANES, N_STATE), 0)
    second = rows >= SUBLANES // 2
    full = lambda v: jnp.broadcast_to(v, (SUBLANES, N_STATE))
    pairr_ref[...] = jnp.where(second, full(ar * ar - ai * ai), full(ar))
    pairi_ref[...] = jnp.where(second, full(ar * ai + ai * ar), full(ai))


def _s5_params(lam_re, lam_im, log_step, b_re, b_im, c_re, c_im):
    lr = lam_re.reshape(1, N_STATE)
    li = lam_im.reshape(1, N_STATE)
    ls = jnp.repeat(log_step, S5_STATE).reshape(1, N_STATE)
    b_rows = lambda b: jnp.transpose(b, (2, 0, 1)).reshape(S5_GROUP, N_STATE)
    c_cols = lambda c: jnp.tile(jnp.transpose(c, (0, 2, 1)).reshape(N_STATE, S5_GROUP),
                                (1, SLAB_GROUPS))
    pair = jax.ShapeDtypeStruct((SUBLANES, N_STATE), F32)
    wb = jax.ShapeDtypeStruct((N_SLABS, 2 * SLAB_IN, SLAB_STATE), BF16)
    cd = jax.ShapeDtypeStruct((N_SLABS, SLAB_STATE, SLAB_IN), BF16)
    pairr, pairi, wbr, wbi, cdr, cdi = pl.pallas_call(
        _s5_param_kernel,
        out_shape=(pair, pair, wb, wb, cd, cd),
        name="s5_params",
    )(lr, li, ls, b_rows(b_re), b_rows(b_im), c_cols(c_re), c_cols(c_im))
    return (pairr, pairi), (wbr, wbi), (cdr, cdi)


def _emit_pipelined(stages, n):
    for step in range(n + len(stages) - 1):
        for lag, stage in enumerate(stages):
            if 0 <= step - lag < n:
                stage(step - lag)


def _cmul_add(xr, xi, pr, pi, vr, vi):
    return xr + (pr * vr - pi * vi), xi + (pr * vi + pi * vr)


def _s5_weight_specs(k_in):
    const3 = lambda *_: (0, 0, 0)
    const2 = lambda *_: (0, 0)
    return [
        pl.BlockSpec((SUBLANES, N_STATE), const2),
        pl.BlockSpec((SUBLANES, N_STATE), const2),
        pl.BlockSpec((N_SLABS, k_in, SLAB_STATE), const3),
        pl.BlockSpec((N_SLABS, k_in, SLAB_STATE), const3),
        pl.BlockSpec((N_SLABS, SLAB_STATE, SLAB_IN), const3),
        pl.BlockSpec((N_SLABS, SLAB_STATE, SLAB_IN), const3),
        pl.BlockSpec((1, S5_WIDTH), const2),
        pl.BlockSpec((S5_WIDTH, S5_WIDTH), const2),
        pl.BlockSpec((1, S5_WIDTH), const2),
    ]


def _s5_step_kernel(u_ref, x0r_ref, x0i_ref, pairr_ref, pairi_ref, wbr_ref, wbi_ref, cdr_ref, cdi_ref,
                    d_ref, wg_ref, bg_ref, o_ref, xlr_ref, xli_ref, wgb_ref, xr_s, xi_s, y_s, o_s):
    seq_len, n_seq = u_ref.shape[1], u_ref.shape[2]

    @pl.when(pl.program_id(0) == 0)
    def _():
        wgb_ref[...] = wg_ref[...].astype(BF16)

    def b_projection(s):
        cs = slice(s * SLAB_STATE, (s + 1) * SLAB_STATE)
        ub = u_ref[s].reshape(seq_len * n_seq, SLAB_IN).astype(BF16)
        xr_s[:, cs] = jnp.dot(ub, wbr_ref[s], preferred_element_type=F32)
        xi_s[:, cs] = jnp.dot(ub, wbi_ref[s], preferred_element_type=F32)

    def scan(s):
        cs = slice(s * SLAB_STATE, (s + 1) * SLAB_STATE)
        a_r = jnp.broadcast_to(pairr_ref[0:1, cs], (SUBLANES, SLAB_STATE))
        a_i = jnp.broadcast_to(pairi_ref[0:1, cs], (SUBLANES, SLAB_STATE))
        for b in range(n_seq // SUBLANES):
            seqs = slice(b * SUBLANES, (b + 1) * SUBLANES)
            xr, xi = x0r_ref[seqs, cs], x0i_ref[seqs, cs]
            for t in range(seq_len):
                rows = slice(t * n_seq + b * SUBLANES, t * n_seq + (b + 1) * SUBLANES)
                xr, xi = _cmul_add(xr_s[rows, cs], xi_s[rows, cs], a_r, a_i, xr, xi)
                xr_s[rows, cs] = xr
                xi_s[rows, cs] = xi
            xlr_ref[seqs, cs] = xr
            xli_ref[seqs, cs] = xi

    def c_projection(s):
        cs = slice(s * SLAB_STATE, (s + 1) * SLAB_STATE)
        us = slice(s * SLAB_IN, (s + 1) * SLAB_IN)
        u = u_ref[s].reshape(seq_len * n_seq, SLAB_IN)
        y = (jnp.dot(xr_s[:, cs].astype(BF16), cdr_ref[s], preferred_element_type=F32)
             - jnp.dot(xi_s[:, cs].astype(BF16), cdi_ref[s], preferred_element_type=F32))
        y_s[:, us] = jax.nn.gelu(y + d_ref[:, us] * u)

    _emit_pipelined((b_projection, scan, c_projection), N_SLABS)

    y = y_s[...]
    z = jnp.dot(y.astype(BF16), wgb_ref[...], preferred_element_type=F32) + bg_ref[...]
    o = y * jax.nn.sigmoid(z)
    for s in range(N_SLABS):
        for t in range(seq_len):
            o_s[s, pl.ds(t, n_seq, stride=seq_len), :] = (
                o[t * n_seq:(t + 1) * n_seq, s * SLAB_IN:(s + 1) * SLAB_IN])
    for s in range(N_SLABS):
        o_ref[:, s * SLAB_IN:(s + 1) * SLAB_IN] = o_s[s].astype(o_ref.dtype)


def _s5_step(u_t, x0, pairs, wb, cd, d, w_glu, b_glu, n_tile=32):
    _, seq_len, n_seq, _ = u_t.shape
    rows = n_tile * seq_len
    row = lambda i: (i, 0)
    st = jax.ShapeDtypeStruct((n_seq, N_STATE), F32)
    return pl.pallas_call(
        _s5_step_kernel,
        grid=(n_seq // n_tile,),
        in_specs=[pl.BlockSpec((N_SLABS, seq_len, n_tile, SLAB_IN), lambda i: (0, 0, i, 0)),
                  pl.BlockSpec((n_tile, N_STATE), row), pl.BlockSpec((n_tile, N_STATE), row)]
        + _s5_weight_specs(SLAB_IN),
        out_specs=(pl.BlockSpec((rows, S5_WIDTH), row), pl.BlockSpec((n_tile, N_STATE), row),
                   pl.BlockSpec((n_tile, N_STATE), row),
                   pl.BlockSpec((S5_WIDTH, S5_WIDTH), lambda i: (0, 0))),
        out_shape=(jax.ShapeDtypeStruct((n_seq * seq_len, S5_WIDTH), BF16), st, st,
                   jax.ShapeDtypeStruct((S5_WIDTH, S5_WIDTH), BF16)),
        scratch_shapes=[pltpu.VMEM((rows, N_STATE), F32), pltpu.VMEM((rows, N_STATE), F32),
                        pltpu.VMEM((rows, S5_WIDTH), F32),
                        pltpu.VMEM((N_SLABS, rows, SLAB_IN), F32)],
        compiler_params=_params("arbitrary"),
        name="s5_step",
    )(u_t, x0[0], x0[1], pairs[0], pairs[1], wb[0], wb[1], cd[0], cd[1], d, w_glu, b_glu)


def _s5_seq_kernel(u_ref, pairr_ref, pairi_ref, wbr_ref, wbi_ref, cdr_ref, cdi_ref, d_ref, wg_ref,
                   bg_ref, o_ref, xlr_ref, xli_ref, xr_s, xi_s, y_s, o_s, cbr_s, cbi_s, ul_s):
    n_rows = u_ref.shape[1]
    half = SUBLANES // 2
    tt = n_rows // half

    @pl.when(pl.program_id(0) == 0)
    def _():
        cbr_s[...] = jnp.zeros_like(cbr_s)
        cbi_s[...] = jnp.zeros_like(cbi_s)
        ul_s[...] = jnp.zeros_like(ul_s)

    first = lax.broadcasted_iota(jnp.int32, (SUBLANES, SLAB_IN), 0) < half

    def b_projection(s):
        cs = slice(s * SLAB_STATE, (s + 1) * SLAB_STATE)
        u = u_ref[s]
        shifted = pltpu.roll(u, half, 0)
        head = jnp.where(first, pltpu.roll(ul_s[s], half, 0), shifted[:SUBLANES])
        u_prev = jnp.concatenate([head, shifted[SUBLANES:]], axis=0)
        ul_s[s] = u[n_rows - SUBLANES:]
        ub = jnp.concatenate([u, u_prev], axis=1).astype(BF16)
        xr_s[:, cs] = jnp.dot(ub, wbr_ref[s], preferred_element_type=F32)
        xi_s[:, cs] = jnp.dot(ub, wbi_ref[s], preferred_element_type=F32)

    def scan(s):
        cs = slice(s * SLAB_STATE, (s + 1) * SLAB_STATE)
        pr = jnp.broadcast_to(pairr_ref[half:half + 1, cs], (SUBLANES, SLAB_STATE))
        pi = jnp.broadcast_to(pairi_ref[half:half + 1, cs], (SUBLANES, SLAB_STATE))
        cr, ci = cbr_s[:, cs], cbi_s[:, cs]
        for b in range(n_rows // SUBLANES):
            rows = slice(b * SUBLANES, (b + 1) * SUBLANES)
            cr, ci = _cmul_add(xr_s[rows, cs], xi_s[rows, cs], pr, pi, cr, ci)
            xr_s[rows, cs] = cr
            xi_s[rows, cs] = ci
        cbr_s[:, cs] = cr
        cbi_s[:, cs] = ci

    def c_projection(s):
        cs = slice(s * SLAB_STATE, (s + 1) * SLAB_STATE)
        us = slice(s * SLAB_IN, (s + 1) * SLAB_IN)
        y = (jnp.dot(xr_s[:, cs].astype(BF16), cdr_ref[s], preferred_element_type=F32)
             - jnp.dot(xi_s[:, cs].astype(BF16), cdi_ref[s], preferred_element_type=F32))
        y_s[:, us] = jax.nn.gelu(y + d_ref[:, us] * u_ref[s])

    _emit_pipelined((b_projection, scan, c_projection), N_SLABS)

    xlr_ref[...] = cbr_s[...]
    xli_ref[...] = cbi_s[...]

    y = y_s[...]
    z = jnp.dot(y.astype(BF16), wg_ref[...], preferred_element_type=F32) + bg_ref[...]
    o = y * jax.nn.sigmoid(z)
    for s in range(N_SLABS):
        o_s[s] = o[:, s * SLAB_IN:(s + 1) * SLAB_IN]
    for n in range(half):
        for s in range(N_SLABS):
            o_ref[n, :, s * SLAB_IN:(s + 1) * SLAB_IN] = (
                o_s[s, pl.ds(n, tt, stride=half), :].astype(o_ref.dtype))


def _s5_seq(u_t, pairs, wb, cd, d, w_glu, b_glu, n_seq, seq_len, tt=128):
    assert 2 * n_seq == SUBLANES
    n_rows = n_seq * tt
    st = jax.ShapeDtypeStruct((SUBLANES, N_STATE), F32)
    y, xlr, xli = pl.pallas_call(
        _s5_seq_kernel,
        grid=(seq_len // tt,),
        in_specs=[pl.BlockSpec((N_SLABS, n_rows, SLAB_IN), lambda i: (0, i, 0))]
        + _s5_weight_specs(2 * SLAB_IN),
        out_specs=(pl.BlockSpec((n_seq, tt, S5_WIDTH), lambda i: (0, i, 0)),
                   pl.BlockSpec((SUBLANES, N_STATE), lambda i: (0, 0)),
                   pl.BlockSpec((SUBLANES, N_STATE), lambda i: (0, 0))),
        out_shape=(jax.ShapeDtypeStruct((n_seq, seq_len, S5_WIDTH), BF16), st, st),
        scratch_shapes=[pltpu.VMEM((n_rows, N_STATE), F32), pltpu.VMEM((n_rows, N_STATE), F32),
                        pltpu.VMEM((n_rows, S5_WIDTH), F32),
                        pltpu.VMEM((N_SLABS, n_rows, SLAB_IN), F32),
                        pltpu.VMEM((SUBLANES, N_STATE), F32), pltpu.VMEM((SUBLANES, N_STATE), F32),
                        pltpu.VMEM((N_SLABS, SUBLANES, SLAB_IN), F32)],
        compiler_params=_params("arbitrary"),
        name="s5_seq",
    )(u_t, pairs[0], pairs[1], wb[0], wb[1], cd[0], cd[1], d, w_glu, b_glu)
    return y.reshape(n_seq * seq_len, S5_WIDTH), xlr[n_seq:], xli[n_seq:]


_NT = (((1,), (1,)), ((), ()))
_TN = (((0,), (0,)), ((), ()))


def _norm_gate(o, g, gn_w):
    mu = jnp.mean(o, axis=-1, keepdims=True)
    oc = o - mu
    var = jnp.mean(oc * oc, axis=-1, keepdims=True)
    return jax.nn.silu(g) * (oc * lax.rsqrt(var + NORM_EPS) * gn_w)


def _ret_seq_kernel(q_ref, k_ref, v_ref, g_ref, mask_ref, qd_ref, kd_ref, cd_ref,
                    gnw_ref, o_ref, rn_ref, r_s, sc_s, kv_s, *, chunks):
    @pl.when(pl.program_id(1) == 0)
    def _():
        r_s[...] = jnp.zeros_like(r_s)

    for c in range(chunks):
        rows = slice(c * RET_TILE, (c + 1) * RET_TILE)
        for h in range(RET_HEADS):
            hs = slice(h * HEAD_DIM, (h + 1) * HEAD_DIM)
            qb, kb, vb = q_ref[rows, hs], k_ref[rows, hs], v_ref[rows, hs]
            sc = lax.dot_general(qb, kb, _NT, preferred_element_type=F32) * mask_ref[h]
            sc_s[c, h] = sc.astype(BF16)
            k_dec = (kb.astype(F32) * kd_ref[h]).astype(BF16)
            kv_s[c, h] = lax.dot_general(k_dec, vb, _TN, preferred_element_type=F32)

    for c in range(chunks):
        rows = slice(c * RET_TILE, (c + 1) * RET_TILE)
        for h in range(RET_HEADS):
            hs = slice(h * HEAD_DIM, (h + 1) * HEAD_DIM)
            qb, vb = q_ref[rows, hs], v_ref[rows, hs]
            r_old = r_s[h]
            q_dec = (qb.astype(F32) * qd_ref[h]).astype(BF16)
            lhs = jnp.concatenate([sc_s[c, h], q_dec], axis=1)
            rhs = jnp.concatenate([vb, r_old.astype(BF16)], axis=0)
            o = jnp.dot(lhs, rhs, preferred_element_type=F32)
            r_s[h] = r_old * cd_ref[h] + kv_s[c, h]
            o_ref[rows, hs] = _norm_gate(o, g_ref[rows, hs].astype(F32),
                                         gnw_ref[:, hs]).astype(o_ref.dtype)
    rn_ref[0] = r_s[...]


def _ret_step_kernel(q_ref, k_ref, v_ref, g_ref, mask_ref, qd_ref, kd_ref, cd_ref,
                     gnw_ref, r0_ref, o_ref, rn_ref, qr_s, kdt_s, ob_s, *, n_seq):
    rows_per_seq = RET_TILE // n_seq
    for h in range(RET_HEADS):
        hs = slice(h * HEAD_DIM, (h + 1) * HEAD_DIM)
        qb, kb = q_ref[:, hs], k_ref[:, hs]
        sc = lax.dot_general(qb, kb, _NT, preferred_element_type=F32) * mask_ref[h]
        ob_s[:, hs] = jnp.dot(sc.astype(BF16), v_ref[:, hs], preferred_element_type=F32)
        qr_s[:, hs] = qb.astype(F32) * qd_ref[h]
        kdt_s[h] = (kb.astype(F32) * kd_ref[h]).T

    lane = lax.broadcasted_iota(jnp.int32, (HEAD_DIM, RET_TILE), 1)
    for s in range(n_seq):
        rows = slice(s * rows_per_seq, (s + 1) * rows_per_seq)
        in_seq = (lane >= s * rows_per_seq) & (lane < (s + 1) * rows_per_seq)
        for h in range(RET_HEADS):
            hs = slice(h * HEAD_DIM, (h + 1) * HEAD_DIM)
            r_old = r0_ref[s, h]
            ob_s[rows, hs] += jnp.dot(qr_s[rows, hs].astype(BF16), r_old.astype(BF16),
                                      preferred_element_type=F32)
            kdt = jnp.where(in_seq, kdt_s[h], 0.0).astype(BF16)
            rn_ref[s, h] = r_old * cd_ref[h] + jnp.dot(kdt, v_ref[:, hs], preferred_element_type=F32)

    for h in range(RET_HEADS):
        hs = slice(h * HEAD_DIM, (h + 1) * HEAD_DIM)
        o_ref[:, hs] = _norm_gate(ob_s[:, hs], g_ref[:, hs].astype(F32),
                                  gnw_ref[:, hs]).astype(o_ref.dtype)


def _decay_tables(chunk, n_seq):
    scale = HEAD_DIM ** -0.5
    log_gamma = np.log(1.0 - 2.0 ** (-5.0 - np.arange(RET_HEADS, dtype=np.float64)))
    idx = np.arange(chunk, dtype=np.float64)
    diff = idx[:, None] - idx[None, :]
    mask = np.where(diff >= 0, np.exp(log_gamma[:, None, None] * np.maximum(diff, 0.0)), 0.0)
    q_decay = np.exp(log_gamma[:, None] * (idx + 1.0))
    k_decay = np.exp(log_gamma[:, None] * (chunk - 1.0 - idx)) * scale
    chunk_decay = np.exp(log_gamma * chunk)
    mask_t = scale * np.einsum("hab,st->hsatb", mask, np.eye(n_seq)).reshape(
        RET_HEADS, RET_TILE, RET_TILE)
    qd_t = np.broadcast_to(np.tile(q_decay, (1, n_seq))[:, :, None], (RET_HEADS, RET_TILE, HEAD_DIM))
    kd_t = np.broadcast_to(np.tile(k_decay, (1, n_seq))[:, :, None], (RET_HEADS, RET_TILE, HEAD_DIM))
    const = lambda a: jnp.asarray(np.ascontiguousarray(a), dtype=F32)
    return const(mask_t), const(qd_t), const(kd_t), const(chunk_decay)


def _retention(proj, r0, gn_w, n_seq, seq_len, chunks=8):
    first, t = proj.shape[0] - 4, proj.shape[1]
    const3 = lambda *_: (0, 0, 0)
    if r0 is None:
        tile, tile_seqs = chunks * RET_TILE, 1
        steps = seq_len // tile
        grid = (n_seq, steps)
        row = lambda n, c: n * steps + c
        state_map = lambda n, c: (n, 0, 0, 0)
        state_block = (1, RET_HEADS, HEAD_DIM, HEAD_DIM)
        sem = ("arbitrary", "arbitrary")
        body = functools.partial(_ret_seq_kernel, chunks=chunks)
        scratch = [pltpu.VMEM((RET_HEADS, HEAD_DIM, HEAD_DIM), F32),
                   pltpu.VMEM((chunks, RET_HEADS, RET_TILE, RET_TILE), BF16),
                   pltpu.VMEM((chunks, RET_HEADS, HEAD_DIM, HEAD_DIM), F32)]
        name = "ret_seq"
    else:
        tile, tile_seqs = RET_TILE, RET_TILE // seq_len
        grid = (t // tile,)
        row = lambda i: i
        state_map = lambda i: (i, 0, 0, 0)
        state_block = (tile_seqs, RET_HEADS, HEAD_DIM, HEAD_DIM)
        sem = ("arbitrary",)
        body = functools.partial(_ret_step_kernel, n_seq=tile_seqs)
        scratch = [pltpu.VMEM((RET_TILE, RET_WIDTH), F32),
                   pltpu.VMEM((RET_HEADS, HEAD_DIM, RET_TILE), F32),
                   pltpu.VMEM((RET_TILE, RET_WIDTH), F32)]
        name = "ret_step"
    mask_t, qd_t, kd_t, cd = _decay_tables(RET_TILE // tile_seqs, tile_seqs)

    def col(cb):
        return pl.BlockSpec((None, tile, RET_WIDTH), lambda *a: (first + cb, row(*a), 0))

    in_specs = [
        col(0), col(1), col(2), col(3),
        pl.BlockSpec((RET_HEADS, RET_TILE, RET_TILE), const3),
        pl.BlockSpec((RET_HEADS, RET_TILE, HEAD_DIM), const3),
        pl.BlockSpec((RET_HEADS, RET_TILE, HEAD_DIM), const3),
        pl.BlockSpec(memory_space=pltpu.SMEM),
        pl.BlockSpec((1, RET_WIDTH), lambda *_: (0, 0)),
    ]
    args = [proj, proj, proj, proj, mask_t, qd_t, kd_t, cd, gn_w]
    if r0 is not None:
        in_specs.append(pl.BlockSpec(state_block, state_map))
        args.append(r0)
    return pl.pallas_call(
        body,
        grid=grid,
        in_specs=in_specs,
        out_specs=(pl.BlockSpec((tile, RET_WIDTH), lambda *a: (row(*a), 0)),
                   pl.BlockSpec(state_block, state_map)),
        out_shape=(jax.ShapeDtypeStruct((t, RET_WIDTH), BF16),
                   jax.ShapeDtypeStruct((n_seq, RET_HEADS, HEAD_DIM, HEAD_DIM), F32)),
        scratch_shapes=scratch,
        compiler_params=_params(*sem),
        name=name,
    )(*args)


def _outproj_kernel(x_ref, a_ref, b_ref, wa_ref, wb_ref, o_ref, *w_bf16, cast):
    if cast:
        (wo_ref,) = w_bf16

        @pl.when(pl.program_id(0) == 0)
        def _():
            wo_ref[:S5_WIDTH, :] = wa_ref[...].astype(BF16)
            wo_ref[S5_WIDTH:, :] = wb_ref[...].astype(BF16)

        wa, wb = wo_ref[:S5_WIDTH, :], wo_ref[S5_WIDTH:, :]
    else:
        wa, wb = wa_ref[...], wb_ref[...]
    o_ref[...] = (x_ref[...]
                  + jnp.dot(a_ref[...], wa, preferred_element_type=F32)
                  + jnp.dot(b_ref[...], wb, preferred_element_type=F32))


def _outproj(x2d, a, b, w, tm=512):
    t = x2d.shape[0]
    cast = w.dtype == F32
    resident = pl.Buffered(1)
    out_specs = [pl.BlockSpec((tm, D_MODEL), lambda i: (i, 0))]
    out_shape = [jax.ShapeDtypeStruct((t, D_MODEL), F32)]
    if cast:
        out_specs.append(pl.BlockSpec((S5_WIDTH + RET_WIDTH, D_MODEL), lambda i: (0, 0)))
        out_shape.append(jax.ShapeDtypeStruct((S5_WIDTH + RET_WIDTH, D_MODEL), BF16))
    return pl.pallas_call(
        functools.partial(_outproj_kernel, cast=cast),
        grid=(t // tm,),
        in_specs=[
            pl.BlockSpec((tm, D_MODEL), lambda i: (i, 0)),
            pl.BlockSpec((tm, S5_WIDTH), lambda i: (i, 0)),
            pl.BlockSpec((tm, RET_WIDTH), lambda i: (i, 0)),
            pl.BlockSpec((S5_WIDTH, D_MODEL), lambda i: (0, 0), pipeline_mode=resident),
            pl.BlockSpec((RET_WIDTH, D_MODEL), lambda i: (1, 0), pipeline_mode=resident),
        ],
        out_specs=out_specs,
        out_shape=out_shape,
        compiler_params=_params("arbitrary"),
        name="out_proj_cast" if cast else "out_proj",
    )(x2d, a, b, w, w)


def _ffn_kernel(x_ref, gn_ref, wg_ref, wu_ref, wo_ref, gf_ref, o_ref, *rest, cast):
    if cast:
        wgb_ref, wub_ref, wob_ref, h_scr = rest
        wgb_ref[...] = wg_ref[...].astype(BF16)
        wub_ref[...] = wu_ref[...].astype(BF16)
        wob_ref[...] = wo_ref[...].astype(BF16)
        wg_ref, wu_ref, wo_ref = wgb_ref, wub_ref, wob_ref
    else:
        (h_scr,) = rest
    j = pl.program_id(1)
    last_j = pl.num_programs(1) - 1

    def step(first, last):
        for r in range(x_ref.shape[0] // FFN_ROWS):
            rows = slice(r * FFN_ROWS, (r + 1) * FFN_ROWS)
            if first:
                h_scr[rows, :] = _rms(x_ref[rows, :], gn_ref[...]).astype(BF16)
            h = h_scr[rows, :]
            gate = jnp.dot(h, wg_ref[...], preferred_element_type=F32)
            up = jnp.dot(h, wu_ref[...], preferred_element_type=F32)
            act = (jax.nn.silu(gate) * up).astype(BF16)
            acc = (x_ref if first else o_ref)[rows, :] + jnp.dot(act, wo_ref[...],
                                                                  preferred_element_type=F32)
            o_ref[rows, :] = _rms(acc, gf_ref[...]) if last else acc

    pl.when(j == 0)(lambda: step(True, False))
    pl.when((j > 0) & (j < last_j))(lambda: step(False, False))
    pl.when(j == last_j)(lambda: step(False, True))


def _ffn(x2d, g_ffn, w_gate, w_up, w_down, g_final, tm=1024, tf=512):
    t = x2d.shape[0]
    nf = D_FF // tf
    cast = w_down.dtype == F32
    (wg, g0), (wu, u0) = w_gate, w_up
    g0, u0 = g0 * nf, u0 * nf
    x_mode = {}
    out_specs = [pl.BlockSpec((tm, D_MODEL), lambda i, j: (i, 0))]
    out_shape = [jax.ShapeDtypeStruct((t, D_MODEL), F32)]
    if cast:
        assert t == tm, "the bf16 weight outputs are written once per row tile"
        x_mode = dict(pipeline_mode=pl.Buffered(1))
        out_specs += [pl.BlockSpec((D_MODEL, tf), lambda i, j: (0, j)),
                      pl.BlockSpec((D_MODEL, tf), lambda i, j: (0, j)),
                      pl.BlockSpec((tf, D_MODEL), lambda i, j: (j, 0))]
        out_shape += [jax.ShapeDtypeStruct((D_MODEL, D_FF), BF16),
                      jax.ShapeDtypeStruct((D_MODEL, D_FF), BF16),
                      jax.ShapeDtypeStruct((D_FF, D_MODEL), BF16)]
    return pl.pallas_call(
        functools.partial(_ffn_kernel, cast=cast),
        grid=(t // tm, nf),
        in_specs=[
            pl.BlockSpec((tm, D_MODEL), lambda i, j: (i, 0), **x_mode),
            pl.BlockSpec((1, D_MODEL), lambda i, j: (0, 0)),
            pl.BlockSpec((D_MODEL, tf), lambda i, j: (0, j + g0)),
            pl.BlockSpec((D_MODEL, tf), lambda i, j: (0, j + u0)),
            pl.BlockSpec((tf, D_MODEL), lambda i, j: (j, 0)),
            pl.BlockSpec((1, D_MODEL), lambda i, j: (0, 0)),
        ],
        out_specs=out_specs,
        out_shape=out_shape,
        scratch_shapes=[pltpu.VMEM((tm, D_MODEL), BF16)],
        compiler_params=_params("arbitrary", "arbitrary"),
        name="ffn_cast" if cast else "ffn",
    )(x2d, g_ffn, wg, wu, w_down, g_final)


def _finish(x2d, proj, s5_out, ret_state, w, w_out, ffn_w, ffn_tf, n, l):
    ret_out, ret_new = _retention(proj, ret_state, w["gn_w"], n, l)
    x1, *w_out_b = _outproj(x2d, s5_out, ret_out, w_out)
    y, *ffn_b = _ffn(x1, w["norm_ffn"], *ffn_w, w["norm_final"], tf=ffn_tf)
    return y.reshape(n, l, D_MODEL), ret_new, w_out_b, ffn_b


def _as_groups(re, im, n):
    return re.reshape(n, S5_GROUPS, S5_STATE), im.reshape(n, S5_GROUPS, S5_STATE)


def _sample_layer(x, s5_state, ret_state, w, big):
    n, l, _ = x.shape
    x2d = x.reshape(n * l, D_MODEL)
    proj, w_in_b, u_t = _inproj_cast(x2d, w["norm_mix"], big["w_in"], l, float(PAST_LEN))
    s5_out, s5_re, s5_im, w_glu_b = _s5_step(u_t, s5_state, w["pairs"], w["wb"], w["cd"], w["d"],
                                             big["w_glu"], w["b_glu"])
    ffn_f32 = ((big["w_ffn_in"], 0), (big["w_ffn_in"], 1), big["w_ffn_out"])
    y, ret_new, (w_out_b,), (wg_b, wu_b, wd_b) = _finish(
        x2d, proj, s5_out, ret_state, w, big["w_out"], ffn_f32, 256, n, l)
    big_b = dict(w_in=w_in_b, w_glu=w_glu_b, w_out=w_out_b, ffn=((wg_b, 0), (wu_b, 0), wd_b))
    return (y, *_as_groups(s5_re, s5_im, n), ret_new), big_b


def _prompt_layer(x, w, big_b):
    n, l, _ = x.shape
    x2d = x.reshape(n * l, D_MODEL)
    proj, u_t = _inproj_seq(x2d, w["norm_mix"], big_b["w_in"], n, 0.0)
    s5_out, s5_re, s5_im = _s5_seq(u_t, w["pairs"], w["wb"], w["cd"], w["d"], big_b["w_glu"],
                                   w["b_glu"], n, l)
    y, ret_new, _, _ = _finish(x2d, proj, s5_out, None, w, big_b["w_out"], big_b["ffn"],
                               512, n, l)
    return (y, *_as_groups(s5_re, s5_im, n), ret_new)


def kernel(x_prompt, x_sample, state_s5_re, state_s5_im, state_ret, norm_mix, w_in, s5_lambda_re, s5_lambda_im, s5_log_step, s5_b_re, s5_b_im, s5_c_re, s5_c_im, s5_d, s5_w_glu, s5_b_glu, ret_gn_w, w_out, norm_ffn, w_ffn_in, w_ffn_out, norm_final):
    assert norm_mix.shape[0] == 1, "single-layer stack"
    pairs, wb, cd = _s5_params(s5_lambda_re[0], s5_lambda_im[0], s5_log_step[0],
                               s5_b_re[0], s5_b_im[0], s5_c_re[0], s5_c_im[0])
    w = dict(norm_mix=norm_mix, pairs=pairs, wb=wb, cd=cd, d=s5_d, b_glu=s5_b_glu,
             gn_w=ret_gn_w, norm_ffn=norm_ffn, norm_final=norm_final.reshape(1, D_MODEL))
    big = dict(w_in=w_in[0], w_glu=s5_w_glu[0], w_out=w_out[0],
               w_ffn_in=w_ffn_in[0], w_ffn_out=w_ffn_out[0])
    n_s = x_sample.shape[0]
    (ys, s_re, s_im, s_ret), big_b = _sample_layer(
        x_sample,
        (state_s5_re[0].reshape(n_s, N_STATE), state_s5_im[0].reshape(n_s, N_STATE)),
        state_ret[0], w, big)
    yp, p_re, p_im, p_ret = _prompt_layer(x_prompt, w, big_b)
    return (yp, ys, p_re[None], p_im[None], p_ret[None], s_re[None], s_im[None], s_ret[None])
```

```python
import functools
import math

import jax
import jax.numpy as jnp
import numpy as np
from jax import lax
from jax.experimental import pallas as pl
from jax.experimental.pallas import tpu as pltpu

F32 = jnp.float32
BF16 = jnp.bfloat16

D_MODEL = 2048
S5_WIDTH = 1024
S5_GROUP = 16
S5_GROUPS = 64
S5_STATE = 64
N_STATE = S5_GROUPS * S5_STATE
RET_WIDTH = 1024
RET_HEADS = 8
HEAD_DIM = 128
ROPE_BASE = 10000.0
D_FF = 5632
IN_WIDTH = S5_WIDTH + 4 * RET_WIDTH
NORM_EPS = 1e-6
PAST_LEN = 16384

SUBLANES = 8
SLAB_GROUPS = 8
N_SLABS = S5_GROUPS // SLAB_GROUPS
SLAB_IN = SLAB_GROUPS * S5_GROUP
SLAB_STATE = SLAB_GROUPS * S5_STATE
RET_TILE = 128
ROTARY_TILES = (1, 2)
FFN_ROWS = 512

VMEM_LIMIT_BYTES = 60 * 1024 * 1024


def _params(*sem):
    return pltpu.CompilerParams(dimension_semantics=sem, vmem_limit_bytes=VMEM_LIMIT_BYTES)


def _rms(xf, g):
    ms = jnp.mean(xf * xf, axis=-1, keepdims=True)
    return xf * lax.rsqrt(ms + NORM_EPS) * g


def _rotary(x, cos2, sin2):
    return x * cos2 + pltpu.roll(x, HEAD_DIM // 2, 1) * sin2


def _rotary_heads(x, cos2, sin2):
    return jnp.concatenate(
        [_rotary(x[:, h * HEAD_DIM:(h + 1) * HEAD_DIM], cos2, sin2) for h in range(RET_HEADS)],
        axis=1)


def _rotary_tables(pos0, rows, reps):
    half = HEAD_DIM // 2
    inv_freq = ROPE_BASE ** (-np.arange(half, dtype=np.float64) / half)
    pos = pos0 + np.arange(rows, dtype=np.float64)
    ang = pos[:, None] * inv_freq[None, :]
    cos, sin = np.cos(ang), np.sin(ang)
    cos2 = np.concatenate([cos, cos], axis=-1)
    sin2 = np.concatenate([-sin, sin], axis=-1)
    return (jnp.asarray(np.tile(cos2, (reps, 1)), dtype=F32),
            jnp.asarray(np.tile(sin2, (reps, 1)), dtype=F32))


def _inproj_seq_kernel(x_ref, g_ref, cos_ref, sin_ref, w_ref, *rest, tn, n_seq):
    n_cast = (len(rest) - 3) // 2
    cast_in, (o_ref, ou_ref), cast_out = rest[:n_cast], rest[n_cast:n_cast + 2], rest[n_cast + 2:-1]
    h_scr = rest[-1]
    for src, dst in zip(cast_in, cast_out):
        dst[...] = src[...].astype(BF16)
    seq = pl.program_id(1)
    tm = x_ref.shape[0]
    h_scr[...] = _rms(x_ref[...], g_ref[...]).astype(BF16)
    for j in range(IN_WIDTH // tn):
        cols = slice(j * tn, (j + 1) * tn)
        res = jnp.dot(h_scr[...], w_ref[:, cols], preferred_element_type=F32)
        if j == 0:
            dst = pl.ds(seq, tm, stride=n_seq)
            for s in range(N_SLABS):
                ou_ref[s, dst, :] = res[:, s * SLAB_IN:(s + 1) * SLAB_IN]
            continue
        if j in ROTARY_TILES:
            res = _rotary_heads(res, cos_ref[...], sin_ref[...])
        o_ref[j - 1] = res.astype(o_ref.dtype)


def _inproj_seq(x2d, g, w, n_seq, pos0, to_cast, tm=256, tn=RET_WIDTH):
    assert tn == S5_WIDTH == RET_WIDTH
    t = x2d.shape[0]
    tiles = t // n_seq // tm
    steps = tiles * n_seq
    row = lambda i, n: (n * tiles + i, 0)
    const = lambda i, n: (0, 0)
    cos2, sin2 = _rotary_tables(pos0, t // n_seq, 1)
    cast_specs = [pl.BlockSpec((m.shape[0] // steps, m.shape[1]), lambda i, n: (i * n_seq + n, 0))
                  for m in to_cast]
    assert all(m.shape[0] % (16 * steps) == 0 for m in to_cast)
    return pl.pallas_call(
        functools.partial(_inproj_seq_kernel, tn=tn, n_seq=n_seq),
        grid=(tiles, n_seq),
        in_specs=[
            pl.BlockSpec((tm, D_MODEL), row),
            pl.BlockSpec((1, D_MODEL), const),
            pl.BlockSpec((tm, HEAD_DIM), lambda i, n: (i, 0)),
            pl.BlockSpec((tm, HEAD_DIM), lambda i, n: (i, 0)),
            pl.BlockSpec((D_MODEL, IN_WIDTH), const, pipeline_mode=pl.Buffered(1)),
        ] + cast_specs,
        out_specs=[pl.BlockSpec((4, tm, RET_WIDTH), lambda i, n: (0, n * tiles + i, 0)),
                   pl.BlockSpec((N_SLABS, n_seq * tm, SLAB_IN), lambda i, n: (0, i, 0))]
        + cast_specs,
        out_shape=[jax.ShapeDtypeStruct((4, t, RET_WIDTH), BF16),
                   jax.ShapeDtypeStruct((N_SLABS, t, SLAB_IN), F32)]
        + [jax.ShapeDtypeStruct(m.shape, BF16) for m in to_cast],
        scratch_shapes=[pltpu.VMEM((tm, D_MODEL), BF16)],
        compiler_params=_params("arbitrary", "arbitrary"),
        name="in_proj_seq",
    )(x2d, g, cos2, sin2, w, *to_cast)


def _inproj_cast_kernel(x_ref, g_ref, cos_ref, sin_ref, w_ref, o_ref, wb_ref, ou_ref, h_scr, ru_s):
    j = pl.program_id(0)

    @pl.when(j == 0)
    def _():
        h_scr[...] = _rms(x_ref[...], g_ref[...]).astype(BF16)

    wb_ref[...] = w_ref[...].astype(BF16)
    res = jnp.dot(h_scr[...], wb_ref[...], preferred_element_type=F32)
    is_rotary = functools.reduce(jnp.logical_or, [j == r for r in ROTARY_TILES])

    @pl.when(is_rotary)
    def _():
        o_ref[...] = _rotary_heads(res, cos_ref[...], sin_ref[...]).astype(o_ref.dtype)

    @pl.when(jnp.logical_not(is_rotary))
    def _():
        o_ref[...] = res.astype(o_ref.dtype)

    @pl.when(j == 0)
    def _():
        seq_len, n_seq = ou_ref.shape[1], ou_ref.shape[2]
        for s in range(N_SLABS):
            ru_s[s] = res[:, s * SLAB_IN:(s + 1) * SLAB_IN]
        for s in range(N_SLABS):
            for t in range(seq_len):
                ou_ref[s, t] = ru_s[s, pl.ds(t, n_seq, stride=seq_len), :]


def _inproj_cast(x2d, g, w_f32, seq_len, pos0):
    t = x2d.shape[0]
    tn = RET_WIDTH
    cos2, sin2 = _rotary_tables(pos0, seq_len, t // seq_len)
    return pl.pallas_call(
        _inproj_cast_kernel,
        grid=(IN_WIDTH // tn,),
        in_specs=[
            pl.BlockSpec((t, D_MODEL), lambda j: (0, 0), pipeline_mode=pl.Buffered(1)),
            pl.BlockSpec((1, D_MODEL), lambda j: (0, 0)),
            pl.BlockSpec((t, HEAD_DIM), lambda j: (0, 0)),
            pl.BlockSpec((t, HEAD_DIM), lambda j: (0, 0)),
            pl.BlockSpec((D_MODEL, tn), lambda j: (0, j)),
        ],
        out_specs=[pl.BlockSpec((None, t, tn), lambda j: (j, 0, 0)),
                   pl.BlockSpec((D_MODEL, tn), lambda j: (0, j)),
                   pl.BlockSpec((N_SLABS, seq_len, t // seq_len, SLAB_IN), lambda j: (0, 0, 0, 0))],
        out_shape=[jax.ShapeDtypeStruct((IN_WIDTH // tn, t, tn), BF16),
                   jax.ShapeDtypeStruct((D_MODEL, IN_WIDTH), BF16),
                   jax.ShapeDtypeStruct((N_SLABS, seq_len, t // seq_len, SLAB_IN), F32)],
        scratch_shapes=[pltpu.VMEM((t, D_MODEL), BF16), pltpu.VMEM((N_SLABS, t, SLAB_IN), F32)],
        compiler_params=_params("arbitrary"),
        name="in_proj_cast",
    )(x2d, g, cos2, sin2, w_f32)


def _s5_param_kernel(lr_ref, li_ref, ls_ref, br_ref, bi_ref, ctr_ref, cti_ref,
                     pairr_ref, pairi_ref, wbr_ref, wbi_ref, cdr_ref, cdi_ref):
    lr = lr_ref[...]
    li = li_ref[...]
    dt = jnp.exp(ls_ref[...])
    mag = jnp.exp(lr * dt)
    ar = mag * jnp.cos(li * dt)
    ai = mag * jnp.sin(li * dt)
    den = lr * lr + li * li
    nr = ar - 1.0
    cr = (nr * lr + ai * li) / den
    ci = (ai * lr - nr * li) / den
    br = br_ref[...]
    bi = bi_ref[...]
    bbr = cr * br - ci * bi
    bbi = cr * bi + ci * br
    abr = ar * bbr - ai * bbi
    abi = ar * bbi + ai * bbr

    in_shape = (SLAB_IN, SLAB_STATE)
    same_in = (lax.broadcasted_iota(jnp.int32, in_shape, 0) // S5_GROUP
               == lax.broadcasted_iota(jnp.int32, in_shape, 1) // S5_STATE)
    out_shape = (SLAB_STATE, SLAB_IN)
    same_out = (lax.broadcasted_iota(jnp.int32, out_shape, 0) // S5_STATE
                == lax.broadcasted_iota(jnp.int32, out_shape, 1) // S5_GROUP)
    for s in range(N_SLABS):
        cs = slice(s * SLAB_STATE, (s + 1) * SLAB_STATE)
        for dst, top, bot in ((wbr_ref, bbr, abr), (wbi_ref, bbi, abi)):
            for k, part in enumerate((top, bot)):
                blk = jnp.tile(part[:, cs], (SLAB_GROUPS, 1))
                dst[s, k * SLAB_IN:(k + 1) * SLAB_IN, :] = jnp.where(same_in, blk, 0.0).astype(BF16)
        cdr_ref[s] = jnp.where(same_out, ctr_ref[cs, :], 0.0).astype(BF16)
        cdi_ref[s] = jnp.where(same_out, cti_ref[cs, :], 0.0).astype(BF16)

    rows = lax.broadcasted_iota(jnp.int32, (SUBLANES, N_STATE), 0)
    second = rows >= SUBLANES // 2
    full = lambda v: jnp.broadcast_to(v, (SUBLANES, N_STATE))
    pairr_ref[...] = jnp.where(second, full(ar * ar - ai * ai), full(ar))
    pairi_ref[...] = jnp.where(second, full(ar * ai + ai * ar), full(ai))


def _s5_params(lam_re, lam_im, log_step, b_re, b_im, c_re, c_im):
    lr = lam_re.reshape(1, N_STATE)
    li = lam_im.reshape(1, N_STATE)
    ls = jnp.repeat(log_step, S5_STATE).reshape(1, N_STATE)
    b_rows = lambda b: jnp.transpose(b, (2, 0, 1)).reshape(S5_GROUP, N_STATE)
    c_cols = lambda c: jnp.tile(jnp.transpose(c, (0, 2, 1)).reshape(N_STATE, S5_GROUP),
                                (1, SLAB_GROUPS))
    pair = jax.ShapeDtypeStruct((SUBLANES, N_STATE), F32)
    wb = jax.ShapeDtypeStruct((N_SLABS, 2 * SLAB_IN, SLAB_STATE), BF16)
    cd = jax.ShapeDtypeStruct((N_SLABS, SLAB_STATE, SLAB_IN), BF16)
    pairr, pairi, wbr, wbi, cdr, cdi = pl.pallas_call(
        _s5_param_kernel,
        out_shape=(pair, pair, wb, wb, cd, cd),
        name="s5_params",
    )(lr, li, ls, b_rows(b_re), b_rows(b_im), c_cols(c_re), c_cols(c_im))
    return (pairr, pairi), (wbr, wbi), (cdr, cdi)


def _emit_pipelined(stages, n):
    for step in range(n + len(stages) - 1):
        for lag, stage in enumerate(stages):
            if 0 <= step - lag < n:
                stage(step - lag)


def _cmul_add(xr, xi, pr, pi, vr, vi):
    return xr + (pr * vr - pi * vi), xi + (pr * vi + pi * vr)


def _s5_weight_specs(k_in):
    const3 = lambda *_: (0, 0, 0)
    const2 = lambda *_: (0, 0)
    return [
        pl.BlockSpec((SUBLANES, N_STATE), const2),
        pl.BlockSpec((SUBLANES, N_STATE), const2),
        pl.BlockSpec((N_SLABS, k_in, SLAB_STATE), const3),
        pl.BlockSpec((N_SLABS, k_in, SLAB_STATE), const3),
        pl.BlockSpec((N_SLABS, SLAB_STATE, SLAB_IN), const3),
        pl.BlockSpec((N_SLABS, SLAB_STATE, SLAB_IN), const3),
        pl.BlockSpec((1, S5_WIDTH), const2),
        pl.BlockSpec((S5_WIDTH, S5_WIDTH), const2),
        pl.BlockSpec((1, S5_WIDTH), const2),
    ]


def _s5_step_kernel(u_ref, x0r_ref, x0i_ref, pairr_ref, pairi_ref, wbr_ref, wbi_ref, cdr_ref, cdi_ref,
                    d_ref, wg_ref, bg_ref, o_ref, xlr_ref, xli_ref, wgb_ref, xr_s, xi_s, y_s, o_s):
    seq_len, n_seq = u_ref.shape[1], u_ref.shape[2]

    @pl.when(pl.program_id(0) == 0)
    def _():
        wgb_ref[...] = wg_ref[...].astype(BF16)

    def b_projection(s):
        cs = slice(s * SLAB_STATE, (s + 1) * SLAB_STATE)
        ub = u_ref[s].reshape(seq_len * n_seq, SLAB_IN).astype(BF16)
        xr_s[:, cs] = jnp.dot(ub, wbr_ref[s], preferred_element_type=F32)
        xi_s[:, cs] = jnp.dot(ub, wbi_ref[s], preferred_element_type=F32)

    def scan(s):
        cs = slice(s * SLAB_STATE, (s + 1) * SLAB_STATE)
        a_r = jnp.broadcast_to(pairr_ref[0:1, cs], (SUBLANES, SLAB_STATE))
        a_i = jnp.broadcast_to(pairi_ref[0:1, cs], (SUBLANES, SLAB_STATE))
        for b in range(n_seq // SUBLANES):
            seqs = slice(b * SUBLANES, (b + 1) * SUBLANES)
            xr, xi = x0r_ref[seqs, cs], x0i_ref[seqs, cs]
            for t in range(seq_len):
                rows = slice(t * n_seq + b * SUBLANES, t * n_seq + (b + 1) * SUBLANES)
                xr, xi = _cmul_add(xr_s[rows, cs], xi_s[rows, cs], a_r, a_i, xr, xi)
                xr_s[rows, cs] = xr
                xi_s[rows, cs] = xi
            xlr_ref[seqs, cs] = xr
            xli_ref[seqs, cs] = xi

    def c_projection(s):
        cs = slice(s * SLAB_STATE, (s + 1) * SLAB_STATE)
        us = slice(s * SLAB_IN, (s + 1) * SLAB_IN)
        u = u_ref[s].reshape(seq_len * n_seq, SLAB_IN)
        y = (jnp.dot(xr_s[:, cs].astype(BF16), cdr_ref[s], preferred_element_type=F32)
             - jnp.dot(xi_s[:, cs].astype(BF16), cdi_ref[s], preferred_element_type=F32))
        y_s[:, us] = jax.nn.gelu(y + d_ref[:, us] * u)

    _emit_pipelined((b_projection, scan, c_projection), N_SLABS)

    y = y_s[...]
    z = jnp.dot(y.astype(BF16), wgb_ref[...], preferred_element_type=F32) + bg_ref[...]
    o = y * jax.nn.sigmoid(z)
    for s in range(N_SLABS):
        for t in range(seq_len):
            o_s[s, pl.ds(t, n_seq, stride=seq_len), :] = (
                o[t * n_seq:(t + 1) * n_seq, s * SLAB_IN:(s + 1) * SLAB_IN])
    for s in range(N_SLABS):
        o_ref[:, s * SLAB_IN:(s + 1) * SLAB_IN] = o_s[s].astype(o_ref.dtype)


def _s5_step(u_t, x0, pairs, wb, cd, d, w_glu, b_glu, n_tile=32):
    _, seq_len, n_seq, _ = u_t.shape
    rows = n_tile * seq_len
    row = lambda i: (i, 0)
    st = jax.ShapeDtypeStruct((n_seq, N_STATE), F32)
    return pl.pallas_call(
        _s5_step_kernel,
        grid=(n_seq // n_tile,),
        in_specs=[pl.BlockSpec((N_SLABS, seq_len, n_tile, SLAB_IN), lambda i: (0, 0, i, 0)),
                  pl.BlockSpec((n_tile, N_STATE), row), pl.BlockSpec((n_tile, N_STATE), row)]
        + _s5_weight_specs(SLAB_IN),
        out_specs=(pl.BlockSpec((rows, S5_WIDTH), row), pl.BlockSpec((n_tile, N_STATE), row),
                   pl.BlockSpec((n_tile, N_STATE), row),
                   pl.BlockSpec((S5_WIDTH, S5_WIDTH), lambda i: (0, 0))),
        out_shape=(jax.ShapeDtypeStruct((n_seq * seq_len, S5_WIDTH), BF16), st, st,
                   jax.ShapeDtypeStruct((S5_WIDTH, S5_WIDTH), BF16)),
        scratch_shapes=[pltpu.VMEM((rows, N_STATE), F32), pltpu.VMEM((rows, N_STATE), F32),
                        pltpu.VMEM((rows, S5_WIDTH), F32),
                        pltpu.VMEM((N_SLABS, rows, SLAB_IN), F32)],
        compiler_params=_params("arbitrary"),
        name="s5_step",
    )(u_t, x0[0], x0[1], pairs[0], pairs[1], wb[0], wb[1], cd[0], cd[1], d, w_glu, b_glu)


def _s5_seq_kernel(u_ref, pairr_ref, pairi_ref, wbr_ref, wbi_ref, cdr_ref, cdi_ref, d_ref, wg_ref,
                   bg_ref, o_ref, xlr_ref, xli_ref, xr_s, xi_s, y_s, o_s, cbr_s, cbi_s, ul_s):
    n_rows = u_ref.shape[1]
    half = SUBLANES // 2
    tt = n_rows // half

    @pl.when(pl.program_id(0) == 0)
    def _():
        cbr_s[...] = jnp.zeros_like(cbr_s)
        cbi_s[...] = jnp.zeros_like(cbi_s)
        ul_s[...] = jnp.zeros_like(ul_s)

    first = lax.broadcasted_iota(jnp.int32, (SUBLANES, SLAB_IN), 0) < half

    def b_projection(s):
        cs = slice(s * SLAB_STATE, (s + 1) * SLAB_STATE)
        u = u_ref[s]
        shifted = pltpu.roll(u, half, 0)
        head = jnp.where(first, pltpu.roll(ul_s[s], half, 0), shifted[:SUBLANES])
        u_prev = jnp.concatenate([head, shifted[SUBLANES:]], axis=0)
        ul_s[s] = u[n_rows - SUBLANES:]
        ub = jnp.concatenate([u, u_prev], axis=1).astype(BF16)
        xr_s[:, cs] = jnp.dot(ub, wbr_ref[s], preferred_element_type=F32)
        xi_s[:, cs] = jnp.dot(ub, wbi_ref[s], preferred_element_type=F32)

    def scan(s):
        cs = slice(s * SLAB_STATE, (s + 1) * SLAB_STATE)
        pr = jnp.broadcast_to(pairr_ref[half:half + 1, cs], (SUBLANES, SLAB_STATE))
        pi = jnp.broadcast_to(pairi_ref[half:half + 1, cs], (SUBLANES, SLAB_STATE))
        cr, ci = cbr_s[:, cs], cbi_s[:, cs]
        for b in range(n_rows // SUBLANES):
            rows = slice(b * SUBLANES, (b + 1) * SUBLANES)
            cr, ci = _cmul_add(xr_s[rows, cs], xi_s[rows, cs], pr, pi, cr, ci)
            xr_s[rows, cs] = cr
            xi_s[rows, cs] = ci
        cbr_s[:, cs] = cr
        cbi_s[:, cs] = ci

    def c_projection(s):
        cs = slice(s * SLAB_STATE, (s + 1) * SLAB_STATE)
        us = slice(s * SLAB_IN, (s + 1) * SLAB_IN)
        y = (jnp.dot(xr_s[:, cs].astype(BF16), cdr_ref[s], preferred_element_type=F32)
             - jnp.dot(xi_s[:, cs].astype(BF16), cdi_ref[s], preferred_element_type=F32))
        y_s[:, us] = jax.nn.gelu(y + d_ref[:, us] * u_ref[s])

    _emit_pipelined((b_projection, scan, c_projection), N_SLABS)

    xlr_ref[...] = cbr_s[...]
    xli_ref[...] = cbi_s[...]

    y = y_s[...]
    z = jnp.dot(y.astype(BF16), wg_ref[...], preferred_element_type=F32) + bg_ref[...]
    o = y * jax.nn.sigmoid(z)
    for s in range(N_SLABS):
        o_s[s] = o[:, s * SLAB_IN:(s + 1) * SLAB_IN]
    for n in range(half):
        for s in range(N_SLABS):
            o_ref[n, :, s * SLAB_IN:(s + 1) * SLAB_IN] = (
                o_s[s, pl.ds(n, tt, stride=half), :].astype(o_ref.dtype))


def _s5_seq(u_t, pairs, wb, cd, d, w_glu, b_glu, n_seq, seq_len, tt=128):
    assert 2 * n_seq == SUBLANES
    n_rows = n_seq * tt
    st = jax.ShapeDtypeStruct((SUBLANES, N_STATE), F32)
    y, xlr, xli = pl.pallas_call(
        _s5_seq_kernel,
        grid=(seq_len // tt,),
        in_specs=[pl.BlockSpec((N_SLABS, n_rows, SLAB_IN), lambda i: (0, i, 0))]
        + _s5_weight_specs(2 * SLAB_IN),
        out_specs=(pl.BlockSpec((n_seq, tt, S5_WIDTH), lambda i: (0, i, 0)),
                   pl.BlockSpec((SUBLANES, N_STATE), lambda i: (0, 0)),
                   pl.BlockSpec((SUBLANES, N_STATE), lambda i: (0, 0))),
        out_shape=(jax.ShapeDtypeStruct((n_seq, seq_len, S5_WIDTH), BF16), st, st),
        scratch_shapes=[pltpu.VMEM((n_rows, N_STATE), F32), pltpu.VMEM((n_rows, N_STATE), F32),
                        pltpu.VMEM((n_rows, S5_WIDTH), F32),
                        pltpu.VMEM((N_SLABS, n_rows, SLAB_IN), F32),
                        pltpu.VMEM((SUBLANES, N_STATE), F32), pltpu.VMEM((SUBLANES, N_STATE), F32),
                        pltpu.VMEM((N_SLABS, SUBLANES, SLAB_IN), F32)],
        compiler_params=_params("arbitrary"),
        name="s5_seq",
    )(u_t, pairs[0], pairs[1], wb[0], wb[1], cd[0], cd[1], d, w_glu, b_glu)
    return y.reshape(n_seq * seq_len, S5_WIDTH), xlr[n_seq:], xli[n_seq:]


_NT = (((1,), (1,)), ((), ()))
_TN = (((0,), (0,)), ((), ()))


def _norm_gate(o, g, gn_w):
    mu = jnp.mean(o, axis=-1, keepdims=True)
    oc = o - mu
    var = jnp.mean(oc * oc, axis=-1, keepdims=True)
    return jax.nn.silu(g) * (oc * lax.rsqrt(var + NORM_EPS) * gn_w)


def _ret_seq_kernel(q_ref, k_ref, v_ref, g_ref, mask_ref, qd_ref, kd_ref, cd_ref,
                    gnw_ref, o_ref, rn_ref, r_s, sc_s, kv_s, *, chunks):
    @pl.when(pl.program_id(1) == 0)
    def _():
        r_s[...] = jnp.zeros_like(r_s)

    for c in range(chunks):
        rows = slice(c * RET_TILE, (c + 1) * RET_TILE)
        for h in range(RET_HEADS):
            hs = slice(h * HEAD_DIM, (h + 1) * HEAD_DIM)
            qb, kb, vb = q_ref[rows, hs], k_ref[rows, hs], v_ref[rows, hs]
            sc = lax.dot_general(qb, kb, _NT, preferred_element_type=F32) * mask_ref[h]
            sc_s[c, h] = sc.astype(BF16)
            k_dec = (kb.astype(F32) * kd_ref[h]).astype(BF16)
            kv_s[c, h] = lax.dot_general(k_dec, vb, _TN, preferred_element_type=F32)

    for c in range(chunks):
        rows = slice(c * RET_TILE, (c + 1) * RET_TILE)
        for h in range(RET_HEADS):
            hs = slice(h * HEAD_DIM, (h + 1) * HEAD_DIM)
            qb, vb = q_ref[rows, hs], v_ref[rows, hs]
            r_old = r_s[h]
            q_dec = (qb.astype(F32) * qd_ref[h]).astype(BF16)
            lhs = jnp.concatenate([sc_s[c, h], q_dec], axis=1)
            rhs = jnp.concatenate([vb, r_old.astype(BF16)], axis=0)
            o = jnp.dot(lhs, rhs, preferred_element_type=F32)
            r_s[h] = r_old * cd_ref[h] + kv_s[c, h]
            o_ref[rows, hs] = _norm_gate(o, g_ref[rows, hs].astype(F32),
                                         gnw_ref[:, hs]).astype(o_ref.dtype)
    rn_ref[0] = r_s[...]


def _ret_step_kernel(q_ref, k_ref, v_ref, g_ref, mask_ref, qd_ref, kd_ref, cd_ref,
                     gnw_ref, r0_ref, o_ref, rn_ref, qr_s, kdt_s, ob_s, *, n_seq):
    rows_per_seq = RET_TILE // n_seq
    for h in range(RET_HEADS):
        hs = slice(h * HEAD_DIM, (h + 1) * HEAD_DIM)
        qb, kb = q_ref[:, hs], k_ref[:, hs]
        sc = lax.dot_general(qb, kb, _NT, preferred_element_type=F32) * mask_ref[h]
        ob_s[:, hs] = jnp.dot(sc.astype(BF16), v_ref[:, hs], preferred_element_type=F32)
        qr_s[:, hs] = qb.astype(F32) * qd_ref[h]
        kdt_s[h] = (kb.astype(F32) * kd_ref[h]).T

    lane = lax.broadcasted_iota(jnp.int32, (HEAD_DIM, RET_TILE), 1)
    for s in range(n_seq):
        rows = slice(s * rows_per_seq, (s + 1) * rows_per_seq)
        in_seq = (lane >= s * rows_per_seq) & (lane < (s + 1) * rows_per_seq)
        for h in range(RET_HEADS):
            hs = slice(h * HEAD_DIM, (h + 1) * HEAD_DIM)
            r_old = r0_ref[s, h]
            ob_s[rows, hs] += jnp.dot(qr_s[rows, hs].astype(BF16), r_old.astype(BF16),
                                      preferred_element_type=F32)
            kdt = jnp.where(in_seq, kdt_s[h], 0.0).astype(BF16)
            rn_ref[s, h] = r_old * cd_ref[h] + jnp.dot(kdt, v_ref[:, hs], preferred_element_type=F32)

    for h in range(RET_HEADS):
        hs = slice(h * HEAD_DIM, (h + 1) * HEAD_DIM)
        o_ref[:, hs] = _norm_gate(ob_s[:, hs], g_ref[:, hs].astype(F32),
                                  gnw_ref[:, hs]).astype(o_ref.dtype)


def _decay_tables(chunk, n_seq):
    scale = HEAD_DIM ** -0.5
    log_gamma = np.log(1.0 - 2.0 ** (-5.0 - np.arange(RET_HEADS, dtype=np.float64)))
    idx = np.arange(chunk, dtype=np.float64)
    diff = idx[:, None] - idx[None, :]
    mask = np.where(diff >= 0, np.exp(log_gamma[:, None, None] * np.maximum(diff, 0.0)), 0.0)
    q_decay = np.exp(log_gamma[:, None] * (idx + 1.0))
    k_decay = np.exp(log_gamma[:, None] * (chunk - 1.0 - idx)) * scale
    chunk_decay = np.exp(log_gamma * chunk)
    mask_t = scale * np.einsum("hab,st->hsatb", mask, np.eye(n_seq)).reshape(
        RET_HEADS, RET_TILE, RET_TILE)
    qd_t = np.broadcast_to(np.tile(q_decay, (1, n_seq))[:, :, None], (RET_HEADS, RET_TILE, HEAD_DIM))
    kd_t = np.broadcast_to(np.tile(k_decay, (1, n_seq))[:, :, None], (RET_HEADS, RET_TILE, HEAD_DIM))
    const = lambda a: jnp.asarray(np.ascontiguousarray(a), dtype=F32)
    return const(mask_t), const(qd_t), const(kd_t), const(chunk_decay)


def _retention(proj, r0, gn_w, n_seq, seq_len, chunks=8):
    first, t = proj.shape[0] - 4, proj.shape[1]
    const3 = lambda *_: (0, 0, 0)
    if r0 is None:
        tile, tile_seqs = chunks * RET_TILE, 1
        steps = seq_len // tile
        grid = (n_seq, steps)
        row = lambda n, c: n * steps + c
        state_map = lambda n, c: (n, 0, 0, 0)
        state_block = (1, RET_HEADS, HEAD_DIM, HEAD_DIM)
        sem = ("arbitrary", "arbitrary")
        body = functools.partial(_ret_seq_kernel, chunks=chunks)
        scratch = [pltpu.VMEM((RET_HEADS, HEAD_DIM, HEAD_DIM), F32),
                   pltpu.VMEM((chunks, RET_HEADS, RET_TILE, RET_TILE), BF16),
                   pltpu.VMEM((chunks, RET_HEADS, HEAD_DIM, HEAD_DIM), F32)]
        name = "ret_seq"
    else:
        tile, tile_seqs = RET_TILE, RET_TILE // seq_len
        grid = (t // tile,)
        row = lambda i: i
        state_map = lambda i: (i, 0, 0, 0)
        state_block = (tile_seqs, RET_HEADS, HEAD_DIM, HEAD_DIM)
        sem = ("arbitrary",)
        body = functools.partial(_ret_step_kernel, n_seq=tile_seqs)
        scratch = [pltpu.VMEM((RET_TILE, RET_WIDTH), F32),
                   pltpu.VMEM((RET_HEADS, HEAD_DIM, RET_TILE), F32),
                   pltpu.VMEM((RET_TILE, RET_WIDTH), F32)]
        name = "ret_step"
    mask_t, qd_t, kd_t, cd = _decay_tables(RET_TILE // tile_seqs, tile_seqs)

    def col(cb):
        return pl.BlockSpec((None, tile, RET_WIDTH), lambda *a: (first + cb, row(*a), 0))

    in_specs = [
        col(0), col(1), col(2), col(3),
        pl.BlockSpec((RET_HEADS, RET_TILE, RET_TILE), const3),
        pl.BlockSpec((RET_HEADS, RET_TILE, HEAD_DIM), const3),
        pl.BlockSpec((RET_HEADS, RET_TILE, HEAD_DIM), const3),
        pl.BlockSpec(memory_space=pltpu.SMEM),
        pl.BlockSpec((1, RET_WIDTH), lambda *_: (0, 0)),
    ]
    args = [proj, proj, proj, proj, mask_t, qd_t, kd_t, cd, gn_w]
    if r0 is not None:
        in_specs.append(pl.BlockSpec(state_block, state_map))
        args.append(r0)
    return pl.pallas_call(
        body,
        grid=grid,
        in_specs=in_specs,
        out_specs=(pl.BlockSpec((tile, RET_WIDTH), lambda *a: (row(*a), 0)),
                   pl.BlockSpec(state_block, state_map)),
        out_shape=(jax.ShapeDtypeStruct((t, RET_WIDTH), BF16),
                   jax.ShapeDtypeStruct((n_seq, RET_HEADS, HEAD_DIM, HEAD_DIM), F32)),
        scratch_shapes=scratch,
        compiler_params=_params(*sem),
        name=name,
    )(*args)


def _outproj_kernel(x_ref, a_ref, b_ref, wa_ref, wb_ref, o_ref):
    for r in range(x_ref.shape[0] // FFN_ROWS):
        rows = slice(r * FFN_ROWS, (r + 1) * FFN_ROWS)
        o_ref[rows, :] = (x_ref[rows, :]
                          + jnp.dot(a_ref[rows, :], wa_ref[...], preferred_element_type=F32)
                          + jnp.dot(b_ref[rows, :], wb_ref[...], preferred_element_type=F32))


def _outproj(x2d, a, b, w, tm=1024):
    t = x2d.shape[0]
    resident = pl.Buffered(1)
    return pl.pallas_call(
        _outproj_kernel,
        grid=(t // tm,),
        in_specs=[
            pl.BlockSpec((tm, D_MODEL), lambda i: (i, 0)),
            pl.BlockSpec((tm, S5_WIDTH), lambda i: (i, 0)),
            pl.BlockSpec((tm, RET_WIDTH), lambda i: (i, 0)),
            pl.BlockSpec((S5_WIDTH, D_MODEL), lambda i: (0, 0), pipeline_mode=resident),
            pl.BlockSpec((RET_WIDTH, D_MODEL), lambda i: (1, 0), pipeline_mode=resident),
        ],
        out_specs=pl.BlockSpec((tm, D_MODEL), lambda i: (i, 0)),
        out_shape=jax.ShapeDtypeStruct((t, D_MODEL), F32),
        compiler_params=_params("arbitrary"),
        name="out_proj",
    )(x2d, a, b, w, w)


def _ffn_kernel(x_ref, gn_ref, wg_ref, wu_ref, wo_ref, gf_ref, o_ref, h_scr):
    j = pl.program_id(1)
    last_j = pl.num_programs(1) - 1

    def step(first, last):
        for r in range(x_ref.shape[0] // FFN_ROWS):
            rows = slice(r * FFN_ROWS, (r + 1) * FFN_ROWS)
            if first:
                h_scr[rows, :] = _rms(x_ref[rows, :], gn_ref[...]).astype(BF16)
            h = h_scr[rows, :]
            gate = jnp.dot(h, wg_ref[...], preferred_element_type=F32)
            up = jnp.dot(h, wu_ref[...], preferred_element_type=F32)
            act = (jax.nn.silu(gate) * up).astype(BF16)
            acc = (x_ref if first else o_ref)[rows, :] + jnp.dot(act, wo_ref[...],
                                                                  preferred_element_type=F32)
            o_ref[rows, :] = _rms(acc, gf_ref[...]) if last else acc

    pl.when(j == 0)(lambda: step(True, False))
    pl.when((j > 0) & (j < last_j))(lambda: step(False, False))
    pl.when(j == last_j)(lambda: step(False, True))


def _ffn(x2d, g_ffn, w_in, w_out, g_final, tm=1024, tf=512):
    t = x2d.shape[0]
    nf = D_FF // tf
    return pl.pallas_call(
        _ffn_kernel,
        grid=(t // tm, nf),
        in_specs=[
            pl.BlockSpec((tm, D_MODEL), lambda i, j: (i, 0)),
            pl.BlockSpec((1, D_MODEL), lambda i, j: (0, 0)),
            pl.BlockSpec((D_MODEL, tf), lambda i, j: (0, j)),
            pl.BlockSpec((D_MODEL, tf), lambda i, j: (0, j + nf)),
            pl.BlockSpec((tf, D_MODEL), lambda i, j: (j, 0)),
            pl.BlockSpec((1, D_MODEL), lambda i, j: (0, 0)),
        ],
        out_specs=pl.BlockSpec((tm, D_MODEL), lambda i, j: (i, 0)),
        out_shape=jax.ShapeDtypeStruct((t, D_MODEL), F32),
        scratch_shapes=[pltpu.VMEM((tm, D_MODEL), BF16)],
        compiler_params=_params("arbitrary", "arbitrary"),
        name="ffn",
    )(x2d, g_ffn, w_in, w_in, w_out, g_final)


def _finish(x2d, proj, s5, ret_state, w, n, l):
    s5_out, s5_re, s5_im = s5
    ret_out, ret_new = _retention(proj, ret_state, w["gn_w"], n, l)
    x1 = _outproj(x2d, s5_out, ret_out, w["w_out"])
    y = _ffn(x1, w["norm_ffn"], w["w_ffn_in"], w["w_ffn_out"], w["norm_final"])
    return (y.reshape(n, l, D_MODEL), s5_re.reshape(n, S5_GROUPS, S5_STATE),
            s5_im.reshape(n, S5_GROUPS, S5_STATE), ret_new)


def kernel(x_prompt, x_sample, state_s5_re, state_s5_im, state_ret, norm_mix, w_in, s5_lambda_re, s5_lambda_im, s5_log_step, s5_b_re, s5_b_im, s5_c_re, s5_c_im, s5_d, s5_w_glu, s5_b_glu, ret_gn_w, w_out, norm_ffn, w_ffn_in, w_ffn_out, norm_final):
    assert norm_mix.shape[0] == 1, "single-layer stack"
    pairs, wb, cd = _s5_params(s5_lambda_re[0], s5_lambda_im[0], s5_log_step[0],
                               s5_b_re[0], s5_b_im[0], s5_c_re[0], s5_c_im[0])
    w = dict(gn_w=ret_gn_w, norm_ffn=norm_ffn, norm_final=norm_final.reshape(1, D_MODEL))
    s5_w = (pairs, wb, cd, s5_d)
    n_s, l_s, _ = x_sample.shape
    n_p, l_p, _ = x_prompt.shape
    xs2d = x_sample.reshape(n_s * l_s, D_MODEL)
    xp2d = x_prompt.reshape(n_p * l_p, D_MODEL)

    proj_s, w_in_b, u_s = _inproj_cast(xs2d, norm_mix, w_in[0], l_s, float(PAST_LEN))
    proj_p, u_p, w["w_out"], w["w_ffn_in"], w["w_ffn_out"] = _inproj_seq(
        xp2d, norm_mix, w_in_b, n_p, 0.0, (w_out[0], w_ffn_in[0], w_ffn_out[0]))

    x0 = (state_s5_re[0].reshape(n_s, N_STATE), state_s5_im[0].reshape(n_s, N_STATE))
    *s5_s, w_glu_b = _s5_step(u_s, x0, *s5_w, s5_w_glu[0], s5_b_glu)
    ys, s_re, s_im, s_ret = _finish(xs2d, proj_s, s5_s, state_ret[0], w, n_s, l_s)

    s5_p = _s5_seq(u_p, *s5_w, w_glu_b, s5_b_glu, n_p, l_p)
    yp, p_re, p_im, p_ret = _finish(xp2d, proj_p, s5_p, None, w, n_p, l_p)
    return (yp, ys, p_re[None], p_im[None], p_ret[None], s_re[None], s_im[None], s_ret[None])
```

```python
import functools
import math

import jax
import jax.numpy as jnp
import numpy as np
from jax import lax
from jax.experimental import pallas as pl
from jax.experimental.pallas import tpu as pltpu

F32 = jnp.float32
BF16 = jnp.bfloat16

D_MODEL = 2048
S5_WIDTH = 1024
S5_GROUP = 16
S5_GROUPS = 64
S5_STATE = 64
N_STATE = S5_GROUPS * S5_STATE
RET_WIDTH = 1024
RET_HEADS = 8
HEAD_DIM = 128
ROPE_BASE = 10000.0
D_FF = 5632
IN_WIDTH = S5_WIDTH + 4 * RET_WIDTH
NORM_EPS = 1e-6
PAST_LEN = 16384

SUBLANES = 8
SLAB_GROUPS = 8
N_SLABS = S5_GROUPS // SLAB_GROUPS
SLAB_IN = SLAB_GROUPS * S5_GROUP
SLAB_STATE = SLAB_GROUPS * S5_STATE
RET_TILE = 128
ROTARY_TILES = (1, 2)
FFN_ROWS = 512

VMEM_LIMIT_BYTES = 60 * 1024 * 1024


def _params(*sem):
    return pltpu.CompilerParams(dimension_semantics=sem, vmem_limit_bytes=VMEM_LIMIT_BYTES)


def _rms(xf, g):
    ms = jnp.mean(xf * xf, axis=-1, keepdims=True)
    return xf * lax.rsqrt(ms + NORM_EPS) * g


def _rotary(x, cos2, sin2):
    return x * cos2 + pltpu.roll(x, HEAD_DIM // 2, 1) * sin2


def _rotary_heads(x, cos2, sin2):
    return jnp.concatenate(
        [_rotary(x[:, h * HEAD_DIM:(h + 1) * HEAD_DIM], cos2, sin2) for h in range(RET_HEADS)],
        axis=1)


def _rotary_tables(pos0, rows, reps):
    half = HEAD_DIM // 2
    inv_freq = ROPE_BASE ** (-np.arange(half, dtype=np.float64) / half)
    pos = pos0 + np.arange(rows, dtype=np.float64)
    ang = pos[:, None] * inv_freq[None, :]
    cos, sin = np.cos(ang), np.sin(ang)
    cos2 = np.concatenate([cos, cos], axis=-1)
    sin2 = np.concatenate([-sin, sin], axis=-1)
    return (jnp.asarray(np.tile(cos2, (reps, 1)), dtype=F32),
            jnp.asarray(np.tile(sin2, (reps, 1)), dtype=F32))


def _inproj_seq_kernel(x_ref, g_ref, cos_ref, sin_ref, w_ref, *rest, tn, n_seq):
    n_cast = (len(rest) - 3) // 2
    cast_in, (o_ref, ou_ref), cast_out = rest[:n_cast], rest[n_cast:n_cast + 2], rest[n_cast + 2:-1]
    h_scr = rest[-1]
    for src, dst in zip(cast_in, cast_out):
        dst[...] = src[...].astype(BF16)
    seq = pl.program_id(1)
    tm = x_ref.shape[0]
    h_scr[...] = _rms(x_ref[...], g_ref[...]).astype(BF16)
    for j in range(IN_WIDTH // tn):
        cols = slice(j * tn, (j + 1) * tn)
        res = jnp.dot(h_scr[...], w_ref[:, cols], preferred_element_type=F32)
        if j == 0:
            dst = pl.ds(seq, tm, stride=n_seq)
            for s in range(N_SLABS):
                ou_ref[s, dst, :] = res[:, s * SLAB_IN:(s + 1) * SLAB_IN]
            continue
        if j in ROTARY_TILES:
            res = _rotary_heads(res, cos_ref[...], sin_ref[...])
        o_ref[j - 1] = res.astype(o_ref.dtype)


def _inproj_seq(x2d, g, w, n_seq, pos0, to_cast, tm=256, tn=RET_WIDTH):
    assert tn == S5_WIDTH == RET_WIDTH
    t = x2d.shape[0]
    tiles = t // n_seq // tm
    steps = tiles * n_seq
    row = lambda i, n: (n * tiles + i, 0)
    const = lambda i, n: (0, 0)
    cos2, sin2 = _rotary_tables(pos0, t // n_seq, 1)
    cast_specs = [pl.BlockSpec((m.shape[0] // steps, m.shape[1]), lambda i, n: (i * n_seq + n, 0))
                  for m in to_cast]
    assert all(m.shape[0] % (16 * steps) == 0 for m in to_cast)
    return pl.pallas_call(
        functools.partial(_inproj_seq_kernel, tn=tn, n_seq=n_seq),
        grid=(tiles, n_seq),
        in_specs=[
            pl.BlockSpec((tm, D_MODEL), row),
            pl.BlockSpec((1, D_MODEL), const),
            pl.BlockSpec((tm, HEAD_DIM), lambda i, n: (i, 0)),
            pl.BlockSpec((tm, HEAD_DIM), lambda i, n: (i, 0)),
            pl.BlockSpec((D_MODEL, IN_WIDTH), const, pipeline_mode=pl.Buffered(1)),
        ] + cast_specs,
        out_specs=[pl.BlockSpec((4, tm, RET_WIDTH), lambda i, n: (0, n * tiles + i, 0)),
                   pl.BlockSpec((N_SLABS, n_seq * tm, SLAB_IN), lambda i, n: (0, i, 0))]
        + cast_specs,
        out_shape=[jax.ShapeDtypeStruct((4, t, RET_WIDTH), BF16),
                   jax.ShapeDtypeStruct((N_SLABS, t, SLAB_IN), F32)]
        + [jax.ShapeDtypeStruct(m.shape, BF16) for m in to_cast],
        scratch_shapes=[pltpu.VMEM((tm, D_MODEL), BF16)],
        compiler_params=_params("arbitrary", "arbitrary"),
        name="in_proj_seq",
    )(x2d, g, cos2, sin2, w, *to_cast)


def _inproj_cast_kernel(x_ref, g_ref, cos_ref, sin_ref, w_ref, o_ref, wb_ref, ou_ref, h_scr, ru_s):
    j = pl.program_id(0)

    @pl.when(j == 0)
    def _():
        h_scr[...] = _rms(x_ref[...], g_ref[...]).astype(BF16)

    wb = w_ref[...].astype(BF16)
    wb_ref[...] = wb
    is_rotary = functools.reduce(jnp.logical_or, [j == r for r in ROTARY_TILES])
    for r in range(x_ref.shape[0] // FFN_ROWS):
        rows = slice(r * FFN_ROWS, (r + 1) * FFN_ROWS)
        res = jnp.dot(h_scr[rows, :], wb, preferred_element_type=F32)
        rot = _rotary_heads(res, cos_ref[rows, :], sin_ref[rows, :])
        o_ref[rows, :] = jnp.where(is_rotary, rot, res).astype(o_ref.dtype)
        for s in range(N_SLABS):
            ru_s[s, rows, :] = res[:, s * SLAB_IN:(s + 1) * SLAB_IN]

    @pl.when(j == 0)
    def _():
        seq_len, n_seq = ou_ref.shape[1], ou_ref.shape[2]
        for s in range(N_SLABS):
            for t in range(seq_len):
                ou_ref[s, t] = ru_s[s, pl.ds(t, n_seq, stride=seq_len), :]


def _inproj_cast(x2d, g, w_f32, seq_len, pos0):
    t = x2d.shape[0]
    tn = RET_WIDTH
    cos2, sin2 = _rotary_tables(pos0, seq_len, t // seq_len)
    return pl.pallas_call(
        _inproj_cast_kernel,
        grid=(IN_WIDTH // tn,),
        in_specs=[
            pl.BlockSpec((t, D_MODEL), lambda j: (0, 0), pipeline_mode=pl.Buffered(1)),
            pl.BlockSpec((1, D_MODEL), lambda j: (0, 0)),
            pl.BlockSpec((t, HEAD_DIM), lambda j: (0, 0)),
            pl.BlockSpec((t, HEAD_DIM), lambda j: (0, 0)),
            pl.BlockSpec((D_MODEL, tn), lambda j: (0, j)),
        ],
        out_specs=[pl.BlockSpec((None, t, tn), lambda j: (j, 0, 0)),
                   pl.BlockSpec((D_MODEL, tn), lambda j: (0, j)),
                   pl.BlockSpec((N_SLABS, seq_len, t // seq_len, SLAB_IN), lambda j: (0, 0, 0, 0))],
        out_shape=[jax.ShapeDtypeStruct((IN_WIDTH // tn, t, tn), BF16),
                   jax.ShapeDtypeStruct((D_MODEL, IN_WIDTH), BF16),
                   jax.ShapeDtypeStruct((N_SLABS, seq_len, t // seq_len, SLAB_IN), F32)],
        scratch_shapes=[pltpu.VMEM((t, D_MODEL), BF16), pltpu.VMEM((N_SLABS, t, SLAB_IN), F32)],
        compiler_params=_params("arbitrary"),
        name="in_proj_cast",
    )(x2d, g, cos2, sin2, w_f32)


def _s5_param_kernel(lr_ref, li_ref, ls_ref, br_ref, bi_ref, ctr_ref, cti_ref,
                     pairr_ref, pairi_ref, wbr_ref, wbi_ref, cdr_ref, cdi_ref):
    lr = lr_ref[...]
    li = li_ref[...]
    dt = jnp.exp(ls_ref[...])
    mag = jnp.exp(lr * dt)
    ar = mag * jnp.cos(li * dt)
    ai = mag * jnp.sin(li * dt)
    den = lr * lr + li * li
    nr = ar - 1.0
    cr = (nr * lr + ai * li) / den
    ci = (ai * lr - nr * li) / den
    br = br_ref[...]
    bi = bi_ref[...]
    bbr = cr * br - ci * bi
    bbi = cr * bi + ci * br
    abr = ar * bbr - ai * bbi
    abi = ar * bbi + ai * bbr

    in_shape = (SLAB_IN, SLAB_STATE)
    same_in = (lax.broadcasted_iota(jnp.int32, in_shape, 0) // S5_GROUP
               == lax.broadcasted_iota(jnp.int32, in_shape, 1) // S5_STATE)
    out_shape = (SLAB_STATE, SLAB_IN)
    same_out = (lax.broadcasted_iota(jnp.int32, out_shape, 0) // S5_STATE
                == lax.broadcasted_iota(jnp.int32, out_shape, 1) // S5_GROUP)
    for s in range(N_SLABS):
        cs = slice(s * SLAB_STATE, (s + 1) * SLAB_STATE)
        for dst, top, bot in ((wbr_ref, bbr, abr), (wbi_ref, bbi, abi)):
            for k, part in enumerate((top, bot)):
                blk = jnp.tile(part[:, cs], (SLAB_GROUPS, 1))
                dst[s, k * SLAB_IN:(k + 1) * SLAB_IN, :] = jnp.where(same_in, blk, 0.0).astype(BF16)
        cdr_ref[s] = jnp.where(same_out, ctr_ref[cs, :], 0.0).astype(BF16)
        cdi_ref[s] = jnp.where(same_out, cti_ref[cs, :], 0.0).astype(BF16)

    rows = lax.broadcasted_iota(jnp.int32, (SUBLANES, N_STATE), 0)
    second = rows >= SUBLANES // 2
    full = lambda v: jnp.broadcast_to(v, (SUBLANES, N_STATE))
    pairr_ref[...] = jnp.where(second, full(ar * ar - ai * ai), full(ar))
    pairi_ref[...] = jnp.where(second, full(ar * ai + ai * ar), full(ai))


def _s5_params(lam_re, lam_im, log_step, b_re, b_im, c_re, c_im):
    lr = lam_re.reshape(1, N_STATE)
    li = lam_im.reshape(1, N_STATE)
    ls = jnp.repeat(log_step, S5_STATE).reshape(1, N_STATE)
    b_rows = lambda b: jnp.transpose(b, (2, 0, 1)).reshape(S5_GROUP, N_STATE)
    c_cols = lambda c: jnp.tile(jnp.transpose(c, (0, 2, 1)).reshape(N_STATE, S5_GROUP),
                                (1, SLAB_GROUPS))
    pair = jax.ShapeDtypeStruct((SUBLANES, N_STATE), F32)
    wb = jax.ShapeDtypeStruct((N_SLABS, 2 * SLAB_IN, SLAB_STATE), BF16)
    cd = jax.ShapeDtypeStruct((N_SLABS, SLAB_STATE, SLAB_IN), BF16)
    pairr, pairi, wbr, wbi, cdr, cdi = pl.pallas_call(
        _s5_param_kernel,
        out_shape=(pair, pair, wb, wb, cd, cd),
        name="s5_params",
    )(lr, li, ls, b_rows(b_re), b_rows(b_im), c_cols(c_re), c_cols(c_im))
    return (pairr, pairi), (wbr, wbi), (cdr, cdi)


def _emit_pipelined(stages, n):
    for step in range(n + len(stages) - 1):
        for lag, stage in enumerate(stages):
            if 0 <= step - lag < n:
                stage(step - lag)


def _cmul_add(xr, xi, pr, pi, vr, vi):
    return xr + (pr * vr - pi * vi), xi + (pr * vi + pi * vr)


def _s5_weight_specs(k_in):
    const3 = lambda *_: (0, 0, 0)
    const2 = lambda *_: (0, 0)
    return [
        pl.BlockSpec((SUBLANES, N_STATE), const2),
        pl.BlockSpec((SUBLANES, N_STATE), const2),
        pl.BlockSpec((N_SLABS, k_in, SLAB_STATE), const3),
        pl.BlockSpec((N_SLABS, k_in, SLAB_STATE), const3),
        pl.BlockSpec((N_SLABS, SLAB_STATE, SLAB_IN), const3),
        pl.BlockSpec((N_SLABS, SLAB_STATE, SLAB_IN), const3),
        pl.BlockSpec((1, S5_WIDTH), const2),
        pl.BlockSpec((S5_WIDTH, S5_WIDTH), const2),
        pl.BlockSpec((1, S5_WIDTH), const2),
    ]


def _s5_step_kernel(u_ref, x0r_ref, x0i_ref, pairr_ref, pairi_ref, wbr_ref, wbi_ref, cdr_ref, cdi_ref,
                    d_ref, wg_ref, bg_ref, o_ref, xlr_ref, xli_ref, wgb_ref, xr_s, xi_s, y_s, o_s):
    seq_len, n_seq = u_ref.shape[1], u_ref.shape[2]

    @pl.when(pl.program_id(0) == 0)
    def _():
        wgb_ref[...] = wg_ref[...].astype(BF16)

    def b_projection(s):
        cs = slice(s * SLAB_STATE, (s + 1) * SLAB_STATE)
        ub = u_ref[s].reshape(seq_len * n_seq, SLAB_IN).astype(BF16)
        xr_s[:, cs] = jnp.dot(ub, wbr_ref[s], preferred_element_type=F32)
        xi_s[:, cs] = jnp.dot(ub, wbi_ref[s], preferred_element_type=F32)

    def scan(s):
        cs = slice(s * SLAB_STATE, (s + 1) * SLAB_STATE)
        a_r = jnp.broadcast_to(pairr_ref[0:1, cs], (SUBLANES, SLAB_STATE))
        a_i = jnp.broadcast_to(pairi_ref[0:1, cs], (SUBLANES, SLAB_STATE))
        for b in range(n_seq // SUBLANES):
            seqs = slice(b * SUBLANES, (b + 1) * SUBLANES)
            xr, xi = x0r_ref[seqs, cs], x0i_ref[seqs, cs]
            for t in range(seq_len):
                rows = slice(t * n_seq + b * SUBLANES, t * n_seq + (b + 1) * SUBLANES)
                xr, xi = _cmul_add(xr_s[rows, cs], xi_s[rows, cs], a_r, a_i, xr, xi)
                xr_s[rows, cs] = xr
                xi_s[rows, cs] = xi
            xlr_ref[seqs, cs] = xr
            xli_ref[seqs, cs] = xi

    def c_projection(s):
        cs = slice(s * SLAB_STATE, (s + 1) * SLAB_STATE)
        us = slice(s * SLAB_IN, (s + 1) * SLAB_IN)
        u = u_ref[s].reshape(seq_len * n_seq, SLAB_IN)
        y = (jnp.dot(xr_s[:, cs].astype(BF16), cdr_ref[s], preferred_element_type=F32)
             - jnp.dot(xi_s[:, cs].astype(BF16), cdi_ref[s], preferred_element_type=F32))
        y_s[:, us] = jax.nn.gelu(y + d_ref[:, us] * u)

    _emit_pipelined((b_projection, scan, c_projection), N_SLABS)

    y = y_s[...]
    z = jnp.dot(y.astype(BF16), wgb_ref[...], preferred_element_type=F32) + bg_ref[...]
    o = y * jax.nn.sigmoid(z)
    for s in range(N_SLABS):
        for t in range(seq_len):
            o_s[s, pl.ds(t, n_seq, stride=seq_len), :] = (
                o[t * n_seq:(t + 1) * n_seq, s * SLAB_IN:(s + 1) * SLAB_IN])
    for s in range(N_SLABS):
        o_ref[:, s * SLAB_IN:(s + 1) * SLAB_IN] = o_s[s].astype(o_ref.dtype)


def _s5_step(u_t, x0, pairs, wb, cd, d, w_glu, b_glu, n_tile=32):
    _, seq_len, n_seq, _ = u_t.shape
    rows = n_tile * seq_len
    row = lambda i: (i, 0)
    st = jax.ShapeDtypeStruct((n_seq, N_STATE), F32)
    return pl.pallas_call(
        _s5_step_kernel,
        grid=(n_seq // n_tile,),
        in_specs=[pl.BlockSpec((N_SLABS, seq_len, n_tile, SLAB_IN), lambda i: (0, 0, i, 0)),
                  pl.BlockSpec((n_tile, N_STATE), row), pl.BlockSpec((n_tile, N_STATE), row)]
        + _s5_weight_specs(SLAB_IN),
        out_specs=(pl.BlockSpec((rows, S5_WIDTH), row), pl.BlockSpec((n_tile, N_STATE), row),
                   pl.BlockSpec((n_tile, N_STATE), row),
                   pl.BlockSpec((S5_WIDTH, S5_WIDTH), lambda i: (0, 0))),
        out_shape=(jax.ShapeDtypeStruct((n_seq * seq_len, S5_WIDTH), BF16), st, st,
                   jax.ShapeDtypeStruct((S5_WIDTH, S5_WIDTH), BF16)),
        scratch_shapes=[pltpu.VMEM((rows, N_STATE), F32), pltpu.VMEM((rows, N_STATE), F32),
                        pltpu.VMEM((rows, S5_WIDTH), F32),
                        pltpu.VMEM((N_SLABS, rows, SLAB_IN), F32)],
        compiler_params=_params("arbitrary"),
        name="s5_step",
    )(u_t, x0[0], x0[1], pairs[0], pairs[1], wb[0], wb[1], cd[0], cd[1], d, w_glu, b_glu)


def _s5_seq_kernel(u_ref, pairr_ref, pairi_ref, wbr_ref, wbi_ref, cdr_ref, cdi_ref, d_ref, wg_ref,
                   bg_ref, o_ref, xlr_ref, xli_ref, xr_s, xi_s, y_s, o_s, cbr_s, cbi_s, ul_s):
    n_rows = u_ref.shape[1]
    half = SUBLANES // 2
    tt = n_rows // half

    @pl.when(pl.program_id(0) == 0)
    def _():
        cbr_s[...] = jnp.zeros_like(cbr_s)
        cbi_s[...] = jnp.zeros_like(cbi_s)
        ul_s[...] = jnp.zeros_like(ul_s)

    first = lax.broadcasted_iota(jnp.int32, (SUBLANES, SLAB_IN), 0) < half

    def b_projection(s):
        cs = slice(s * SLAB_STATE, (s + 1) * SLAB_STATE)
        u = u_ref[s]
        shifted = pltpu.roll(u, half, 0)
        head = jnp.where(first, pltpu.roll(ul_s[s], half, 0), shifted[:SUBLANES])
        u_prev = jnp.concatenate([head, shifted[SUBLANES:]], axis=0)
        ul_s[s] = u[n_rows - SUBLANES:]
        ub = jnp.concatenate([u, u_prev], axis=1).astype(BF16)
        xr_s[:, cs] = jnp.dot(ub, wbr_ref[s], preferred_element_type=F32)
        xi_s[:, cs] = jnp.dot(ub, wbi_ref[s], preferred_element_type=F32)

    def scan(s):
        cs = slice(s * SLAB_STATE, (s + 1) * SLAB_STATE)
        pr = jnp.broadcast_to(pairr_ref[half:half + 1, cs], (SUBLANES, SLAB_STATE))
        pi = jnp.broadcast_to(pairi_ref[half:half + 1, cs], (SUBLANES, SLAB_STATE))
        cr, ci = cbr_s[:, cs], cbi_s[:, cs]
        for b in range(n_rows // SUBLANES):
            rows = slice(b * SUBLANES, (b + 1) * SUBLANES)
            cr, ci = _cmul_add(xr_s[rows, cs], xi_s[rows, cs], pr, pi, cr, ci)
            xr_s[rows, cs] = cr
            xi_s[rows, cs] = ci
        cbr_s[:, cs] = cr
        cbi_s[:, cs] = ci

    def c_projection(s):
        cs = slice(s * SLAB_STATE, (s + 1) * SLAB_STATE)
        us = slice(s * SLAB_IN, (s + 1) * SLAB_IN)
        y = (jnp.dot(xr_s[:, cs].astype(BF16), cdr_ref[s], preferred_element_type=F32)
             - jnp.dot(xi_s[:, cs].astype(BF16), cdi_ref[s], preferred_element_type=F32))
        y_s[:, us] = jax.nn.gelu(y + d_ref[:, us] * u_ref[s])

    _emit_pipelined((b_projection, scan, c_projection), N_SLABS)

    xlr_ref[...] = cbr_s[...]
    xli_ref[...] = cbi_s[...]

    y = y_s[...]
    z = jnp.dot(y.astype(BF16), wg_ref[...], preferred_element_type=F32) + bg_ref[...]
    o = y * jax.nn.sigmoid(z)
    for s in range(N_SLABS):
        o_s[s] = o[:, s * SLAB_IN:(s + 1) * SLAB_IN]
    for n in range(half):
        for s in range(N_SLABS):
            o_ref[n, :, s * SLAB_IN:(s + 1) * SLAB_IN] = (
                o_s[s, pl.ds(n, tt, stride=half), :].astype(o_ref.dtype))


def _s5_seq(u_t, pairs, wb, cd, d, w_glu, b_glu, n_seq, seq_len, tt=128):
    assert 2 * n_seq == SUBLANES
    n_rows = n_seq * tt
    st = jax.ShapeDtypeStruct((SUBLANES, N_STATE), F32)
    y, xlr, xli = pl.pallas_call(
        _s5_seq_kernel,
        grid=(seq_len // tt,),
        in_specs=[pl.BlockSpec((N_SLABS, n_rows, SLAB_IN), lambda i: (0, i, 0))]
        + _s5_weight_specs(2 * SLAB_IN),
        out_specs=(pl.BlockSpec((n_seq, tt, S5_WIDTH), lambda i: (0, i, 0)),
                   pl.BlockSpec((SUBLANES, N_STATE), lambda i: (0, 0)),
                   pl.BlockSpec((SUBLANES, N_STATE), lambda i: (0, 0))),
        out_shape=(jax.ShapeDtypeStruct((n_seq, seq_len, S5_WIDTH), BF16), st, st),
        scratch_shapes=[pltpu.VMEM((n_rows, N_STATE), F32), pltpu.VMEM((n_rows, N_STATE), F32),
                        pltpu.VMEM((n_rows, S5_WIDTH), F32),
                        pltpu.VMEM((N_SLABS, n_rows, SLAB_IN), F32),
                        pltpu.VMEM((SUBLANES, N_STATE), F32), pltpu.VMEM((SUBLANES, N_STATE), F32),
                        pltpu.VMEM((N_SLABS, SUBLANES, SLAB_IN), F32)],
        compiler_params=_params("arbitrary"),
        name="s5_seq",
    )(u_t, pairs[0], pairs[1], wb[0], wb[1], cd[0], cd[1], d, w_glu, b_glu)
    return y.reshape(n_seq * seq_len, S5_WIDTH), xlr[n_seq:], xli[n_seq:]


_NT = (((1,), (1,)), ((), ()))
_TN = (((0,), (0,)), ((), ()))


def _norm_gate(o, g, gn_w):
    mu = jnp.mean(o, axis=-1, keepdims=True)
    oc = o - mu
    var = jnp.mean(oc * oc, axis=-1, keepdims=True)
    return jax.nn.silu(g) * (oc * lax.rsqrt(var + NORM_EPS) * gn_w)


def _ret_seq_kernel(q_ref, k_ref, v_ref, g_ref, mask_ref, qd_ref, kd_ref, cd_ref,
                    gnw_ref, o_ref, rn_ref, r_s, sc_s, kv_s, *, chunks):
    @pl.when(pl.program_id(1) == 0)
    def _():
        r_s[...] = jnp.zeros_like(r_s)

    for c in range(chunks):
        rows = slice(c * RET_TILE, (c + 1) * RET_TILE)
        for h in range(RET_HEADS):
            hs = slice(h * HEAD_DIM, (h + 1) * HEAD_DIM)
            qb, kb, vb = q_ref[rows, hs], k_ref[rows, hs], v_ref[rows, hs]
            sc = lax.dot_general(qb, kb, _NT, preferred_element_type=F32) * mask_ref[h]
            sc_s[c, h] = sc.astype(BF16)
            k_dec = (kb.astype(F32) * kd_ref[h]).astype(BF16)
            kv_s[c, h] = lax.dot_general(k_dec, vb, _TN, preferred_element_type=F32)

    for c in range(chunks):
        rows = slice(c * RET_TILE, (c + 1) * RET_TILE)
        for h in range(RET_HEADS):
            hs = slice(h * HEAD_DIM, (h + 1) * HEAD_DIM)
            qb, vb = q_ref[rows, hs], v_ref[rows, hs]
            r_old = r_s[h]
            q_dec = (qb.astype(F32) * qd_ref[h]).astype(BF16)
            lhs = jnp.concatenate([sc_s[c, h], q_dec], axis=1)
            rhs = jnp.concatenate([vb, r_old.astype(BF16)], axis=0)
            o = jnp.dot(lhs, rhs, preferred_element_type=F32)
            r_s[h] = r_old * cd_ref[h] + kv_s[c, h]
            o_ref[rows, hs] = _norm_gate(o, g_ref[rows, hs].astype(F32),
                                         gnw_ref[:, hs]).astype(o_ref.dtype)
    rn_ref[0] = r_s[...]


def _ret_step_kernel(q_ref, k_ref, v_ref, g_ref, mask_ref, qd_ref, kd_ref, cd_ref,
                     gnw_ref, r0_ref, o_ref, rn_ref, qr_s, kdt_s, ob_s, *, n_seq):
    rows_per_seq = RET_TILE // n_seq
    for h in range(RET_HEADS):
        hs = slice(h * HEAD_DIM, (h + 1) * HEAD_DIM)
        qb, kb = q_ref[:, hs], k_ref[:, hs]
        sc = lax.dot_general(qb, kb, _NT, preferred_element_type=F32) * mask_ref[h]
        ob_s[:, hs] = jnp.dot(sc.astype(BF16), v_ref[:, hs], preferred_element_type=F32)
        qr_s[:, hs] = qb.astype(F32) * qd_ref[h]
        kdt_s[h] = (kb.astype(F32) * kd_ref[h]).T

    lane = lax.broadcasted_iota(jnp.int32, (HEAD_DIM, RET_TILE), 1)
    for s in range(n_seq):
        rows = slice(s * rows_per_seq, (s + 1) * rows_per_seq)
        in_seq = (lane >= s * rows_per_seq) & (lane < (s + 1) * rows_per_seq)
        for h in range(RET_HEADS):
            hs = slice(h * HEAD_DIM, (h + 1) * HEAD_DIM)
            r_old = r0_ref[s, h]
            ob_s[rows, hs] += jnp.dot(qr_s[rows, hs].astype(BF16), r_old.astype(BF16),
                                      preferred_element_type=F32)
            kdt = jnp.where(in_seq, kdt_s[h], 0.0).astype(BF16)
            rn_ref[s, h] = r_old * cd_ref[h] + jnp.dot(kdt, v_ref[:, hs], preferred_element_type=F32)

    for h in range(RET_HEADS):
        hs = slice(h * HEAD_DIM, (h + 1) * HEAD_DIM)
        o_ref[:, hs] = _norm_gate(ob_s[:, hs], g_ref[:, hs].astype(F32),
                                  gnw_ref[:, hs]).astype(o_ref.dtype)


def _decay_tables(chunk, n_seq):
    scale = HEAD_DIM ** -0.5
    log_gamma = np.log(1.0 - 2.0 ** (-5.0 - np.arange(RET_HEADS, dtype=np.float64)))
    idx = np.arange(chunk, dtype=np.float64)
    diff = idx[:, None] - idx[None, :]
    mask = np.where(diff >= 0, np.exp(log_gamma[:, None, None] * np.maximum(diff, 0.0)), 0.0)
    q_decay = np.exp(log_gamma[:, None] * (idx + 1.0))
    k_decay = np.exp(log_gamma[:, None] * (chunk - 1.0 - idx)) * scale
    chunk_decay = np.exp(log_gamma * chunk)
    mask_t = scale * np.einsum("hab,st->hsatb", mask, np.eye(n_seq)).reshape(
        RET_HEADS, RET_TILE, RET_TILE)
    qd_t = np.broadcast_to(np.tile(q_decay, (1, n_seq))[:, :, None], (RET_HEADS, RET_TILE, HEAD_DIM))
    kd_t = np.broadcast_to(np.tile(k_decay, (1, n_seq))[:, :, None], (RET_HEADS, RET_TILE, HEAD_DIM))
    const = lambda a: jnp.asarray(np.ascontiguousarray(a), dtype=F32)
    return const(mask_t), const(qd_t), const(kd_t), const(chunk_decay)


def _retention(proj, r0, gn_w, n_seq, seq_len, chunks=8):
    first, t = proj.shape[0] - 4, proj.shape[1]
    const3 = lambda *_: (0, 0, 0)
    if r0 is None:
        tile, tile_seqs = chunks * RET_TILE, 1
        steps = seq_len // tile
        grid = (n_seq, steps)
        row = lambda n, c: n * steps + c
        state_map = lambda n, c: (n, 0, 0, 0)
        state_block = (1, RET_HEADS, HEAD_DIM, HEAD_DIM)
        sem = ("arbitrary", "arbitrary")
        body = functools.partial(_ret_seq_kernel, chunks=chunks)
        scratch = [pltpu.VMEM((RET_HEADS, HEAD_DIM, HEAD_DIM), F32),
                   pltpu.VMEM((chunks, RET_HEADS, RET_TILE, RET_TILE), BF16),
                   pltpu.VMEM((chunks, RET_HEADS, HEAD_DIM, HEAD_DIM), F32)]
        name = "ret_seq"
    else:
        tile, tile_seqs = RET_TILE, RET_TILE // seq_len
        grid = (t // tile,)
        row = lambda i: i
        state_map = lambda i: (i, 0, 0, 0)
        state_block = (tile_seqs, RET_HEADS, HEAD_DIM, HEAD_DIM)
        sem = ("arbitrary",)
        body = functools.partial(_ret_step_kernel, n_seq=tile_seqs)
        scratch = [pltpu.VMEM((RET_TILE, RET_WIDTH), F32),
                   pltpu.VMEM((RET_HEADS, HEAD_DIM, RET_TILE), F32),
                   pltpu.VMEM((RET_TILE, RET_WIDTH), F32)]
        name = "ret_step"
    mask_t, qd_t, kd_t, cd = _decay_tables(RET_TILE // tile_seqs, tile_seqs)

    def col(cb):
        return pl.BlockSpec((None, tile, RET_WIDTH), lambda *a: (first + cb, row(*a), 0))

    in_specs = [
        col(0), col(1), col(2), col(3),
        pl.BlockSpec((RET_HEADS, RET_TILE, RET_TILE), const3),
        pl.BlockSpec((RET_HEADS, RET_TILE, HEAD_DIM), const3),
        pl.BlockSpec((RET_HEADS, RET_TILE, HEAD_DIM), const3),
        pl.BlockSpec(memory_space=pltpu.SMEM),
        pl.BlockSpec((1, RET_WIDTH), lambda *_: (0, 0)),
    ]
    args = [proj, proj, proj, proj, mask_t, qd_t, kd_t, cd, gn_w]
    if r0 is not None:
        in_specs.append(pl.BlockSpec(state_block, state_map))
        args.append(r0)
    return pl.pallas_call(
        body,
        grid=grid,
        in_specs=in_specs,
        out_specs=(pl.BlockSpec((tile, RET_WIDTH), lambda *a: (row(*a), 0)),
                   pl.BlockSpec(state_block, state_map)),
        out_shape=(jax.ShapeDtypeStruct((t, RET_WIDTH), BF16),
                   jax.ShapeDtypeStruct((n_seq, RET_HEADS, HEAD_DIM, HEAD_DIM), F32)),
        scratch_shapes=scratch,
        compiler_params=_params(*sem),
        name=name,
    )(*args)


def _outproj_kernel(x_ref, a_ref, b_ref, wa_ref, wb_ref, o_ref):
    o_ref[...] = (x_ref[...]
                  + jnp.dot(a_ref[...], wa_ref[...], preferred_element_type=F32)
                  + jnp.dot(b_ref[...], wb_ref[...], preferred_element_type=F32))


def _outproj(x2d, a, b, w, tm=512):
    t = x2d.shape[0]
    resident = pl.Buffered(1)
    return pl.pallas_call(
        _outproj_kernel,
        grid=(t // tm,),
        in_specs=[
            pl.BlockSpec((tm, D_MODEL), lambda i: (i, 0)),
            pl.BlockSpec((tm, S5_WIDTH), lambda i: (i, 0)),
            pl.BlockSpec((tm, RET_WIDTH), lambda i: (i, 0)),
            pl.BlockSpec((S5_WIDTH, D_MODEL), lambda i: (0, 0), pipeline_mode=resident),
            pl.BlockSpec((RET_WIDTH, D_MODEL), lambda i: (1, 0), pipeline_mode=resident),
        ],
        out_specs=pl.BlockSpec((tm, D_MODEL), lambda i: (i, 0)),
        out_shape=jax.ShapeDtypeStruct((t, D_MODEL), F32),
        compiler_params=_params("arbitrary"),
        name="out_proj",
    )(x2d, a, b, w, w)


def _ffn_kernel(x_ref, gn_ref, wg_ref, wu_ref, wo_ref, gf_ref, o_ref, h_scr):
    j = pl.program_id(1)
    last_j = pl.num_programs(1) - 1

    def step(first, last):
        for r in range(x_ref.shape[0] // FFN_ROWS):
            rows = slice(r * FFN_ROWS, (r + 1) * FFN_ROWS)
            if first:
                h_scr[rows, :] = _rms(x_ref[rows, :], gn_ref[...]).astype(BF16)
            h = h_scr[rows, :]
            gate = jnp.dot(h, wg_ref[...], preferred_element_type=F32)
            up = jnp.dot(h, wu_ref[...], preferred_element_type=F32)
            act = (jax.nn.silu(gate) * up).astype(BF16)
            acc = (x_ref if first else o_ref)[rows, :] + jnp.dot(act, wo_ref[...],
                                                                  preferred_element_type=F32)
            o_ref[rows, :] = _rms(acc, gf_ref[...]) if last else acc

    pl.when(j == 0)(lambda: step(True, False))
    pl.when((j > 0) & (j < last_j))(lambda: step(False, False))
    pl.when(j == last_j)(lambda: step(False, True))


def _ffn(x2d, g_ffn, w_in, w_out, g_final, tm=1024, tf=512):
    t = x2d.shape[0]
    nf = D_FF // tf
    return pl.pallas_call(
        _ffn_kernel,
        grid=(t // tm, nf),
        in_specs=[
            pl.BlockSpec((tm, D_MODEL), lambda i, j: (i, 0)),
            pl.BlockSpec((1, D_MODEL), lambda i, j: (0, 0)),
            pl.BlockSpec((D_MODEL, tf), lambda i, j: (0, j)),
            pl.BlockSpec((D_MODEL, tf), lambda i, j: (0, j + nf)),
            pl.BlockSpec((tf, D_MODEL), lambda i, j: (j, 0)),
            pl.BlockSpec((1, D_MODEL), lambda i, j: (0, 0)),
        ],
        out_specs=pl.BlockSpec((tm, D_MODEL), lambda i, j: (i, 0)),
        out_shape=jax.ShapeDtypeStruct((t, D_MODEL), F32),
        scratch_shapes=[pltpu.VMEM((tm, D_MODEL), BF16)],
        compiler_params=_params("arbitrary", "arbitrary"),
        name="ffn",
    )(x2d, g_ffn, w_in, w_in, w_out, g_final)


def _finish(x2d, proj, s5, ret_state, w, n, l):
    s5_out, s5_re, s5_im = s5
    ret_out, ret_new = _retention(proj, ret_state, w["gn_w"], n, l)
    x1 = _outproj(x2d, s5_out, ret_out, w["w_out"])
    y = _ffn(x1, w["norm_ffn"], w["w_ffn_in"], w["w_ffn_out"], w["norm_final"])
    return (y.reshape(n, l, D_MODEL), s5_re.reshape(n, S5_GROUPS, S5_STATE),
            s5_im.reshape(n, S5_GROUPS, S5_STATE), ret_new)


def kernel(x_prompt, x_sample, state_s5_re, state_s5_im, state_ret, norm_mix, w_in, s5_lambda_re, s5_lambda_im, s5_log_step, s5_b_re, s5_b_im, s5_c_re, s5_c_im, s5_d, s5_w_glu, s5_b_glu, ret_gn_w, w_out, norm_ffn, w_ffn_in, w_ffn_out, norm_final):
    assert norm_mix.shape[0] == 1, "single-layer stack"
    pairs, wb, cd = _s5_params(s5_lambda_re[0], s5_lambda_im[0], s5_log_step[0],
                               s5_b_re[0], s5_b_im[0], s5_c_re[0], s5_c_im[0])
    w = dict(gn_w=ret_gn_w, norm_ffn=norm_ffn, norm_final=norm_final.reshape(1, D_MODEL))
    s5_w = (pairs, wb, cd, s5_d)
    n_s, l_s, _ = x_sample.shape
    n_p, l_p, _ = x_prompt.shape
    xs2d = x_sample.reshape(n_s * l_s, D_MODEL)
    xp2d = x_prompt.reshape(n_p * l_p, D_MODEL)

    proj_s, w_in_b, u_s = _inproj_cast(xs2d, norm_mix, w_in[0], l_s, float(PAST_LEN))
    proj_p, u_p, w["w_out"], w["w_ffn_in"], w["w_ffn_out"] = _inproj_seq(
        xp2d, norm_mix, w_in_b, n_p, 0.0, (w_out[0], w_ffn_in[0], w_ffn_out[0]))

    x0 = (state_s5_re[0].reshape(n_s, N_STATE), state_s5_im[0].reshape(n_s, N_STATE))
    *s5_s, w_glu_b = _s5_step(u_s, x0, *s5_w, s5_w_glu[0], s5_b_glu)
    ys, s_re, s_im, s_ret = _finish(xs2d, proj_s, s5_s, state_ret[0], w, n_s, l_s)

    s5_p = _s5_seq(u_p, *s5_w, w_glu_b, s5_b_glu, n_p, l_p)
    yp, p_re, p_im, p_ret = _finish(xp2d, proj_p, s5_p, None, w, n_p, l_p)
    return (yp, ys, p_re[None], p_im[None], p_ret[None], s_re[None], s_im[None], s_ret[None])
```

```python
import functools

import jax
import jax.numpy as jnp
import numpy as np
from jax import lax
from jax.experimental import pallas as pl
from jax.experimental.pallas import tpu as pltpu

F32 = jnp.float32
BF16 = jnp.bfloat16

D_MODEL = 2048
S5_WIDTH = 1024
S5_GROUP = 16
S5_GROUPS = 64
S5_STATE = 64
N_STATE = S5_GROUPS * S5_STATE
RET_WIDTH = 1024
RET_HEADS = 8
HEAD_DIM = 128
ROPE_BASE = 10000.0
D_FF = 5632
IN_WIDTH = S5_WIDTH + 4 * RET_WIDTH
NORM_EPS = 1e-6
PAST_LEN = 16384

SUBLANES = 8
SLAB_GROUPS = 8
N_SLABS = S5_GROUPS // SLAB_GROUPS
SLAB_IN = SLAB_GROUPS * S5_GROUP
SLAB_STATE = SLAB_GROUPS * S5_STATE
RET_TILE = 128
STATE_RING = 3
STATE_DMA_PRIORITY = 1
ROTARY_TILES = (1, 2)
FFN_ROWS = 512

VMEM_LIMIT_BYTES = 60 * 1024 * 1024


def _params(*sem):
    return pltpu.CompilerParams(dimension_semantics=sem, vmem_limit_bytes=VMEM_LIMIT_BYTES)


def _rms(xf, g):
    ms = jnp.mean(xf * xf, axis=-1, keepdims=True)
    return xf * lax.rsqrt(ms + NORM_EPS) * g


def _rotary(x, cos2, sin2):
    return x * cos2 + pltpu.roll(x, HEAD_DIM // 2, 1) * sin2


def _rotary_heads(x, cos2, sin2):
    return jnp.concatenate(
        [_rotary(x[:, h * HEAD_DIM:(h + 1) * HEAD_DIM], cos2, sin2) for h in range(RET_HEADS)],
        axis=1)


def _rotary_tables(pos0, rows, reps):
    half = HEAD_DIM // 2
    inv_freq = ROPE_BASE ** (-np.arange(half, dtype=np.float64) / half)
    pos = pos0 + np.arange(rows, dtype=np.float64)
    ang = pos[:, None] * inv_freq[None, :]
    cos, sin = np.cos(ang), np.sin(ang)
    cos2 = np.concatenate([cos, cos], axis=-1)
    sin2 = np.concatenate([-sin, sin], axis=-1)
    return (jnp.asarray(np.tile(cos2, (reps, 1)), dtype=F32),
            jnp.asarray(np.tile(sin2, (reps, 1)), dtype=F32))


def _inproj_seq_kernel(x_ref, g_ref, cos_ref, sin_ref, w_ref, *rest, tn, n_seq):
    n_cast = (len(rest) - 3) // 2
    cast_in, (o_ref, ou_ref), cast_out = rest[:n_cast], rest[n_cast:n_cast + 2], rest[n_cast + 2:-1]
    h_scr = rest[-1]
    for src, dst in zip(cast_in, cast_out):
        dst[...] = src[...].astype(BF16)
    seq = pl.program_id(1)
    tm = x_ref.shape[0]
    h_scr[...] = _rms(x_ref[...], g_ref[...]).astype(BF16)
    for j in range(IN_WIDTH // tn):
        cols = slice(j * tn, (j + 1) * tn)
        res = jnp.dot(h_scr[...], w_ref[:, cols], preferred_element_type=F32)
        if j == 0:
            dst = pl.ds(seq, tm, stride=n_seq)
            for s in range(N_SLABS):
                ou_ref[s, dst, :] = res[:, s * SLAB_IN:(s + 1) * SLAB_IN]
            continue
        if j in ROTARY_TILES:
            res = _rotary_heads(res, cos_ref[...], sin_ref[...])
        o_ref[j - 1] = res.astype(o_ref.dtype)


def _inproj_seq(x2d, g, w, n_seq, pos0, to_cast, tm=256, tn=RET_WIDTH):
    assert tn == S5_WIDTH == RET_WIDTH
    t = x2d.shape[0]
    tiles = t // n_seq // tm
    steps = tiles * n_seq
    row = lambda i, n: (n * tiles + i, 0)
    const = lambda i, n: (0, 0)
    cos2, sin2 = _rotary_tables(pos0, t // n_seq, 1)
    cast_specs = [pl.BlockSpec((m.shape[0] // steps, m.shape[1]), lambda i, n: (i * n_seq + n, 0))
                  for m in to_cast]
    assert all(m.shape[0] % (16 * steps) == 0 for m in to_cast)
    return pl.pallas_call(
        functools.partial(_inproj_seq_kernel, tn=tn, n_seq=n_seq),
        grid=(tiles, n_seq),
        in_specs=[
            pl.BlockSpec((tm, D_MODEL), row),
            pl.BlockSpec((1, D_MODEL), const),
            pl.BlockSpec((tm, HEAD_DIM), lambda i, n: (i, 0)),
            pl.BlockSpec((tm, HEAD_DIM), lambda i, n: (i, 0)),
            pl.BlockSpec((D_MODEL, IN_WIDTH), const, pipeline_mode=pl.Buffered(1)),
        ] + cast_specs,
        out_specs=[pl.BlockSpec((4, tm, RET_WIDTH), lambda i, n: (0, n * tiles + i, 0)),
                   pl.BlockSpec((N_SLABS, n_seq * tm, SLAB_IN), lambda i, n: (0, i, 0))]
        + cast_specs,
        out_shape=[jax.ShapeDtypeStruct((4, t, RET_WIDTH), BF16),
                   jax.ShapeDtypeStruct((N_SLABS, t, SLAB_IN), F32)]
        + [jax.ShapeDtypeStruct(m.shape, BF16) for m in to_cast],
        scratch_shapes=[pltpu.VMEM((tm, D_MODEL), BF16)],
        compiler_params=_params("arbitrary", "arbitrary"),
        name="in_proj_seq",
    )(x2d, g, cos2, sin2, w, *to_cast)


def _inproj_cast_kernel(x_ref, g_ref, cos_ref, sin_ref, w_ref, o_ref, wb_ref, ou_ref, h_scr, ru_s):
    j = pl.program_id(0)

    @pl.when(j == 0)
    def _():
        h_scr[...] = _rms(x_ref[...], g_ref[...]).astype(BF16)

    wb = w_ref[...].astype(BF16)
    wb_ref[...] = wb
    is_rotary = functools.reduce(jnp.logical_or, [j == r for r in ROTARY_TILES])
    for r in range(x_ref.shape[0] // FFN_ROWS):
        rows = slice(r * FFN_ROWS, (r + 1) * FFN_ROWS)
        res = jnp.dot(h_scr[rows, :], wb, preferred_element_type=F32)
        rot = _rotary_heads(res, cos_ref[rows, :], sin_ref[rows, :])
        o_ref[rows, :] = jnp.where(is_rotary, rot, res).astype(o_ref.dtype)
        for s in range(N_SLABS):
            ru_s[s, rows, :] = res[:, s * SLAB_IN:(s + 1) * SLAB_IN]

    @pl.when(j == 0)
    def _():
        seq_len, n_seq = ou_ref.shape[1], ou_ref.shape[2]
        for s in range(N_SLABS):
            for t in range(seq_len):
                ou_ref[s, t] = ru_s[s, pl.ds(t, n_seq, stride=seq_len), :]


def _inproj_cast(x2d, g, w_f32, seq_len, pos0):
    t = x2d.shape[0]
    tn = RET_WIDTH
    cos2, sin2 = _rotary_tables(pos0, seq_len, t // seq_len)
    return pl.pallas_call(
        _inproj_cast_kernel,
        grid=(IN_WIDTH // tn,),
        in_specs=[
            pl.BlockSpec((t, D_MODEL), lambda j: (0, 0), pipeline_mode=pl.Buffered(1)),
            pl.BlockSpec((1, D_MODEL), lambda j: (0, 0)),
            pl.BlockSpec((t, HEAD_DIM), lambda j: (0, 0)),
            pl.BlockSpec((t, HEAD_DIM), lambda j: (0, 0)),
            pl.BlockSpec((D_MODEL, tn), lambda j: (0, j)),
        ],
        out_specs=[pl.BlockSpec((None, t, tn), lambda j: (j, 0, 0)),
                   pl.BlockSpec((D_MODEL, tn), lambda j: (0, j)),
                   pl.BlockSpec((N_SLABS, seq_len, t // seq_len, SLAB_IN), lambda j: (0, 0, 0, 0))],
        out_shape=[jax.ShapeDtypeStruct((IN_WIDTH // tn, t, tn), BF16),
                   jax.ShapeDtypeStruct((D_MODEL, IN_WIDTH), BF16),
                   jax.ShapeDtypeStruct((N_SLABS, seq_len, t // seq_len, SLAB_IN), F32)],
        scratch_shapes=[pltpu.VMEM((t, D_MODEL), BF16), pltpu.VMEM((N_SLABS, t, SLAB_IN), F32)],
        compiler_params=_params("arbitrary"),
        name="in_proj_cast",
    )(x2d, g, cos2, sin2, w_f32)


def _s5_param_kernel(lr_ref, li_ref, ls_ref, br_ref, bi_ref, ctr_ref, cti_ref,
                     pairr_ref, pairi_ref, wbr_ref, wbi_ref, cdr_ref, cdi_ref):
    lr = lr_ref[...]
    li = li_ref[...]
    dt = jnp.exp(ls_ref[...])
    mag = jnp.exp(lr * dt)
    ar = mag * jnp.cos(li * dt)
    ai = mag * jnp.sin(li * dt)
    den = lr * lr + li * li
    nr = ar - 1.0
    cr = (nr * lr + ai * li) / den
    ci = (ai * lr - nr * li) / den
    br = br_ref[...]
    bi = bi_ref[...]
    bbr = cr * br - ci * bi
    bbi = cr * bi + ci * br
    abr = ar * bbr - ai * bbi
    abi = ar * bbi + ai * bbr

    in_shape = (SLAB_IN, SLAB_STATE)
    same_in = (lax.broadcasted_iota(jnp.int32, in_shape, 0) // S5_GROUP
               == lax.broadcasted_iota(jnp.int32, in_shape, 1) // S5_STATE)
    out_shape = (SLAB_STATE, SLAB_IN)
    same_out = (lax.broadcasted_iota(jnp.int32, out_shape, 0) // S5_STATE
                == lax.broadcasted_iota(jnp.int32, out_shape, 1) // S5_GROUP)
    for s in range(N_SLABS):
        cs = slice(s * SLAB_STATE, (s + 1) * SLAB_STATE)
        for dst, top, bot in ((wbr_ref, bbr, abr), (wbi_ref, bbi, abi)):
            for k, part in enumerate((top, bot)):
                blk = jnp.tile(part[:, cs], (SLAB_GROUPS, 1))
                dst[s, k * SLAB_IN:(k + 1) * SLAB_IN, :] = jnp.where(same_in, blk, 0.0).astype(BF16)
        cdr_ref[s] = jnp.where(same_out, ctr_ref[cs, :], 0.0).astype(BF16)
        cdi_ref[s] = jnp.where(same_out, cti_ref[cs, :], 0.0).astype(BF16)

    rows = lax.broadcasted_iota(jnp.int32, (SUBLANES, N_STATE), 0)
    second = rows >= SUBLANES // 2
    full = lambda v: jnp.broadcast_to(v, (SUBLANES, N_STATE))
    pairr_ref[...] = jnp.where(second, full(ar * ar - ai * ai), full(ar))
    pairi_ref[...] = jnp.where(second, full(ar * ai + ai * ar), full(ai))


def _s5_params(lam_re, lam_im, log_step, b_re, b_im, c_re, c_im):
    lr = lam_re.reshape(1, N_STATE)
    li = lam_im.reshape(1, N_STATE)
    ls = jnp.repeat(log_step, S5_STATE).reshape(1, N_STATE)
    b_rows = lambda b: jnp.transpose(b, (2, 0, 1)).reshape(S5_GROUP, N_STATE)
    c_cols = lambda c: jnp.tile(jnp.transpose(c, (0, 2, 1)).reshape(N_STATE, S5_GROUP),
                                (1, SLAB_GROUPS))
    pair = jax.ShapeDtypeStruct((SUBLANES, N_STATE), F32)
    wb = jax.ShapeDtypeStruct((N_SLABS, 2 * SLAB_IN, SLAB_STATE), BF16)
    cd = jax.ShapeDtypeStruct((N_SLABS, SLAB_STATE, SLAB_IN), BF16)
    pairr, pairi, wbr, wbi, cdr, cdi = pl.pallas_call(
        _s5_param_kernel,
        out_shape=(pair, pair, wb, wb, cd, cd),
        name="s5_params",
    )(lr, li, ls, b_rows(b_re), b_rows(b_im), c_cols(c_re), c_cols(c_im))
    return (pairr, pairi), (wbr, wbi), (cdr, cdi)


def _emit_pipelined(stages, n):
    for step in range(n + len(stages) - 1):
        for lag, stage in enumerate(stages):
            if 0 <= step - lag < n:
                stage(step - lag)


def _cmul_add(xr, xi, pr, pi, vr, vi):
    return xr + (pr * vr - pi * vi), xi + (pr * vi + pi * vr)


def _s5_weight_specs(k_in):
    const3 = lambda *_: (0, 0, 0)
    const2 = lambda *_: (0, 0)
    return [
        pl.BlockSpec((SUBLANES, N_STATE), const2),
        pl.BlockSpec((SUBLANES, N_STATE), const2),
        pl.BlockSpec((N_SLABS, k_in, SLAB_STATE), const3),
        pl.BlockSpec((N_SLABS, k_in, SLAB_STATE), const3),
        pl.BlockSpec((N_SLABS, SLAB_STATE, SLAB_IN), const3),
        pl.BlockSpec((N_SLABS, SLAB_STATE, SLAB_IN), const3),
        pl.BlockSpec((1, S5_WIDTH), const2),
        pl.BlockSpec((S5_WIDTH, S5_WIDTH), const2),
        pl.BlockSpec((1, S5_WIDTH), const2),
    ]


def _s5_step_kernel(u_ref, x0r_ref, x0i_ref, pairr_ref, pairi_ref, wbr_ref, wbi_ref, cdr_ref, cdi_ref,
                    d_ref, wg_ref, bg_ref, o_ref, xlr_ref, xli_ref, wgb_ref, xr_s, xi_s, y_s, o_s):
    seq_len, n_seq = u_ref.shape[1], u_ref.shape[2]

    @pl.when(pl.program_id(0) == 0)
    def _():
        wgb_ref[...] = wg_ref[...].astype(BF16)

    def b_projection(s):
        cs = slice(s * SLAB_STATE, (s + 1) * SLAB_STATE)
        ub = u_ref[s].reshape(seq_len * n_seq, SLAB_IN).astype(BF16)
        xr_s[:, cs] = jnp.dot(ub, wbr_ref[s], preferred_element_type=F32)
        xi_s[:, cs] = jnp.dot(ub, wbi_ref[s], preferred_element_type=F32)

    def scan(s):
        cs = slice(s * SLAB_STATE, (s + 1) * SLAB_STATE)
        a_r = jnp.broadcast_to(pairr_ref[0:1, cs], (SUBLANES, SLAB_STATE))
        a_i = jnp.broadcast_to(pairi_ref[0:1, cs], (SUBLANES, SLAB_STATE))
        for b in range(n_seq // SUBLANES):
            seqs = slice(b * SUBLANES, (b + 1) * SUBLANES)
            xr, xi = x0r_ref[seqs, cs], x0i_ref[seqs, cs]
            for t in range(seq_len):
                rows = slice(t * n_seq + b * SUBLANES, t * n_seq + (b + 1) * SUBLANES)
                xr, xi = _cmul_add(xr_s[rows, cs], xi_s[rows, cs], a_r, a_i, xr, xi)
                xr_s[rows, cs] = xr
                xi_s[rows, cs] = xi
            xlr_ref[seqs, cs] = xr
            xli_ref[seqs, cs] = xi

    def c_projection(s):
        cs = slice(s * SLAB_STATE, (s + 1) * SLAB_STATE)
        us = slice(s * SLAB_IN, (s + 1) * SLAB_IN)
        u = u_ref[s].reshape(seq_len * n_seq, SLAB_IN)
        y = (jnp.dot(xr_s[:, cs].astype(BF16), cdr_ref[s], preferred_element_type=F32)
             - jnp.dot(xi_s[:, cs].astype(BF16), cdi_ref[s], preferred_element_type=F32))
        y_s[:, us] = jax.nn.gelu(y + d_ref[:, us] * u)

    _emit_pipelined((b_projection, scan, c_projection), N_SLABS)

    y = y_s[...]
    z = jnp.dot(y.astype(BF16), wgb_ref[...], preferred_element_type=F32) + bg_ref[...]
    o = y * jax.nn.sigmoid(z)
    for s in range(N_SLABS):
        for t in range(seq_len):
            o_s[s, pl.ds(t, n_seq, stride=seq_len), :] = (
                o[t * n_seq:(t + 1) * n_seq, s * SLAB_IN:(s + 1) * SLAB_IN])
    for s in range(N_SLABS):
        o_ref[:, s * SLAB_IN:(s + 1) * SLAB_IN] = o_s[s].astype(o_ref.dtype)


def _s5_step(u_t, x0, pairs, wb, cd, d, w_glu, b_glu, n_tile=32):
    _, seq_len, n_seq, _ = u_t.shape
    rows = n_tile * seq_len
    row = lambda i: (i, 0)
    st = jax.ShapeDtypeStruct((n_seq, N_STATE), F32)
    return pl.pallas_call(
        _s5_step_kernel,
        grid=(n_seq // n_tile,),
        in_specs=[pl.BlockSpec((N_SLABS, seq_len, n_tile, SLAB_IN), lambda i: (0, 0, i, 0)),
                  pl.BlockSpec((n_tile, N_STATE), row), pl.BlockSpec((n_tile, N_STATE), row)]
        + _s5_weight_specs(SLAB_IN),
        out_specs=(pl.BlockSpec((rows, S5_WIDTH), row), pl.BlockSpec((n_tile, N_STATE), row),
                   pl.BlockSpec((n_tile, N_STATE), row),
                   pl.BlockSpec((S5_WIDTH, S5_WIDTH), lambda i: (0, 0))),
        out_shape=(jax.ShapeDtypeStruct((n_seq * seq_len, S5_WIDTH), BF16), st, st,
                   jax.ShapeDtypeStruct((S5_WIDTH, S5_WIDTH), BF16)),
        scratch_shapes=[pltpu.VMEM((rows, N_STATE), F32), pltpu.VMEM((rows, N_STATE), F32),
                        pltpu.VMEM((rows, S5_WIDTH), F32),
                        pltpu.VMEM((N_SLABS, rows, SLAB_IN), F32)],
        compiler_params=_params("arbitrary"),
        name="s5_step",
    )(u_t, x0[0], x0[1], pairs[0], pairs[1], wb[0], wb[1], cd[0], cd[1], d, w_glu, b_glu)


def _s5_seq_kernel(u_ref, pairr_ref, pairi_ref, wbr_ref, wbi_ref, cdr_ref, cdi_ref, d_ref, wg_ref,
                   bg_ref, o_ref, xlr_ref, xli_ref, xr_s, xi_s, y_s, o_s, cbr_s, cbi_s, ul_s):
    n_rows = u_ref.shape[1]
    half = SUBLANES // 2
    tt = n_rows // half

    @pl.when(pl.program_id(0) == 0)
    def _():
        cbr_s[...] = jnp.zeros_like(cbr_s)
        cbi_s[...] = jnp.zeros_like(cbi_s)
        ul_s[...] = jnp.zeros_like(ul_s)

    first = lax.broadcasted_iota(jnp.int32, (SUBLANES, SLAB_IN), 0) < half

    def b_projection(s):
        cs = slice(s * SLAB_STATE, (s + 1) * SLAB_STATE)
        u = u_ref[s]
        shifted = pltpu.roll(u, half, 0)
        head = jnp.where(first, pltpu.roll(ul_s[s], half, 0), shifted[:SUBLANES])
        u_prev = jnp.concatenate([head, shifted[SUBLANES:]], axis=0)
        ul_s[s] = u[n_rows - SUBLANES:]
        ub = jnp.concatenate([u, u_prev], axis=1).astype(BF16)
        xr_s[:, cs] = jnp.dot(ub, wbr_ref[s], preferred_element_type=F32)
        xi_s[:, cs] = jnp.dot(ub, wbi_ref[s], preferred_element_type=F32)

    def scan(s):
        cs = slice(s * SLAB_STATE, (s + 1) * SLAB_STATE)
        pr = jnp.broadcast_to(pairr_ref[half:half + 1, cs], (SUBLANES, SLAB_STATE))
        pi = jnp.broadcast_to(pairi_ref[half:half + 1, cs], (SUBLANES, SLAB_STATE))
        cr, ci = cbr_s[:, cs], cbi_s[:, cs]
        for b in range(n_rows // SUBLANES):
            rows = slice(b * SUBLANES, (b + 1) * SUBLANES)
            cr, ci = _cmul_add(xr_s[rows, cs], xi_s[rows, cs], pr, pi, cr, ci)
            xr_s[rows, cs] = cr
            xi_s[rows, cs] = ci
        cbr_s[:, cs] = cr
        cbi_s[:, cs] = ci

    def c_projection(s):
        cs = slice(s * SLAB_STATE, (s + 1) * SLAB_STATE)
        us = slice(s * SLAB_IN, (s + 1) * SLAB_IN)
        y = (jnp.dot(xr_s[:, cs].astype(BF16), cdr_ref[s], preferred_element_type=F32)
             - jnp.dot(xi_s[:, cs].astype(BF16), cdi_ref[s], preferred_element_type=F32))
        y_s[:, us] = jax.nn.gelu(y + d_ref[:, us] * u_ref[s])

    _emit_pipelined((b_projection, scan, c_projection), N_SLABS)

    xlr_ref[...] = cbr_s[...]
    xli_ref[...] = cbi_s[...]

    y = y_s[...]
    z = jnp.dot(y.astype(BF16), wg_ref[...], preferred_element_type=F32) + bg_ref[...]
    o = y * jax.nn.sigmoid(z)
    for s in range(N_SLABS):
        o_s[s] = o[:, s * SLAB_IN:(s + 1) * SLAB_IN]
    for n in range(half):
        for s in range(N_SLABS):
            o_ref[n, :, s * SLAB_IN:(s + 1) * SLAB_IN] = (
                o_s[s, pl.ds(n, tt, stride=half), :].astype(o_ref.dtype))


def _s5_seq(u_t, pairs, wb, cd, d, w_glu, b_glu, n_seq, seq_len, tt=128):
    assert 2 * n_seq == SUBLANES
    n_rows = n_seq * tt
    st = jax.ShapeDtypeStruct((SUBLANES, N_STATE), F32)
    y, xlr, xli = pl.pallas_call(
        _s5_seq_kernel,
        grid=(seq_len // tt,),
        in_specs=[pl.BlockSpec((N_SLABS, n_rows, SLAB_IN), lambda i: (0, i, 0))]
        + _s5_weight_specs(2 * SLAB_IN),
        out_specs=(pl.BlockSpec((n_seq, tt, S5_WIDTH), lambda i: (0, i, 0)),
                   pl.BlockSpec((SUBLANES, N_STATE), lambda i: (0, 0)),
                   pl.BlockSpec((SUBLANES, N_STATE), lambda i: (0, 0))),
        out_shape=(jax.ShapeDtypeStruct((n_seq, seq_len, S5_WIDTH), BF16), st, st),
        scratch_shapes=[pltpu.VMEM((n_rows, N_STATE), F32), pltpu.VMEM((n_rows, N_STATE), F32),
                        pltpu.VMEM((n_rows, S5_WIDTH), F32),
                        pltpu.VMEM((N_SLABS, n_rows, SLAB_IN), F32),
                        pltpu.VMEM((SUBLANES, N_STATE), F32), pltpu.VMEM((SUBLANES, N_STATE), F32),
                        pltpu.VMEM((N_SLABS, SUBLANES, SLAB_IN), F32)],
        compiler_params=_params("arbitrary"),
        name="s5_seq",
    )(u_t, pairs[0], pairs[1], wb[0], wb[1], cd[0], cd[1], d, w_glu, b_glu)
    return y.reshape(n_seq * seq_len, S5_WIDTH), xlr[n_seq:], xli[n_seq:]


_NT = (((1,), (1,)), ((), ()))
_TN = (((0,), (0,)), ((), ()))


def _norm_gate(o, g, gn_w):
    mu = jnp.mean(o, axis=-1, keepdims=True)
    oc = o - mu
    var = jnp.mean(oc * oc, axis=-1, keepdims=True)
    return jax.nn.silu(g) * (oc * lax.rsqrt(var + NORM_EPS) * gn_w)


def _ret_seq_kernel(q_ref, k_ref, v_ref, g_ref, mask_ref, qd_ref, kd_ref, cd_ref,
                    gnw_ref, o_ref, rn_ref, r_s, sc_s, kv_s, *, chunks):
    @pl.when(pl.program_id(1) == 0)
    def _():
        r_s[...] = jnp.zeros_like(r_s)

    for c in range(chunks):
        rows = slice(c * RET_TILE, (c + 1) * RET_TILE)
        for h in range(RET_HEADS):
            hs = slice(h * HEAD_DIM, (h + 1) * HEAD_DIM)
            qb, kb, vb = q_ref[rows, hs], k_ref[rows, hs], v_ref[rows, hs]
            sc = lax.dot_general(qb, kb, _NT, preferred_element_type=F32) * mask_ref[h]
            sc_s[c, h] = sc.astype(BF16)
            k_dec = (kb.astype(F32) * kd_ref[h]).astype(BF16)
            kv_s[c, h] = lax.dot_general(k_dec, vb, _TN, preferred_element_type=F32)

    for c in range(chunks):
        rows = slice(c * RET_TILE, (c + 1) * RET_TILE)
        for h in range(RET_HEADS):
            hs = slice(h * HEAD_DIM, (h + 1) * HEAD_DIM)
            qb, vb = q_ref[rows, hs], v_ref[rows, hs]
            r_old = r_s[h]
            q_dec = (qb.astype(F32) * qd_ref[h]).astype(BF16)
            lhs = jnp.concatenate([sc_s[c, h], q_dec], axis=1)
            rhs = jnp.concatenate([vb, r_old.astype(BF16)], axis=0)
            o = jnp.dot(lhs, rhs, preferred_element_type=F32)
            r_s[h] = r_old * cd_ref[h] + kv_s[c, h]
            o_ref[rows, hs] = _norm_gate(o, g_ref[rows, hs].astype(F32),
                                         gnw_ref[:, hs]).astype(o_ref.dtype)
    rn_ref[0] = r_s[...]


def _ret_step_kernel(q_ref, k_ref, v_ref, g_ref, mask_ref, qd_ref, kd_ref, cd_ref,
                     gnw_ref, r0_hbm, o_ref, rn_ref, qr_s, kdt_s, ob_s, r0_buf, r0_sem, *, n_seq):
    rows_per_seq = RET_TILE // n_seq
    step, n_steps = pl.program_id(0), pl.num_programs(0)

    def state_copy(t):
        slot = t % STATE_RING
        return pltpu.make_async_copy(r0_hbm.at[pl.ds(t * n_seq, n_seq)], r0_buf.at[slot],
                                     r0_sem.at[slot])

    @pl.when(step == 0)
    def _():
        for t in range(STATE_RING - 1):
            state_copy(t).start(priority=STATE_DMA_PRIORITY)

    @pl.when(step + STATE_RING - 1 < n_steps)
    def _():
        state_copy(step + STATE_RING - 1).start(priority=STATE_DMA_PRIORITY)

    for h in range(RET_HEADS):
        hs = slice(h * HEAD_DIM, (h + 1) * HEAD_DIM)
        qb, kb = q_ref[:, hs], k_ref[:, hs]
        sc = lax.dot_general(qb, kb, _NT, preferred_element_type=F32) * mask_ref[h]
        ob_s[:, hs] = jnp.dot(sc.astype(BF16), v_ref[:, hs], preferred_element_type=F32)
        qr_s[:, hs] = qb.astype(F32) * qd_ref[h]
        kdt_s[h] = (kb.astype(F32) * kd_ref[h]).T

    state_copy(step).wait()
    r0_ref = r0_buf.at[step % STATE_RING]
    lane = lax.broadcasted_iota(jnp.int32, (HEAD_DIM, RET_TILE), 1)
    for s in range(n_seq):
        rows = slice(s * rows_per_seq, (s + 1) * rows_per_seq)
        in_seq = (lane >= s * rows_per_seq) & (lane < (s + 1) * rows_per_seq)
        for h in range(RET_HEADS):
            hs = slice(h * HEAD_DIM, (h + 1) * HEAD_DIM)
            r_old = r0_ref[s, h]
            ob_s[rows, hs] += jnp.dot(qr_s[rows, hs].astype(BF16), r_old.astype(BF16),
                                      preferred_element_type=F32)
            kdt = jnp.where(in_seq, kdt_s[h], 0.0).astype(BF16)
            rn_ref[s, h] = r_old * cd_ref[h] + jnp.dot(kdt, v_ref[:, hs], preferred_element_type=F32)

    for h in range(RET_HEADS):
        hs = slice(h * HEAD_DIM, (h + 1) * HEAD_DIM)
        o_ref[:, hs] = _norm_gate(ob_s[:, hs], g_ref[:, hs].astype(F32),
                                  gnw_ref[:, hs]).astype(o_ref.dtype)


def _decay_tables(chunk, n_seq):
    scale = HEAD_DIM ** -0.5
    log_gamma = np.log(1.0 - 2.0 ** (-5.0 - np.arange(RET_HEADS, dtype=np.float64)))
    idx = np.arange(chunk, dtype=np.float64)
    diff = idx[:, None] - idx[None, :]
    mask = np.where(diff >= 0, np.exp(log_gamma[:, None, None] * np.maximum(diff, 0.0)), 0.0)
    q_decay = np.exp(log_gamma[:, None] * (idx + 1.0))
    k_decay = np.exp(log_gamma[:, None] * (chunk - 1.0 - idx)) * scale
    chunk_decay = np.exp(log_gamma * chunk)
    mask_t = scale * np.einsum("hab,st->hsatb", mask, np.eye(n_seq)).reshape(
        RET_HEADS, RET_TILE, RET_TILE)
    qd_t = np.broadcast_to(np.tile(q_decay, (1, n_seq))[:, :, None], (RET_HEADS, RET_TILE, HEAD_DIM))
    kd_t = np.broadcast_to(np.tile(k_decay, (1, n_seq))[:, :, None], (RET_HEADS, RET_TILE, HEAD_DIM))
    const = lambda a: jnp.asarray(np.ascontiguousarray(a), dtype=F32)
    return const(mask_t), const(qd_t), const(kd_t), const(chunk_decay)


def _retention(proj, r0, gn_w, n_seq, seq_len, chunks=8):
    first, t = proj.shape[0] - 4, proj.shape[1]
    const3 = lambda *_: (0, 0, 0)
    if r0 is None:
        tile, tile_seqs = chunks * RET_TILE, 1
        steps = seq_len // tile
        grid = (n_seq, steps)
        row = lambda n, c: n * steps + c
        state_map = lambda n, c: (n, 0, 0, 0)
        state_block = (1, RET_HEADS, HEAD_DIM, HEAD_DIM)
        sem = ("arbitrary", "arbitrary")
        body = functools.partial(_ret_seq_kernel, chunks=chunks)
        scratch = [pltpu.VMEM((RET_HEADS, HEAD_DIM, HEAD_DIM), F32),
                   pltpu.VMEM((chunks, RET_HEADS, RET_TILE, RET_TILE), BF16),
                   pltpu.VMEM((chunks, RET_HEADS, HEAD_DIM, HEAD_DIM), F32)]
        name = "ret_seq"
    else:
        tile, tile_seqs = RET_TILE, RET_TILE // seq_len
        grid = (t // tile,)
        row = lambda i: i
        state_map = lambda i: (i, 0, 0, 0)
        state_block = (tile_seqs, RET_HEADS, HEAD_DIM, HEAD_DIM)
        sem = ("arbitrary",)
        body = functools.partial(_ret_step_kernel, n_seq=tile_seqs)
        assert grid[0] >= STATE_RING - 1
        scratch = [pltpu.VMEM((RET_TILE, RET_WIDTH), F32),
                   pltpu.VMEM((RET_HEADS, HEAD_DIM, RET_TILE), F32),
                   pltpu.VMEM((RET_TILE, RET_WIDTH), F32),
                   pltpu.VMEM((STATE_RING,) + state_block, F32),
                   pltpu.SemaphoreType.DMA((STATE_RING,))]
        name = "ret_step"
    mask_t, qd_t, kd_t, cd = _decay_tables(RET_TILE // tile_seqs, tile_seqs)

    def col(cb):
        return pl.BlockSpec((None, tile, RET_WIDTH), lambda *a: (first + cb, row(*a), 0))

    in_specs = [
        col(0), col(1), col(2), col(3),
        pl.BlockSpec((RET_HEADS, RET_TILE, RET_TILE), const3),
        pl.BlockSpec((RET_HEADS, RET_TILE, HEAD_DIM), const3),
        pl.BlockSpec((RET_HEADS, RET_TILE, HEAD_DIM), const3),
        pl.BlockSpec(memory_space=pltpu.SMEM),
        pl.BlockSpec((1, RET_WIDTH), lambda *_: (0, 0)),
    ]
    args = [proj, proj, proj, proj, mask_t, qd_t, kd_t, cd, gn_w]
    if r0 is not None:
        in_specs.append(pl.BlockSpec(memory_space=pl.ANY))
        args.append(r0)
    return pl.pallas_call(
        body,
        grid=grid,
        in_specs=in_specs,
        out_specs=(pl.BlockSpec((tile, RET_WIDTH), lambda *a: (row(*a), 0)),
                   pl.BlockSpec(state_block, state_map)),
        out_shape=(jax.ShapeDtypeStruct((t, RET_WIDTH), BF16),
                   jax.ShapeDtypeStruct((n_seq, RET_HEADS, HEAD_DIM, HEAD_DIM), F32)),
        scratch_shapes=scratch,
        compiler_params=_params(*sem),
        name=name,
    )(*args)


def _outproj_kernel(x_ref, a_ref, b_ref, wa_ref, wb_ref, o_ref):
    o_ref[...] = (x_ref[...]
                  + jnp.dot(a_ref[...], wa_ref[...], preferred_element_type=F32)
                  + jnp.dot(b_ref[...], wb_ref[...], preferred_element_type=F32))


def _outproj(x2d, a, b, w, tm=512):
    t = x2d.shape[0]
    resident = pl.Buffered(1)
    return pl.pallas_call(
        _outproj_kernel,
        grid=(t // tm,),
        in_specs=[
            pl.BlockSpec((tm, D_MODEL), lambda i: (i, 0)),
            pl.BlockSpec((tm, S5_WIDTH), lambda i: (i, 0)),
            pl.BlockSpec((tm, RET_WIDTH), lambda i: (i, 0)),
            pl.BlockSpec((S5_WIDTH, D_MODEL), lambda i: (0, 0), pipeline_mode=resident),
            pl.BlockSpec((RET_WIDTH, D_MODEL), lambda i: (1, 0), pipeline_mode=resident),
        ],
        out_specs=pl.BlockSpec((tm, D_MODEL), lambda i: (i, 0)),
        out_shape=jax.ShapeDtypeStruct((t, D_MODEL), F32),
        compiler_params=_params("arbitrary"),
        name="out_proj",
    )(x2d, a, b, w, w)


def _ffn_kernel(x_ref, gn_ref, wg_ref, wu_ref, wo_ref, gf_ref, o_ref, h_scr):
    j = pl.program_id(1)
    last_j = pl.num_programs(1) - 1

    def step(first, last):
        for r in range(x_ref.shape[0] // FFN_ROWS):
            rows = slice(r * FFN_ROWS, (r + 1) * FFN_ROWS)
            if first:
                h_scr[rows, :] = _rms(x_ref[rows, :], gn_ref[...]).astype(BF16)
            h = h_scr[rows, :]
            gate = jnp.dot(h, wg_ref[...], preferred_element_type=F32)
            up = jnp.dot(h, wu_ref[...], preferred_element_type=F32)
            act = (jax.nn.silu(gate) * up).astype(BF16)
            acc = (x_ref if first else o_ref)[rows, :] + jnp.dot(act, wo_ref[...],
                                                                  preferred_element_type=F32)
            o_ref[rows, :] = _rms(acc, gf_ref[...]) if last else acc

    pl.when(j == 0)(lambda: step(True, False))
    pl.when((j > 0) & (j < last_j))(lambda: step(False, False))
    pl.when(j == last_j)(lambda: step(False, True))


def _ffn(x2d, g_ffn, w_in, w_out, g_final, tm=1024, tf=512):
    t = x2d.shape[0]
    nf = D_FF // tf
    return pl.pallas_call(
        _ffn_kernel,
        grid=(t // tm, nf),
        in_specs=[
            pl.BlockSpec((tm, D_MODEL), lambda i, j: (i, 0)),
            pl.BlockSpec((1, D_MODEL), lambda i, j: (0, 0)),
            pl.BlockSpec((D_MODEL, tf), lambda i, j: (0, j)),
            pl.BlockSpec((D_MODEL, tf), lambda i, j: (0, j + nf)),
            pl.BlockSpec((tf, D_MODEL), lambda i, j: (j, 0)),
            pl.BlockSpec((1, D_MODEL), lambda i, j: (0, 0)),
        ],
        out_specs=pl.BlockSpec((tm, D_MODEL), lambda i, j: (i, 0)),
        out_shape=jax.ShapeDtypeStruct((t, D_MODEL), F32),
        scratch_shapes=[pltpu.VMEM((tm, D_MODEL), BF16)],
        compiler_params=_params("arbitrary", "arbitrary"),
        name="ffn",
    )(x2d, g_ffn, w_in, w_in, w_out, g_final)


def _finish(x2d, proj, s5, ret_state, w, n, l):
    s5_out, s5_re, s5_im = s5
    ret_out, ret_new = _retention(proj, ret_state, w["gn_w"], n, l)
    x1 = _outproj(x2d, s5_out, ret_out, w["w_out"])
    y = _ffn(x1, w["norm_ffn"], w["w_ffn_in"], w["w_ffn_out"], w["norm_final"])
    return (y.reshape(n, l, D_MODEL), s5_re.reshape(n, S5_GROUPS, S5_STATE),
            s5_im.reshape(n, S5_GROUPS, S5_STATE), ret_new)


def kernel(x_prompt, x_sample, state_s5_re, state_s5_im, state_ret, norm_mix, w_in, s5_lambda_re, s5_lambda_im, s5_log_step, s5_b_re, s5_b_im, s5_c_re, s5_c_im, s5_d, s5_w_glu, s5_b_glu, ret_gn_w, w_out, norm_ffn, w_ffn_in, w_ffn_out, norm_final):
    assert norm_mix.shape[0] == 1, "single-layer stack"
    pairs, wb, cd = _s5_params(s5_lambda_re[0], s5_lambda_im[0], s5_log_step[0],
                               s5_b_re[0], s5_b_im[0], s5_c_re[0], s5_c_im[0])
    w = dict(gn_w=ret_gn_w, norm_ffn=norm_ffn, norm_final=norm_final.reshape(1, D_MODEL))
    s5_w = (pairs, wb, cd, s5_d)
    n_s, l_s, _ = x_sample.shape
    n_p, l_p, _ = x_prompt.shape
    xs2d = x_sample.reshape(n_s * l_s, D_MODEL)
    xp2d = x_prompt.reshape(n_p * l_p, D_MODEL)

    proj_s, w_in_b, u_s = _inproj_cast(xs2d, norm_mix, w_in[0], l_s, float(PAST_LEN))
    proj_p, u_p, w["w_out"], w["w_ffn_in"], w["w_ffn_out"] = _inproj_seq(
        xp2d, norm_mix, w_in_b, n_p, 0.0, (w_out[0], w_ffn_in[0], w_ffn_out[0]))

    x0 = (state_s5_re[0].reshape(n_s, N_STATE), state_s5_im[0].reshape(n_s, N_STATE))
    *s5_s, w_glu_b = _s5_step(u_s, x0, *s5_w, s5_w_glu[0], s5_b_glu)
    ys, s_re, s_im, s_ret = _finish(xs2d, proj_s, s5_s, state_ret[0], w, n_s, l_s)

    s5_p = _s5_seq(u_p, *s5_w, w_glu_b, s5_b_glu, n_p, l_p)
    yp, p_re, p_im, p_ret = _finish(xp2d, proj_p, s5_p, None, w, n_p, l_p)
    return (yp, ys, p_re[None], p_im[None], p_ret[None], s_re[None], s_im[None], s_ret[None])
```
